```python
import math
import jax, jax.numpy as jnp
from jax import lax
import numpy as np

D_MODEL = 1024
BATCH = 8
SEQ = 4096
DEPTH = 2

N_A = DEPTH // 2
N_B = DEPTH - N_A

MLA_HEADS = 8
MLA_Q_LORA = 256
MLA_KV_LORA = 128
MLA_NOPE = 128
MLA_ROPE = 64
MLA_V = 128
MLA_Q_BLOCK = 128

MOBA_HEADS = 8
MOBA_HEAD_DIM = D_MODEL // MOBA_HEADS
MOBA_BLOCK = 256
MOBA_TOPK = 3
MOBA_Q_CHUNK = 16

N_EXPERTS = 32
TOP_K = 4
D_FF = D_MODEL
SWIGLU_LIMIT = 7.0
SWIGLU_ALPHA = 1.702

ROPE_THETA = 10000.0
NORM_EPS = 1e-6
NEG_INF = -1e30

kernel_name = 'hybrid_mla_moba_moe_adaln'


def rms_norm(x, g):
    xf = x.astype(jnp.float32)
    y = xf * lax.rsqrt(jnp.mean(xf * xf, axis=-1, keepdims=True) + NORM_EPS)
    return (y * g.astype(jnp.float32)).astype(x.dtype)


def modulate(h, shift, scale):
    return h * (1.0 + scale[:, None, :]) + shift[:, None, :]


def rope(x, positions):
    d = x.shape[-1]
    half = d // 2
    inv_freq = ROPE_THETA ** (-jnp.arange(half, dtype=jnp.float32) * (2.0 / d))
    ang = positions.astype(jnp.float32)[:, :, None] * inv_freq
    cos = jnp.cos(ang)[:, :, None, :]
    sin = jnp.sin(ang)[:, :, None, :]
    xf = x.astype(jnp.float32)
    x1, x2 = xf[..., :half], xf[..., half:]
    return jnp.concatenate([x1 * cos - x2 * sin, x2 * cos + x1 * sin], axis=-1).astype(x.dtype)


def causal_dense_attention(q, k, v, scale):
    B, S, H, dq = q.shape
    dv = v.shape[-1]
    nq = S // MLA_Q_BLOCK
    qb = q.reshape(B, nq, MLA_Q_BLOCK, H, dq).transpose(1, 0, 2, 3, 4)
    key_pos = jnp.arange(S)

    def one_block(args):
        i, q_blk = args
        s = jnp.einsum('bqhd,bkhd->bhqk', q_blk, k, preferred_element_type=jnp.float32) * scale
        q_pos = i * MLA_Q_BLOCK + jnp.arange(MLA_Q_BLOCK)
        s = jnp.where(key_pos[None, :] <= q_pos[:, None], s, NEG_INF)
        p = jax.nn.softmax(s, axis=-1).astype(v.dtype)
        return jnp.einsum('bhqk,bkhd->bqhd', p, v)

    out = lax.map(one_block, (jnp.arange(nq), qb))
    return out.transpose(1, 0, 2, 3, 4).reshape(B, S, H, dv)


def mla_attention(h, positions, w_in, q_norm_g, w_uq, kv_norm_g, w_ukv, w_o):
    B, S, _ = h.shape
    H = MLA_HEADS
    proj = h @ w_in
    c_q, c_kv, k_rope = jnp.split(proj, [MLA_Q_LORA, MLA_Q_LORA + MLA_KV_LORA], axis=-1)
    q = (rms_norm(c_q, q_norm_g) @ w_uq).reshape(B, S, H, MLA_NOPE + MLA_ROPE)
    q = jnp.concatenate([q[..., :MLA_NOPE], rope(q[..., MLA_NOPE:], positions)], axis=-1)
    kv = (rms_norm(c_kv, kv_norm_g) @ w_ukv).reshape(B, S, H, MLA_NOPE + MLA_V)
    k_nope, v = kv[..., :MLA_NOPE], kv[..., MLA_NOPE:]
    k_rope = rope(k_rope[:, :, None, :], positions)
    k = jnp.concatenate([k_nope, jnp.broadcast_to(k_rope, (B, S, H, MLA_ROPE))], axis=-1)
    out = causal_dense_attention(q, k, v, (MLA_NOPE + MLA_ROPE) ** -0.5)
    return out.reshape(B, S, H * MLA_V) @ w_o


def moba_shared_kv(h_kv, positions, w_kv):
    B, S, _ = h_kv.shape
    H, HD = MOBA_HEADS, MOBA_HEAD_DIM
    kv = (h_kv @ w_kv).reshape(B, S, 2, H, HD)
    k = rope(kv[:, :, 0], positions)
    v = kv[:, :, 1]
    nb = -(-S // MOBA_BLOCK)
    pad = nb * MOBA_BLOCK - S
    k = jnp.pad(k, ((0, 0), (0, pad), (0, 0), (0, 0)))
    v = jnp.pad(v, ((0, 0), (0, pad), (0, 0), (0, 0)))
    k_blocks = k.reshape(B, nb, MOBA_BLOCK, H, HD).transpose(0, 3, 1, 2, 4)
    v_blocks = v.reshape(B, nb, MOBA_BLOCK, H, HD).transpose(0, 3, 1, 2, 4)
    k_mean = jnp.mean(k_blocks.astype(jnp.float32), axis=3).astype(k.dtype)
    return k_blocks, v_blocks, k_mean


def moba_attention(h, positions, w_q, w_o, k_blocks, v_blocks, k_mean):
    B, S, _ = h.shape
    H, HD, BLK, QC = MOBA_HEADS, MOBA_HEAD_DIM, MOBA_BLOCK, MOBA_Q_CHUNK
    nb = k_blocks.shape[2]
    topk = min(MOBA_TOPK, nb)
    scale = HD ** -0.5
    q = rope((h @ w_q).reshape(B, S, H, HD), positions)
    nc = S // QC
    qc = q.reshape(B, nc, QC, H, HD).transpose(1, 0, 3, 2, 4)
    b_idx = jnp.arange(B)[:, None, None, None]
    h_idx = jnp.arange(H)[None, :, None, None]

    def one_chunk(args):
        ci, q_c = args
        start = ci * QC
        own = start // BLK
        gate = jnp.einsum('bhqd,bhnd->bhqn', q_c, k_mean, preferred_element_type=jnp.float32)
        gate = jnp.where(jnp.arange(nb) < own, gate, -jnp.inf)
        _, sel = lax.top_k(gate, topk)
        sel_valid = jnp.arange(topk) < own
        k_sel = k_blocks[b_idx, h_idx, sel]
        v_sel = v_blocks[b_idx, h_idx, sel]
        s_sel = jnp.einsum('bhqd,bhqnkd->bhqnk', q_c, k_sel, preferred_element_type=jnp.float32) * scale
        s_sel = jnp.where(sel_valid[:, None], s_sel, NEG_INF).reshape(B, H, QC, topk * BLK)
        k_own = lax.dynamic_index_in_dim(k_blocks, own, axis=2, keepdims=False)
        v_own = lax.dynamic_index_in_dim(v_blocks, own, axis=2, keepdims=False)
        s_own = jnp.einsum('bhqd,bhkd->bhqk', q_c, k_own, preferred_element_type=jnp.float32) * scale
        q_pos = start + jnp.arange(QC)
        k_pos = own * BLK + jnp.arange(BLK)
        s_own = jnp.where(k_pos[None, :] <= q_pos[:, None], s_own, NEG_INF)
        p = jax.nn.softmax(jnp.concatenate([s_sel, s_own], axis=-1), axis=-1).astype(v_blocks.dtype)
        p_sel = p[..., :topk * BLK].reshape(B, H, QC, topk, BLK)
        p_own = p[..., topk * BLK:]
        return (jnp.einsum('bhqnk,bhqnkd->bhqd', p_sel, v_sel)
                + jnp.einsum('bhqk,bhkd->bhqd', p_own, v_own))

    out = lax.map(one_chunk, (jnp.arange(nc), qc))
    out = out.transpose(1, 0, 3, 2, 4).reshape(B, S, H * HD)
    return out @ w_o


def moe_ffn(h, router_w, router_b, w_gate, b_gate, w_up, b_up, w_down, b_down):
    B, S, D = h.shape
    t = h.reshape(B * S, D)
    logits = (t @ router_w + router_b).astype(jnp.float32)
    top_vals, top_idx = lax.top_k(logits, TOP_K)
    top_w = jax.nn.softmax(top_vals, axis=-1)
    gates = jnp.sum(jax.nn.one_hot(top_idx, N_EXPERTS, dtype=jnp.float32) * top_w[..., None], axis=-2)
    out = jnp.zeros((B * S, D), jnp.float32)
    for e in range(N_EXPERTS):
        g = jnp.minimum(t @ w_gate[e] + b_gate[e], SWIGLU_LIMIT)
        u = jnp.clip(t @ w_up[e] + b_up[e], -SWIGLU_LIMIT, SWIGLU_LIMIT)
        a = g * jax.nn.sigmoid(SWIGLU_ALPHA * g) * (u + 1.0)
        out = out + gates[:, e:e + 1] * (a @ w_down[e] + b_down[e])
    return out.astype(h.dtype).reshape(B, S, D)


def setup_inputs(seed: int = 0) -> dict:
    key = jax.random.key(seed)
    keys = iter(jax.random.split(key, 40))

    def nrm(shape, scale):
        return jax.random.normal(next(keys), shape, jnp.float32) * scale

    def gain(shape):
        return 1.0 + nrm(shape, 0.05)

    D = D_MODEL
    HM = MLA_HEADS
    HB = MOBA_HEADS * MOBA_HEAD_DIM
    x = nrm((BATCH, SEQ, D), 1.0)
    c = nrm((BATCH, D), 1.0)
    positions = jnp.tile(jnp.arange(SEQ, dtype=jnp.int32)[None, :], (BATCH, 1))
    return {
        'x': x,
        'c': c,
        'positions': positions,
        'ada_w': nrm((DEPTH, D, 6 * D), 0.5 * D ** -0.5),
        'ada_b': nrm((DEPTH, 6 * D), 0.02),
        'norm_attn_g': gain((DEPTH, D)),
        'norm_ffn_g': gain((DEPTH, D)),
        'mla_w_in': nrm((N_A, D, MLA_Q_LORA + MLA_KV_LORA + MLA_ROPE), D ** -0.5),
        'mla_q_norm_g': gain((N_A, MLA_Q_LORA)),
        'mla_w_uq': nrm((N_A, MLA_Q_LORA, HM * (MLA_NOPE + MLA_ROPE)), MLA_Q_LORA ** -0.5),
        'mla_kv_norm_g': gain((N_A, MLA_KV_LORA)),
        'mla_w_ukv': nrm((N_A, MLA_KV_LORA, HM * (MLA_NOPE + MLA_V)), MLA_KV_LORA ** -0.5),
        'mla_w_o': nrm((N_A, HM * MLA_V, D), (HM * MLA_V) ** -0.5),
        'kv_ada_w': nrm((D, 2 * D), 0.5 * D ** -0.5),
        'kv_ada_b': nrm((2 * D,), 0.02),
        'kv_norm_g': gain((D,)),
        'moba_w_kv': nrm((D, 2 * HB), D ** -0.5),
        'moba_w_q': nrm((N_B, D, HB), D ** -0.5),
        'moba_w_o': nrm((N_B, HB, D), HB ** -0.5),
        'router_w': nrm((DEPTH, D, N_EXPERTS), D ** -0.5),
        'router_b': nrm((DEPTH, N_EXPERTS), 0.01),
        'w_gate': nrm((DEPTH, N_EXPERTS, D, D_FF), D ** -0.5),
        'b_gate': nrm((DEPTH, N_EXPERTS, D_FF), 0.02),
        'w_up': nrm((DEPTH, N_EXPERTS, D, D_FF), D ** -0.5),
        'b_up': nrm((DEPTH, N_EXPERTS, D_FF), 0.02),
        'w_down': nrm((DEPTH, N_EXPERTS, D_FF, D), D_FF ** -0.5),
        'b_down': nrm((DEPTH, N_EXPERTS, D), 0.02),
        'final_ada_w': nrm((D, 2 * D), 0.5 * D ** -0.5),
        'final_ada_b': nrm((2 * D,), 0.02),
        'final_norm_g': gain((D,)),
    }


def reference(x, c, positions, ada_w, ada_b, norm_attn_g, norm_ffn_g,
              mla_w_in, mla_q_norm_g, mla_w_uq, mla_kv_norm_g, mla_w_ukv, mla_w_o,
              kv_ada_w, kv_ada_b, kv_norm_g, moba_w_kv, moba_w_q, moba_w_o,
              router_w, router_b, w_gate, b_gate, w_up, b_up, w_down, b_down,
              final_ada_w, final_ada_b, final_norm_g):
    c_act = jax.nn.silu(c)
    shared = None
    for layer in range(DEPTH):
        mod = c_act @ ada_w[layer] + ada_b[layer]
        sh_a, sc_a, g_a, sh_f, sc_f, g_f = jnp.split(mod, 6, axis=-1)
        h = modulate(rms_norm(x, norm_attn_g[layer]), sh_a, sc_a)
        if layer < N_A:
            a = mla_attention(h, positions, mla_w_in[layer], mla_q_norm_g[layer], mla_w_uq[layer],
                              mla_kv_norm_g[layer], mla_w_ukv[layer], mla_w_o[layer])
        else:
            if shared is None:
                kv_shift, kv_scale = jnp.split(c_act @ kv_ada_w + kv_ada_b, 2, axis=-1)
                h_kv = modulate(rms_norm(x, kv_norm_g), kv_shift, kv_scale)
                shared = moba_shared_kv(h_kv, positions, moba_w_kv)
            j = layer - N_A
            a = moba_attention(h, positions, moba_w_q[j], moba_w_o[j], shared[0], shared[1], shared[2])
        x = x + g_a[:, None, :] * a
        h = modulate(rms_norm(x, norm_ffn_g[layer]), sh_f, sc_f)
        f = moe_ffn(h, router_w[layer], router_b[layer], w_gate[layer], b_gate[layer],
                    w_up[layer], b_up[layer], w_down[layer], b_down[layer])
        x = x + g_f[:, None, :] * f
    f_shift, f_scale = jnp.split(c_act @ final_ada_w + final_ada_b, 2, axis=-1)
    return modulate(rms_norm(x, final_norm_g), f_shift, f_scale)
```

```python
import functools

import jax
import jax.numpy as jnp
from jax import lax
from jax.experimental import pallas as pl
from jax.experimental.pallas import tpu as pltpu

D_MODEL = 1024
DEPTH = 2
N_A = DEPTH // 2

MLA_HEADS = 8
MLA_Q_LORA = 256
MLA_KV_LORA = 128
MLA_NOPE = 128
MLA_ROPE = 64
MLA_V = 128

MOBA_HEADS = 8
MOBA_HEAD_DIM = D_MODEL // MOBA_HEADS
MOBA_BLOCK = 256
MOBA_TOPK = 3

N_EXPERTS = 32
TOP_K = 4
SWIGLU_LIMIT = 7.0
SWIGLU_ALPHA = 1.702

ROPE_THETA = 10000.0
NORM_EPS = 1e-6
NEG_INF = -1e30

LANES = 128
TOKEN_TILE = 512
ATTN_Q_TILE = 512
MLA_KV_TILE = 512
MLA_QK_PAD = 2 * LANES

F32 = jnp.float32
BF16 = jnp.bfloat16


def _rms_scale(x):
    return lax.rsqrt(jnp.mean(x * x, axis=-1, keepdims=True) + NORM_EPS)


def _nt_dot(a, b):
    return lax.dot_general(a, b, (((1,), (1,)), ((), ())), preferred_element_type=F32)


def _ada_kernel(c_ref, w_ref, b_ref, o_ref):
    c = c_ref[...]
    ca = c * jax.nn.sigmoid(c)
    o_ref[...] = jnp.dot(ca, w_ref[...], precision=lax.Precision.HIGHEST,
                         preferred_element_type=F32) + b_ref[...]


def _ada_linear(c, w, b):
    L, D, M = w.shape
    B = c.shape[0]
    bn = 1024
    return pl.pallas_call(
        _ada_kernel,
        out_shape=jax.ShapeDtypeStruct((L, B, M), F32),
        grid=(L, M // bn),
        in_specs=[
            pl.BlockSpec((B, D), lambda l, j: (0, 0)),
            pl.BlockSpec((None, D, bn), lambda l, j: (l, 0, j)),
            pl.BlockSpec((None, 1, bn), lambda l, j: (l, 0, j)),
        ],
        out_specs=pl.BlockSpec((None, B, bn), lambda l, j: (l, 0, j)),
        name="ada_linear",
    )(c, w, b.reshape(L, 1, M))


def _rope_table_kernel(pos_ref, inv_a_ref, inv_b_ref, sign_ref, ca_ref, sa_ref, cb_ref, sb_ref):
    pos = pos_ref[...]
    ang_a = pos * inv_a_ref[...]
    ang_b = pos * inv_b_ref[...]
    ca_ref[...] = jnp.cos(ang_a)
    sa_ref[...] = jnp.sin(ang_a)
    cb_ref[...] = jnp.cos(ang_b)
    sb_ref[...] = jnp.sin(ang_b) * sign_ref[...]


def _rope_tables(positions):
    B, S = positions.shape
    t = 1024

    def inv_freq(d):
        half = d // 2
        return ROPE_THETA ** (-jnp.arange(half, dtype=F32) * (2.0 / d))

    inv_a = jnp.tile(inv_freq(MLA_ROPE), LANES // (MLA_ROPE // 2)).reshape(1, LANES)
    inv_b = jnp.tile(inv_freq(MOBA_HEAD_DIM), 2).reshape(1, LANES)
    half = MOBA_HEAD_DIM // 2
    sign = jnp.concatenate([-jnp.ones((half,), F32), jnp.ones((half,), F32)]).reshape(1, LANES)
    pos = positions.astype(F32).reshape(B, S, 1)
    row = pl.BlockSpec((1, LANES), lambda b, i: (0, 0))
    out = pl.BlockSpec((None, t, LANES), lambda b, i: (b, i, 0))
    shp = jax.ShapeDtypeStruct((B, S, LANES), F32)
    return pl.pallas_call(
        _rope_table_kernel,
        out_shape=(shp, shp, shp, shp),
        grid=(B, S // t),
        in_specs=[pl.BlockSpec((None, t, 1), lambda b, i: (b, i, 0)), row, row, row],
        out_specs=(out, out, out, out),
        name="rope_tables",
    )(pos, inv_a, inv_b, sign)


def _mla_proj_kernel(x_ref, sh_ref, sc_ref, g_ref, win_ref, qg_ref, wqn_ref, wqr_ref, wqrr_ref,
                     kvg_ref, wkn_ref, wv_ref, cos_ref, sin_ref, q_ref, k_ref, v_ref):
    x = x_ref[...]
    y = x * _rms_scale(x) * g_ref[...]
    h = (y * (1.0 + sc_ref[...]) + sh_ref[...]).astype(BF16)
    proj = jnp.dot(h, win_ref[...], preferred_element_type=F32)
    c_q = proj[:, :MLA_Q_LORA]
    c_kv = proj[:, MLA_Q_LORA:MLA_Q_LORA + MLA_KV_LORA]
    kr_a = proj[:, 384:512]
    kr_b = proj[:, 512:640]
    cq = (c_q * _rms_scale(c_q) * qg_ref[...]).astype(BF16)
    ckv = (c_kv * _rms_scale(c_kv) * kvg_ref[...]).astype(BF16)
    cos = cos_ref[...]
    sin = sin_ref[...]
    scale = (MLA_NOPE + MLA_ROPE) ** -0.5
    kr = (kr_a * cos + kr_b * sin).astype(BF16)
    q_nope = jnp.dot(cq, wqn_ref[...], preferred_element_type=F32) * scale
    q_ra = jnp.dot(cq, wqr_ref[...], preferred_element_type=F32)
    q_rb = jnp.dot(cq, wqrr_ref[...], preferred_element_type=F32)
    k_nope = jnp.dot(ckv, wkn_ref[...], preferred_element_type=F32)
    v_ref[...] = jnp.dot(ckv, wv_ref[...], preferred_element_type=F32).astype(BF16)
    for hh in range(MLA_HEADS):
        hs = slice(hh * LANES, (hh + 1) * LANES)
        lo = slice(hh * MLA_QK_PAD, hh * MLA_QK_PAD + LANES)
        hi = slice(hh * MLA_QK_PAD + LANES, (hh + 1) * MLA_QK_PAD)
        q_ref[:, lo] = q_nope[:, hs].astype(BF16)
        q_ref[:, hi] = ((q_ra[:, hs] * cos + q_rb[:, hs] * sin) * scale).astype(BF16)
        k_ref[:, lo] = k_nope[:, hs].astype(BF16)
        k_ref[:, hi] = kr


def _rot_half_cols(w, half):
    return jnp.concatenate([-w[..., half:], w[..., :half]], axis=-1)


def _mla_proj(x, sh, sc, g, w_in, q_norm_g, w_uq, kv_norm_g, w_ukv, cos, sin):
    B, S, D = x.shape
    H = MLA_HEADS
    t = TOKEN_TILE
    half = MLA_ROPE // 2
    w_kr = w_in[:, MLA_Q_LORA + MLA_KV_LORA:]
    zpad = jnp.zeros((D, LANES - MLA_ROPE), F32)
    w_in_ext = jnp.concatenate(
        [w_in[:, :MLA_Q_LORA + MLA_KV_LORA], w_kr, zpad, _rot_half_cols(w_kr, half), zpad],
        axis=1).astype(BF16)
    wq = w_uq.reshape(MLA_Q_LORA, H, MLA_NOPE + MLA_ROPE)
    wq_nope = wq[..., :MLA_NOPE].reshape(MLA_Q_LORA, H * MLA_NOPE).astype(BF16)
    wq_r = wq[..., MLA_NOPE:]
    pad = ((0, 0), (0, 0), (0, LANES - MLA_ROPE))
    wq_rope = jnp.pad(wq_r, pad).reshape(MLA_Q_LORA, H * LANES).astype(BF16)
    wq_rope_rot = jnp.pad(_rot_half_cols(wq_r, half), pad).reshape(MLA_Q_LORA, H * LANES).astype(BF16)
    wkv = w_ukv.reshape(MLA_KV_LORA, H, MLA_NOPE + MLA_V)
    wk_nope = wkv[..., :MLA_NOPE].reshape(MLA_KV_LORA, H * MLA_NOPE).astype(BF16)
    wv = wkv[..., MLA_NOPE:].reshape(MLA_KV_LORA, H * MLA_V).astype(BF16)

    tok = lambda w: pl.BlockSpec((None, t, w), lambda b, i: (b, i, 0))
    vec = lambda w: pl.BlockSpec((None, 1, w), lambda b, i: (b, 0, 0))
    full = lambda a: pl.BlockSpec(a.shape, lambda b, i: (0,) * a.ndim)
    g2 = g.reshape(1, D)
    qg2 = q_norm_g.reshape(1, MLA_Q_LORA)
    kvg2 = kv_norm_g.reshape(1, MLA_KV_LORA)
    return pl.pallas_call(
        _mla_proj_kernel,
        out_shape=(jax.ShapeDtypeStruct((B, S, H * MLA_QK_PAD), BF16),
                   jax.ShapeDtypeStruct((B, S, H * MLA_QK_PAD), BF16),
                   jax.ShapeDtypeStruct((B, S, H * MLA_V), BF16)),
        grid=(B, S // t),
        in_specs=[tok(D), vec(D), vec(D), full(g2), full(w_in_ext), full(qg2), full(wq_nope),
                  full(wq_rope), full(wq_rope_rot), full(kvg2), full(wk_nope), full(wv),
                  tok(LANES), tok(LANES)],
        out_specs=(tok(H * MLA_QK_PAD), tok(H * MLA_QK_PAD), tok(H * MLA_V)),
        name="mla_proj",
    )(x, sh, sc, g2, w_in_ext, qg2, wq_nope, wq_rope, wq_rope_rot, kvg2, wk_nope, wv, cos, sin)


def _attn_kernel(*refs, moba, tq, bk):
    if moba:
        q_ref, k_ref, v_ref, km_ref, o_ref, m_sc, l_sc, acc_sc = refs
    else:
        q_ref, k_ref, v_ref, o_ref, m_sc, l_sc, acc_sc = refs
    i = pl.program_id(2)
    q = q_ref[...]
    m_sc[...] = jnp.full(m_sc.shape, -jnp.inf, F32)
    l_sc[...] = jnp.zeros(l_sc.shape, F32)
    acc_sc[...] = jnp.zeros(acc_sc.shape, F32)
    qpos = i * tq + lax.broadcasted_iota(jnp.int32, (tq, 1), 0)

    if moba:
        nb = km_ref.shape[0]
        own = qpos // MOBA_BLOCK
        gate = _nt_dot(q, km_ref[...].astype(BF16))
        blk = lax.broadcasted_iota(jnp.int32, (tq, nb), 1)
        blk_f = blk.astype(F32)
        gate = jnp.where(blk < own, gate, -jnp.inf)
        sel = jnp.zeros((tq, nb), F32)
        for _ in range(MOBA_TOPK):
            mx = jnp.max(gate, axis=1, keepdims=True)
            first = jnp.min(jnp.where(gate == mx, blk_f, float(nb)), axis=1, keepdims=True)
            hit = blk_f == first
            sel = jnp.where(hit & (mx > -jnp.inf), 1.0, sel)
            gate = jnp.where(hit, -jnp.inf, gate)

    def body(n, carry):
        start = pl.multiple_of(n * bk, bk)
        k = k_ref[pl.ds(start, bk), :]
        v = v_ref[pl.ds(start, bk), :]
        s = _nt_dot(q, k)
        kpos = start + lax.broadcasted_iota(jnp.int32, (1, bk), 1)
        causal = kpos <= qpos
        if moba:
            sel_n = jnp.sum(jnp.where(blk == n, sel, 0.0), axis=1, keepdims=True)
            allowed = ((own == n) & causal) | ((sel_n > 0.0) & (n < own))
        else:
            allowed = causal
        s = jnp.where(allowed, s, NEG_INF)
        m_prev = m_sc[...]
        m_new = jnp.maximum(m_prev, jnp.max(s, axis=1, keepdims=True))
        p = jnp.where(allowed, jnp.exp(s - m_new), 0.0)
        alpha = jnp.exp(m_prev - m_new)
        l_sc[...] = alpha * l_sc[...] + jnp.sum(p, axis=1, keepdims=True)
        acc_sc[...] = alpha * acc_sc[...] + jnp.dot(p.astype(BF16), v, preferred_element_type=F32)
        m_sc[...] = m_new
        return carry

    lax.fori_loop(0, ((i + 1) * tq) // bk, body, 0)
    o_ref[...] = (acc_sc[...] / l_sc[...]).astype(o_ref.dtype)


def _attention(q, k, v, k_mean, *, heads, dk, dv, bk, moba):
    B, S, _ = q.shape
    tq = ATTN_Q_TILE
    in_specs = [
        pl.BlockSpec((None, tq, dk), lambda b, h, i: (b, i, h)),
        pl.BlockSpec((None, S, dk), lambda b, h, i: (b, 0, h)),
        pl.BlockSpec((None, S, dv), lambda b, h, i: (b, 0, h)),
    ]
    args = [q, k, v]
    if moba:
        nb = k_mean.shape[1]
        in_specs.append(pl.BlockSpec((None, nb, dk), lambda b, h, i: (b, 0, h)))
        args.append(k_mean)
    return pl.pallas_call(
        functools.partial(_attn_kernel, moba=moba, tq=tq, bk=bk),
        out_shape=jax.ShapeDtypeStruct((B, S, heads * dv), BF16),
        grid=(B, heads, S // tq),
        in_specs=in_specs,
        out_specs=pl.BlockSpec((None, tq, dv), lambda b, h, i: (b, i, h)),
        scratch_shapes=[pltpu.VMEM((tq, 1), F32), pltpu.VMEM((tq, 1), F32), pltpu.VMEM((tq, dv), F32)],
        name="moba_attention" if moba else "mla_attention",
    )(*args)


def _attn_out_kernel(a_ref, wo_ref, x_ref, ga_ref, g_ref, sh_ref, sc_ref, rw_ref, rb_ref,
                     x1_ref, h_ref, gates_ref):
    a = jnp.dot(a_ref[...], wo_ref[...], preferred_element_type=F32)
    x1 = x_ref[...] + ga_ref[...] * a
    x1_ref[...] = x1
    y = x1 * _rms_scale(x1) * g_ref[...]
    h = (y * (1.0 + sc_ref[...]) + sh_ref[...]).astype(BF16)
    h_ref[...] = h
    logits = jnp.dot(h, rw_ref[...], preferred_element_type=F32) + rb_ref[...]
    lane = lax.broadcasted_iota(jnp.int32, logits.shape, 1).astype(F32)
    logits = jnp.where(lane < float(N_EXPERTS), logits, -jnp.inf)
    top = None
    num = jnp.zeros(logits.shape, F32)
    den = jnp.zeros((logits.shape[0], 1), F32)
    for _ in range(TOP_K):
        mx = jnp.max(logits, axis=1, keepdims=True)
        first = jnp.min(jnp.where(logits == mx, lane, float(LANES)), axis=1, keepdims=True)
        hit = lane == first
        if top is None:
            top = mx
        w = jnp.exp(mx - top)
        num = jnp.where(hit, w, num)
        den = den + w
        logits = jnp.where(hit, -jnp.inf, logits)
    gates_ref[...] = num / den


def _attn_out(attn, w_o, x, g_a, norm_g, sh_f, sc_f, router_w, router_b):
    B, S, D = x.shape
    t = TOKEN_TILE
    rw = jnp.pad(router_w, ((0, 0), (0, LANES - N_EXPERTS))).astype(BF16)
    rb = jnp.pad(router_b, (0, LANES - N_EXPERTS)).reshape(1, LANES)
    wo = w_o.astype(BF16)
    g2 = norm_g.reshape(1, D)
    tok = lambda w: pl.BlockSpec((None, t, w), lambda b, i: (b, i, 0))
    vec = lambda w: pl.BlockSpec((None, 1, w), lambda b, i: (b, 0, 0))
    full = lambda a: pl.BlockSpec(a.shape, lambda b, i: (0,) * a.ndim)
    return pl.pallas_call(
        _attn_out_kernel,
        out_shape=(jax.ShapeDtypeStruct((B, S, D), F32),
                   jax.ShapeDtypeStruct((B, S, D), BF16),
                   jax.ShapeDtypeStruct((B, S, LANES), F32)),
        grid=(B, S // t),
        in_specs=[tok(attn.shape[-1]), full(wo), tok(D), vec(D), full(g2), vec(D), vec(D),
                  full(rw), full(rb)],
        out_specs=(tok(D), tok(D), tok(LANES)),
        name="attn_out_router",
    )(attn, wo, x, g_a, g2, sh_f, sc_f, rw, rb)


def _moe_kernel(h_ref, gates_ref, wg_ref, bg_ref, wu_ref, bu_ref, wd_ref, bd_ref, x_ref, gf_ref,
                o_ref, acc_ref):
    e = pl.program_id(2)

    @pl.when(e == 0)
    def _():
        acc_ref[...] = jnp.zeros(acc_ref.shape, F32)

    h = h_ref[...]
    g = jnp.minimum(jnp.dot(h, wg_ref[...], preferred_element_type=F32) + bg_ref[...], SWIGLU_LIMIT)
    u = jnp.clip(jnp.dot(h, wu_ref[...], preferred_element_type=F32) + bu_ref[...],
                 -SWIGLU_LIMIT, SWIGLU_LIMIT)
    a = g * jax.nn.sigmoid(SWIGLU_ALPHA * g) * (u + 1.0)
    y = jnp.dot(a.astype(BF16), wd_ref[...], preferred_element_type=F32) + bd_ref[...]
    gates = gates_ref[...]
    lane = lax.broadcasted_iota(jnp.int32, gates.shape, 1)
    gcol = jnp.sum(jnp.where(lane == e, gates, 0.0), axis=1, keepdims=True)
    acc_ref[...] += gcol * y

    @pl.when(e == N_EXPERTS - 1)
    def _():
        o_ref[...] = x_ref[...] + gf_ref[...] * acc_ref[...]


def _moe(h, gates, x, g_f, w_gate, b_gate, w_up, b_up, w_down, b_down):
    B, S, D = x.shape
    E, _, F = w_gate.shape
    t = TOKEN_TILE
    tok = lambda w: pl.BlockSpec((None, t, w), lambda b, i, e: (b, i, 0))
    vec = lambda w: pl.BlockSpec((None, 1, w), lambda b, i, e: (b, 0, 0))
    wspec = lambda r, c: pl.BlockSpec((None, r, c), lambda b, i, e: (e, 0, 0))
    return pl.pallas_call(
        _moe_kernel,
        out_shape=jax.ShapeDtypeStruct((B, S, D), F32),
        grid=(B, S // t, E),
        in_specs=[tok(D), tok(LANES), wspec(D, F), wspec(1, F), wspec(D, F), wspec(1, F),
                  wspec(F, D), wspec(1, D), tok(D), vec(D)],
        out_specs=tok(D),
        scratch_shapes=[pltpu.VMEM((t, D), F32)],
        name="moe_dense",
    )(h, gates, w_gate.astype(BF16), b_gate.reshape(E, 1, F), w_up.astype(BF16),
      b_up.reshape(E, 1, F), w_down.astype(BF16), b_down.reshape(E, 1, D), x, g_f)


def _moba_proj_kernel(x_ref, sha_ref, sca_ref, ga_ref, shk_ref, sck_ref, gk_ref, wq_ref, wkv_ref,
                      cos_ref, sin_ref, q_ref, k_ref, v_ref, km_ref):
    x = x_ref[...]
    xn = x * _rms_scale(x)
    h = ((xn * ga_ref[...]) * (1.0 + sca_ref[...]) + sha_ref[...]).astype(BF16)
    hkv = ((xn * gk_ref[...]) * (1.0 + sck_ref[...]) + shk_ref[...]).astype(BF16)
    q = jnp.dot(h, wq_ref[...], preferred_element_type=F32)
    kv = jnp.dot(hkv, wkv_ref[...], preferred_element_type=F32)
    cos = cos_ref[...]
    sin = sin_ref[...]
    scale = MOBA_HEAD_DIM ** -0.5
    hd = MOBA_HEAD_DIM
    t = x.shape[0]
    width = MOBA_HEADS * hd
    v_ref[...] = kv[:, width:].astype(BF16)
    for hh in range(MOBA_HEADS):
        hs = slice(hh * hd, (hh + 1) * hd)
        qh = q[:, hs]
        kh = kv[:, hs]
        q_ref[:, hs] = ((qh * cos + pltpu.roll(qh, hd // 2, 1) * sin) * scale).astype(BF16)
        kr = kh * cos + pltpu.roll(kh, hd // 2, 1) * sin
        k_ref[:, hs] = kr.astype(BF16)
        km_ref[:, hs] = jnp.mean(kr.reshape(t // MOBA_BLOCK, MOBA_BLOCK, hd), axis=1)


def _moba_proj(x, sh_a, sc_a, g_a, sh_k, sc_k, g_k, w_q, w_kv, cos, sin):
    B, S, D = x.shape
    t = TOKEN_TILE
    width = MOBA_HEADS * MOBA_HEAD_DIM
    per = t // MOBA_BLOCK
    tok = lambda w: pl.BlockSpec((None, t, w), lambda b, i: (b, i, 0))
    vec = lambda w: pl.BlockSpec((None, 1, w), lambda b, i: (b, 0, 0))
    full = lambda a: pl.BlockSpec(a.shape, lambda b, i: (0,) * a.ndim)
    ga2 = g_a.reshape(1, D)
    gk2 = g_k.reshape(1, D)
    wq = w_q.astype(BF16)
    wkv = w_kv.astype(BF16)
    q, k, v, km = pl.pallas_call(
        _moba_proj_kernel,
        out_shape=(jax.ShapeDtypeStruct((B, S, width), BF16),
                   jax.ShapeDtypeStruct((B, S, width), BF16),
                   jax.ShapeDtypeStruct((B, S, width), BF16),
                   jax.ShapeDtypeStruct((B, S // t, per, width), F32)),
        grid=(B, S // t),
        in_specs=[tok(D), vec(D), vec(D), full(ga2), vec(D), vec(D), full(gk2), full(wq), full(wkv),
                  tok(LANES), tok(LANES)],
        out_specs=(tok(width), tok(width), tok(width),
                   pl.BlockSpec((None, None, per, width), lambda b, i: (b, i, 0, 0))),
        name="moba_proj",
    )(x, sh_a, sc_a, ga2, sh_k, sc_k, gk2, wq, wkv, cos, sin)
    return q, k, v, km.reshape(B, S // MOBA_BLOCK, width)


def _final_kernel(x_ref, g_ref, sh_ref, sc_ref, o_ref):
    x = x_ref[...]
    y = x * _rms_scale(x) * g_ref[...]
    o_ref[...] = y * (1.0 + sc_ref[...]) + sh_ref[...]


def _final_norm(x, g, sh, sc):
    B, S, D = x.shape
    t = TOKEN_TILE
    tok = pl.BlockSpec((None, t, D), lambda b, i: (b, i, 0))
    vec = pl.BlockSpec((None, 1, D), lambda b, i: (b, 0, 0))
    return pl.pallas_call(
        _final_kernel,
        out_shape=jax.ShapeDtypeStruct((B, S, D), F32),
        grid=(B, S // t),
        in_specs=[tok, pl.BlockSpec((1, D), lambda b, i: (0, 0)), vec, vec],
        out_specs=tok,
        name="final_norm",
    )(x, g.reshape(1, D), sh, sc)


def _split_mod(mod, n):
    return [mod[:, None, j * D_MODEL:(j + 1) * D_MODEL] for j in range(n)]


def kernel(x, c, positions, ada_w, ada_b, norm_attn_g, norm_ffn_g, mla_w_in, mla_q_norm_g, mla_w_uq, mla_kv_norm_g, mla_w_ukv, mla_w_o, kv_ada_w, kv_ada_b, kv_norm_g, moba_w_kv, moba_w_q, moba_w_o, router_w, router_b, w_gate, b_gate, w_up, b_up, w_down, b_down, final_ada_w, final_ada_b, final_norm_g):
    mods = _ada_linear(c, ada_w, ada_b)
    kv_mod = _ada_linear(c, kv_ada_w[None], kv_ada_b[None])[0]
    f_mod = _ada_linear(c, final_ada_w[None], final_ada_b[None])[0]
    cos_a, sin_a, cos_b, sin_b = _rope_tables(positions)
    shared = None
    for layer in range(DEPTH):
        sh_a, sc_a, g_a, sh_f, sc_f, g_f = _split_mod(mods[layer], 6)
        if layer < N_A:
            q, k, v = _mla_proj(x, sh_a, sc_a, norm_attn_g[layer], mla_w_in[layer], mla_q_norm_g[layer],
                                mla_w_uq[layer], mla_kv_norm_g[layer], mla_w_ukv[layer], cos_a, sin_a)
            attn = _attention(q, k, v, None, heads=MLA_HEADS, dk=MLA_QK_PAD, dv=MLA_V,
                              bk=MLA_KV_TILE, moba=False)
            w_o = mla_w_o[layer]
        else:
            j = layer - N_A
            kv_sh, kv_sc = _split_mod(kv_mod, 2)
            q, k, v, km = _moba_proj(x, sh_a, sc_a, norm_attn_g[layer], kv_sh, kv_sc, kv_norm_g,
                                     moba_w_q[j], moba_w_kv, cos_b, sin_b)
            if shared is None:
                shared = (k, v, km)
            attn = _attention(q, shared[0], shared[1], shared[2], heads=MOBA_HEADS, dk=MOBA_HEAD_DIM,
                              dv=MOBA_HEAD_DIM, bk=MOBA_BLOCK, moba=True)
            w_o = moba_w_o[j]
        x, h, gates = _attn_out(attn, w_o, x, g_a, norm_ffn_g[layer], sh_f, sc_f,
                                router_w[layer], router_b[layer])
        x = _moe(h, gates, x, g_f, w_gate[layer], b_gate[layer], w_up[layer], b_up[layer],
                 w_down[layer], b_down[layer])
    f_sh, f_sc = _split_mod(f_mod, 2)
    return _final_norm(x, final_norm_g, f_sh, f_sc)
```

```python
import functools

import jax
import jax.numpy as jnp
from jax import lax
from jax.experimental import pallas as pl
from jax.experimental.pallas import tpu as pltpu

D_MODEL = 1024
DEPTH = 2
N_A = DEPTH // 2

MLA_HEADS = 8
MLA_Q_LORA = 256
MLA_KV_LORA = 128
MLA_NOPE = 128
MLA_ROPE = 64
MLA_V = 128

MOBA_HEADS = 8
MOBA_HEAD_DIM = D_MODEL // MOBA_HEADS
MOBA_BLOCK = 256
MOBA_TOPK = 3

N_EXPERTS = 32
TOP_K = 4
SWIGLU_LIMIT = 7.0
SWIGLU_ALPHA = 1.702

ROPE_THETA = 10000.0
NORM_EPS = 1e-6
NEG_INF = -1e30
LOG2_E = 1.4426950408889634
SOFTMAX_M_INIT = -1e20

LANES = 128
TOKEN_TILE = 512
ATTN_Q_TILE = 1024
ATTN_KV_TILE = 512
ATTN_WIDTH = 2 * LANES
EXPERT_TILE = 512
COMBINE_TILE = 256
VMEM_LIMIT_BYTES = 56 * 1024 * 1024

F32 = jnp.float32
BF16 = jnp.bfloat16


def _rms_scale(x):
    return lax.rsqrt(jnp.mean(x * x, axis=-1, keepdims=True) + NORM_EPS)


def _nt_dot(a, b):
    return lax.dot_general(a, b, (((1,), (1,)), ((), ())), preferred_element_type=F32)


def _ones_column(rows):
    lane = lax.broadcasted_iota(jnp.int32, (rows, LANES), 1)
    return jnp.where(lane == 0, 1.0, 0.0).astype(BF16)


def _ada_kernel(c_ref, w_ref, b_ref, o_ref):
    c = c_ref[...]
    ca = c * jax.nn.sigmoid(c)
    o_ref[...] = jnp.dot(ca, w_ref[...], precision=lax.Precision.HIGHEST,
                         preferred_element_type=F32) + b_ref[...]


def _ada_linear(c, w, b):
    L, D, M = w.shape
    B = c.shape[0]
    bn = 1024
    return pl.pallas_call(
        _ada_kernel,
        out_shape=jax.ShapeDtypeStruct((L, B, M), F32),
        grid=(L, M // bn),
        in_specs=[
            pl.BlockSpec((B, D), lambda l, j: (0, 0)),
            pl.BlockSpec((None, D, bn), lambda l, j: (l, 0, j)),
            pl.BlockSpec((None, 1, bn), lambda l, j: (l, 0, j)),
        ],
        out_specs=pl.BlockSpec((None, B, bn), lambda l, j: (l, 0, j)),
        name="ada_linear",
    )(c, w, b.reshape(L, 1, M))


def _rope_table_kernel(pos_ref, inv_a_ref, inv_b_ref, sign_ref, ca_ref, sa_ref, cb_ref, sb_ref):
    pos = pos_ref[...]
    ang_a = pos * inv_a_ref[...]
    ang_b = pos * inv_b_ref[...]
    ca_ref[...] = jnp.cos(ang_a)
    sa_ref[...] = jnp.sin(ang_a)
    cb_ref[...] = jnp.cos(ang_b)
    sb_ref[...] = jnp.sin(ang_b) * sign_ref[...]


def _rope_tables(positions):
    B, S = positions.shape
    t = 1024

    def inv_freq(d):
        half = d // 2
        return ROPE_THETA ** (-jnp.arange(half, dtype=F32) * (2.0 / d))

    inv_a = jnp.tile(inv_freq(MLA_ROPE), LANES // (MLA_ROPE // 2)).reshape(1, LANES)
    inv_b = jnp.tile(inv_freq(MOBA_HEAD_DIM), 2).reshape(1, LANES)
    half = MOBA_HEAD_DIM // 2
    sign = jnp.concatenate([-jnp.ones((half,), F32), jnp.ones((half,), F32)]).reshape(1, LANES)
    pos = positions.astype(F32).reshape(B, S, 1)
    row = pl.BlockSpec((1, LANES), lambda b, i: (0, 0))
    out = pl.BlockSpec((None, t, LANES), lambda b, i: (b, i, 0))
    shp = jax.ShapeDtypeStruct((B, S, LANES), F32)
    return pl.pallas_call(
        _rope_table_kernel,
        out_shape=(shp, shp, shp, shp),
        grid=(B, S // t),
        in_specs=[pl.BlockSpec((None, t, 1), lambda b, i: (b, i, 0)), row, row, row],
        out_specs=(out, out, out, out),
        name="rope_tables",
    )(pos, inv_a, inv_b, sign)


def _mla_proj_kernel(x_ref, sh_ref, sc_ref, g_ref, win_ref, qg_ref, wqn_ref, wqr_ref, wqrr_ref,
                     kvg_ref, wkn_ref, wv_ref, cos_ref, sin_ref, q_ref, k_ref, v_ref):
    x = x_ref[...]
    y = x * _rms_scale(x) * g_ref[...]
    h = (y * (1.0 + sc_ref[...]) + sh_ref[...]).astype(BF16)
    proj = jnp.dot(h, win_ref[...], preferred_element_type=F32)
    c_q = proj[:, :MLA_Q_LORA]
    c_kv = proj[:, MLA_Q_LORA:MLA_Q_LORA + MLA_KV_LORA]
    kr_a = proj[:, 384:512]
    kr_b = proj[:, 512:640]
    cq = (c_q * _rms_scale(c_q) * qg_ref[...]).astype(BF16)
    ckv = (c_kv * _rms_scale(c_kv) * kvg_ref[...]).astype(BF16)
    cos = cos_ref[...]
    sin = sin_ref[...]
    scale = LOG2_E * (MLA_NOPE + MLA_ROPE) ** -0.5
    kr = (kr_a * cos + kr_b * sin).astype(BF16)
    q_nope = jnp.dot(cq, wqn_ref[...], preferred_element_type=F32) * scale
    q_ra = jnp.dot(cq, wqr_ref[...], preferred_element_type=F32)
    q_rb = jnp.dot(cq, wqrr_ref[...], preferred_element_type=F32)
    k_nope = jnp.dot(ckv, wkn_ref[...], preferred_element_type=F32)
    v = jnp.dot(ckv, wv_ref[...], preferred_element_type=F32)
    ones = _ones_column(x.shape[0])
    for hh in range(MLA_HEADS):
        hs = slice(hh * LANES, (hh + 1) * LANES)
        lo = slice(hh * ATTN_WIDTH, hh * ATTN_WIDTH + LANES)
        hi = slice(hh * ATTN_WIDTH + LANES, (hh + 1) * ATTN_WIDTH)
        q_ref[:, lo] = q_nope[:, hs].astype(BF16)
        q_ref[:, hi] = ((q_ra[:, hs] * cos + q_rb[:, hs] * sin) * scale).astype(BF16)
        k_ref[:, lo] = k_nope[:, hs].astype(BF16)
        k_ref[:, hi] = kr
        v_ref[:, lo] = v[:, hs].astype(BF16)
        v_ref[:, hi] = ones


def _rot_half_cols(w, half):
    return jnp.concatenate([-w[..., half:], w[..., :half]], axis=-1)


def _mla_proj(x, sh, sc, g, w_in, q_norm_g, w_uq, kv_norm_g, w_ukv, cos, sin):
    B, S, D = x.shape
    H = MLA_HEADS
    t = TOKEN_TILE
    half = MLA_ROPE // 2
    w_kr = w_in[:, MLA_Q_LORA + MLA_KV_LORA:]
    zpad = jnp.zeros((D, LANES - MLA_ROPE), F32)
    w_in_ext = jnp.concatenate(
        [w_in[:, :MLA_Q_LORA + MLA_KV_LORA], w_kr, zpad, _rot_half_cols(w_kr, half), zpad],
        axis=1).astype(BF16)
    wq = w_uq.reshape(MLA_Q_LORA, H, MLA_NOPE + MLA_ROPE)
    wq_nope = wq[..., :MLA_NOPE].reshape(MLA_Q_LORA, H * MLA_NOPE).astype(BF16)
    wq_r = wq[..., MLA_NOPE:]
    pad = ((0, 0), (0, 0), (0, LANES - MLA_ROPE))
    wq_rope = jnp.pad(wq_r, pad).reshape(MLA_Q_LORA, H * LANES).astype(BF16)
    wq_rope_rot = jnp.pad(_rot_half_cols(wq_r, half), pad).reshape(MLA_Q_LORA, H * LANES).astype(BF16)
    wkv = w_ukv.reshape(MLA_KV_LORA, H, MLA_NOPE + MLA_V)
    wk_nope = wkv[..., :MLA_NOPE].reshape(MLA_KV_LORA, H * MLA_NOPE).astype(BF16)
    wv = wkv[..., MLA_NOPE:].reshape(MLA_KV_LORA, H * MLA_V).astype(BF16)

    tok = lambda w: pl.BlockSpec((None, t, w), lambda b, i: (b, i, 0))
    vec = lambda w: pl.BlockSpec((None, 1, w), lambda b, i: (b, 0, 0))
    full = lambda a: pl.BlockSpec(a.shape, lambda b, i: (0,) * a.ndim)
    g2 = g.reshape(1, D)
    qg2 = q_norm_g.reshape(1, MLA_Q_LORA)
    kvg2 = kv_norm_g.reshape(1, MLA_KV_LORA)
    wide = jax.ShapeDtypeStruct((B, S, H * ATTN_WIDTH), BF16)
    return pl.pallas_call(
        _mla_proj_kernel,
        out_shape=(wide, wide, wide),
        grid=(B, S // t),
        in_specs=[tok(D), vec(D), vec(D), full(g2), full(w_in_ext), full(qg2), full(wq_nope),
                  full(wq_rope), full(wq_rope_rot), full(kvg2), full(wk_nope), full(wv),
                  tok(LANES), tok(LANES)],
        out_specs=(tok(H * ATTN_WIDTH), tok(H * ATTN_WIDTH), tok(H * ATTN_WIDTH)),
        name="mla_proj",
    )(x, sh, sc, g2, w_in_ext, qg2, wq_nope, wq_rope, wq_rope_rot, kvg2, wk_nope, wv, cos, sin)


def _attn_kernel(*refs, moba, tq, bk, dv):
    if moba:
        (q_ref, k_ref, v_ref, km_ref, o_ref,
         qx_ref, s_a, s_b, p_a, p_b, al_a, al_b, m_sc, acc_sc) = refs
    else:
        q_ref, k_ref, v_ref, o_ref, s_a, s_b, p_a, p_b, al_a, al_b, m_sc, acc_sc = refs
        qx_ref = q_ref
    i = pl.program_id(2)

    if moba:
        q = q_ref[...]
        nb = km_ref.shape[0]
        gate = _nt_dot(km_ref[...].astype(BF16), q)
        blk = lax.broadcasted_iota(jnp.int32, (nb, tq), 0)
        blk_f = blk.astype(F32)
        own = (i * tq + lax.broadcasted_iota(jnp.int32, (1, tq), 1)) // MOBA_BLOCK
        gate = jnp.where(blk < own, gate, -jnp.inf)
        sel = jnp.where(blk == own, 1.0, 0.0)
        for _ in range(MOBA_TOPK):
            mx = jnp.max(gate, axis=0, keepdims=True)
            first = jnp.min(jnp.where(gate == mx, blk_f, float(nb)), axis=0, keepdims=True)
            hit = blk_f == first
            sel = jnp.where(hit & (mx > -jnp.inf), 1.0, sel)
            gate = jnp.where(hit, -jnp.inf, gate)
        bias = jnp.where(sel > 0.0, 0.0, NEG_INF)
        bias = jnp.concatenate([bias, jnp.zeros((LANES - nb, tq), F32)], axis=0)
        qx_ref[:, :LANES] = q
        qx_ref[:, LANES:] = bias.T.astype(BF16)

    m_sc[...] = jnp.full(m_sc.shape, SOFTMAX_M_INIT, F32)
    acc_sc[...] = jnp.zeros(acc_sc.shape, F32)
    p_b[...] = jnp.zeros(p_b.shape, BF16)
    al_b[...] = jnp.ones(al_b.shape, F32)
    row_id = lax.broadcasted_iota(jnp.int32, (tq, 1), 0)
    col_id = lax.broadcasted_iota(jnp.int32, (1, bk), 1)

    def scores(j, s_dst):
        start = pl.multiple_of(j * bk, bk)
        s_dst[...] = _nt_dot(qx_ref[...], k_ref[pl.ds(start, bk), :])

    def soft(s_src, p_dst, al_dst, diag_offset=None):
        s = s_src[...]
        if diag_offset is not None:
            s = jnp.where(col_id + diag_offset <= row_id, s, -jnp.inf)
        m_prev = m_sc[...]
        m_new = jnp.maximum(m_prev, jnp.max(s, axis=1, keepdims=True))
        p_dst[...] = jnp.exp2(s - m_new).astype(BF16)
        al_dst[...] = jnp.exp2(m_prev - m_new)
        m_sc[...] = m_new

    def fold(j, p_src, al_src):
        start = pl.multiple_of(j * bk, bk)
        acc_sc[...] = al_src[...] * acc_sc[...] + jnp.dot(
            p_src[...], v_ref[pl.ds(start, bk), :], preferred_element_type=F32)

    scores(0, s_a)

    def body(t, carry):
        j = 2 * t
        scores(j + 1, s_b)
        soft(s_a, p_a, al_a)
        fold(jnp.maximum(j - 1, 0), p_b, al_b)
        scores(j + 2, s_a)
        soft(s_b, p_b, al_b)
        fold(j, p_a, al_a)
        return carry

    lax.fori_loop(0, i, body, 0)
    j = 2 * i
    scores(j + 1, s_b)
    soft(s_a, p_a, al_a, diag_offset=0)
    fold(jnp.maximum(j - 1, 0), p_b, al_b)
    soft(s_b, p_b, al_b, diag_offset=bk)
    fold(j, p_a, al_a)
    fold(j + 1, p_b, al_b)
    acc = acc_sc[...]
    o_ref[...] = (acc[:, :dv] / acc[:, dv:dv + 1]).astype(o_ref.dtype)


def _attention(q, k, v, k_mean, *, heads, dq, dv, moba):
    B, S, _ = q.shape
    tq, bk, w = ATTN_Q_TILE, ATTN_KV_TILE, ATTN_WIDTH
    assert tq == 2 * bk and tq % MOBA_BLOCK == 0 and S % tq == 0
    in_specs = [
        pl.BlockSpec((None, tq, dq), lambda b, h, i: (b, i, h)),
        pl.BlockSpec((None, S, w), lambda b, h, i: (b, 0, h)),
        pl.BlockSpec((None, S, w), lambda b, h, i: (b, 0, h)),
    ]
    args = [q, k, v]
    scratch = []
    if moba:
        nb = k_mean.shape[1]
        in_specs.append(pl.BlockSpec((None, nb, dq), lambda b, h, i: (b, 0, h)))
        args.append(k_mean)
        scratch.append(pltpu.VMEM((tq, w), BF16))
    scratch += [pltpu.VMEM((tq, bk), F32), pltpu.VMEM((tq, bk), F32),
                pltpu.VMEM((tq, bk), BF16), pltpu.VMEM((tq, bk), BF16),
                pltpu.VMEM((tq, 1), F32), pltpu.VMEM((tq, 1), F32),
                pltpu.VMEM((tq, 1), F32), pltpu.VMEM((tq, w), F32)]
    return pl.pallas_call(
        functools.partial(_attn_kernel, moba=moba, tq=tq, bk=bk, dv=dv),
        out_shape=jax.ShapeDtypeStruct((B, S, heads * dv), BF16),
        grid=(B, heads, S // tq),
        in_specs=in_specs,
        out_specs=pl.BlockSpec((None, tq, dv), lambda b, h, i: (b, i, h)),
        scratch_shapes=scratch,
        name="moba_attention" if moba else "mla_attention",
    )(*args)


def _attn_out_kernel(a_ref, wo_ref, x_ref, ga_ref, g_ref, sh_ref, sc_ref, rw_ref, rb_ref,
                     x1_ref, h_ref, pe_ref, pr_ref, pw_ref, cnt_ref, run_ref):
    first_step = (pl.program_id(0) == 0) & (pl.program_id(1) == 0)

    @pl.when(first_step)
    def _():
        run_ref[...] = jnp.zeros(run_ref.shape, F32)

    a = jnp.dot(a_ref[...], wo_ref[...], preferred_element_type=F32)
    x1 = x_ref[...] + ga_ref[...] * a
    x1_ref[...] = x1
    y = x1 * _rms_scale(x1) * g_ref[...]
    h = y * (1.0 + sc_ref[...]) + sh_ref[...]
    h_ref[...] = h
    logits = jnp.dot(h.astype(BF16), rw_ref[...], preferred_element_type=F32) + rb_ref[...]
    t = logits.shape[0]
    lane_i = lax.broadcasted_iota(jnp.int32, logits.shape, 1)
    lane = lane_i.astype(F32)
    logits = jnp.where(lane_i < N_EXPERTS, logits, -jnp.inf)
    top = None
    den = jnp.zeros((t, 1), F32)
    sel = jnp.zeros(logits.shape, F32)
    hits, firsts, ws = [], [], []
    for _ in range(TOP_K):
        mx = jnp.max(logits, axis=1, keepdims=True)
        first = jnp.min(jnp.where(logits == mx, lane, float(LANES)), axis=1, keepdims=True)
        hit = lane == first
        if top is None:
            top = mx
        w = jnp.exp(mx - top)
        den = den + w
        sel = jnp.where(hit, 1.0, sel)
        logits = jnp.where(hit, -jnp.inf, logits)
        hits.append(hit)
        firsts.append(first)
        ws.append(w)
    r_id = lax.broadcasted_iota(jnp.int32, (t, t), 0)
    c_id = lax.broadcasted_iota(jnp.int32, (t, t), 1)
    before = jnp.where(c_id < r_id, 1.0, 0.0).astype(BF16)
    pos = jnp.dot(before, sel.astype(BF16), preferred_element_type=F32) + run_ref[...]
    run_ref[...] = run_ref[...] + jnp.sum(sel, axis=0, keepdims=True)
    cnt_ref[...] = run_ref[...]
    pe = jnp.zeros(logits.shape, F32)
    pr = jnp.zeros(logits.shape, F32)
    pw = jnp.zeros(logits.shape, F32)
    for kk in range(TOP_K):
        rank = jnp.sum(jnp.where(hits[kk], pos, 0.0), axis=1, keepdims=True)
        pe = jnp.where(lane_i == kk, firsts[kk], pe)
        pr = jnp.where(lane_i == kk, rank, pr)
        pw = jnp.where(lane_i == kk, ws[kk] / den, pw)
    pe_ref[...] = pe.astype(jnp.int32)
    pr_ref[...] = pr.astype(jnp.int32)
    pw_ref[...] = pw


def _attn_out(attn, w_o, x, g_a, norm_g, sh_f, sc_f, router_w, router_b):
    B, S, D = x.shape
    t = TOKEN_TILE
    rw = jnp.pad(router_w, ((0, 0), (0, LANES - N_EXPERTS))).astype(BF16)
    rb = jnp.pad(router_b, (0, LANES - N_EXPERTS)).reshape(1, LANES)
    wo = w_o.astype(BF16)
    g2 = norm_g.reshape(1, D)
    tok = lambda w: pl.BlockSpec((None, t, w), lambda b, i: (b, i, 0))
    vec = lambda w: pl.BlockSpec((None, 1, w), lambda b, i: (b, 0, 0))
    full = lambda a: pl.BlockSpec(a.shape, lambda b, i: (0,) * a.ndim)
    lanes_i = jax.ShapeDtypeStruct((B, S, LANES), jnp.int32)
    return pl.pallas_call(
        _attn_out_kernel,
        out_shape=(jax.ShapeDtypeStruct((B, S, D), F32),
                   jax.ShapeDtypeStruct((B, S, D), F32),
                   lanes_i, lanes_i,
                   jax.ShapeDtypeStruct((B, S, LANES), F32),
                   jax.ShapeDtypeStruct((1, LANES), F32)),
        grid=(B, S // t),
        in_specs=[tok(attn.shape[-1]), full(wo), tok(D), vec(D), full(g2), vec(D), vec(D),
                  full(rw), full(rb)],
        out_specs=(tok(D), tok(D), tok(LANES), tok(LANES), tok(LANES),
                   pl.BlockSpec((1, LANES), lambda b, i: (0, 0))),
        scratch_shapes=[pltpu.VMEM((1, LANES), F32)],
        compiler_params=pltpu.CompilerParams(dimension_semantics=("arbitrary", "arbitrary")),
        name="attn_out_router",
    )(attn, wo, x, g_a, g2, sh_f, sc_f, rw, rb)


def _row_copy(src_ref, src_row, dst_ref, dst_row, sem):
    return pltpu.make_async_copy(src_ref.at[pl.ds(src_row, 1), :], dst_ref.at[pl.ds(dst_row, 1), :], sem)


def _dispatch_kernel(pad_start_ref, pad_cnt_ref, slot_ref, h_ref, xs_ref, zero_ref, sem):
    t = h_ref.shape[0]

    @pl.when(pl.program_id(0) == 0)
    def _():
        zero_ref[...] = jnp.zeros(zero_ref.shape, F32)

        def per_expert(e, carry):
            start = pad_start_ref[e]
            cnt = pad_cnt_ref[e]

            def issue(r, c):
                _row_copy(zero_ref, 0, xs_ref, start + r, sem).start()
                return c

            def drain(r, c):
                _row_copy(zero_ref, 0, xs_ref, start + r, sem).wait()
                return c

            lax.fori_loop(0, cnt, issue, 0)
            lax.fori_loop(0, cnt, drain, 0)
            return carry

        lax.fori_loop(0, N_EXPERTS, per_expert, 0)

    def issue(r, c):
        for kk in range(TOP_K):
            _row_copy(h_ref, r, xs_ref, slot_ref[r * TOP_K + kk], sem).start()
        return c

    def drain(r, c):
        for kk in range(TOP_K):
            _row_copy(h_ref, r, xs_ref, slot_ref[r * TOP_K + kk], sem).wait()
        return c

    lax.fori_loop(0, t, issue, 0)
    lax.fori_loop(0, t, drain, 0)


def _dispatch(h, slot, pad_start, pad_cnt, n_rows):
    N, D = h.shape
    t = TOKEN_TILE
    return pl.pallas_call(
        _dispatch_kernel,
        out_shape=jax.ShapeDtypeStruct((n_rows, D), F32),
        grid_spec=pltpu.PrefetchScalarGridSpec(
            num_scalar_prefetch=2,
            grid=(N // t,),
            in_specs=[pl.BlockSpec((t * TOP_K,), lambda i, ps, pc: (i,), memory_space=pltpu.SMEM),
                      pl.BlockSpec((t, D), lambda i, ps, pc: (i, 0))],
            out_specs=pl.BlockSpec(memory_space=pl.ANY),
            scratch_shapes=[pltpu.VMEM((8, D), F32), pltpu.SemaphoreType.DMA],
        ),
        compiler_params=pltpu.CompilerParams(dimension_semantics=("arbitrary",)),
        name="moe_dispatch",
    )(pad_start, pad_cnt, slot, h)


def _experts_kernel(tile_e_ref, tile_blk_ref, nact_ref, xs_ref, wg_ref, bg_ref, wu_ref, bu_ref,
                    wd_ref, bd_ref, y_ref, wg_bf, wu_bf, wd_bf):
    j = pl.program_id(0)

    @pl.when(j < nact_ref[0])
    def _():
        prev = tile_e_ref[jnp.maximum(j - 1, 0)]

        @pl.when((j == 0) | (tile_e_ref[j] != prev))
        def _():
            wg_bf[...] = wg_ref[...].astype(BF16)
            wu_bf[...] = wu_ref[...].astype(BF16)
            wd_bf[...] = wd_ref[...].astype(BF16)

        x = xs_ref[...].astype(BF16)
        g = jnp.minimum(jnp.dot(x, wg_bf[...], preferred_element_type=F32) + bg_ref[...], SWIGLU_LIMIT)
        u = jnp.clip(jnp.dot(x, wu_bf[...], preferred_element_type=F32) + bu_ref[...],
                     -SWIGLU_LIMIT, SWIGLU_LIMIT)
        a = g * jax.nn.sigmoid(SWIGLU_ALPHA * g) * (u + 1.0)
        y_ref[...] = jnp.dot(a.astype(BF16), wd_bf[...], preferred_element_type=F32) + bd_ref[...]


def _experts(xs, tile_e, tile_blk, nact, w_gate, b_gate, w_up, b_up, w_down, b_down):
    P, D = xs.shape
    E, _, F = w_gate.shape
    tm = EXPERT_TILE
    rows = pl.BlockSpec((tm, D), lambda j, te, tb, na: (tb[j], 0))
    wspec = lambda r, c: pl.BlockSpec((None, r, c), lambda j, te, tb, na: (te[j], 0, 0))
    return pl.pallas_call(
        _experts_kernel,
        out_shape=jax.ShapeDtypeStruct((P, D), F32),
        grid_spec=pltpu.PrefetchScalarGridSpec(
            num_scalar_prefetch=3,
            grid=(P // tm,),
            in_specs=[rows, wspec(D, F), wspec(1, F), wspec(D, F), wspec(1, F), wspec(F, D), wspec(1, D)],
            out_specs=rows,
            scratch_shapes=[pltpu.VMEM((D, F), BF16), pltpu.VMEM((D, F), BF16), pltpu.VMEM((F, D), BF16)],
        ),
        compiler_params=pltpu.CompilerParams(dimension_semantics=("arbitrary",),
                                             vmem_limit_bytes=VMEM_LIMIT_BYTES),
        name="moe_experts",
    )(tile_e, tile_blk, nact, xs, w_gate, b_gate.reshape(E, 1, F), w_up, b_up.reshape(E, 1, F),
      w_down, b_down.reshape(E, 1, D))


def _combine_kernel(slot_ref, y_ref, pw_ref, x_ref, gf_ref, o_ref, buf_ref, sem):
    t = x_ref.shape[0]

    def issue(r, c):
        for kk in range(TOP_K):
            _row_copy(y_ref, slot_ref[r * TOP_K + kk], buf_ref.at[kk], r, sem).start()
        return c

    def drain(r, c):
        for kk in range(TOP_K):
            _row_copy(y_ref, slot_ref[r * TOP_K + kk], buf_ref.at[kk], r, sem).wait()
        return c

    lax.fori_loop(0, t, issue, 0)
    lax.fori_loop(0, t, drain, 0)
    pw = pw_ref[...]
    acc = pw[:, 0:1] * buf_ref[0]
    for kk in range(1, TOP_K):
        acc = acc + pw[:, kk:kk + 1] * buf_ref[kk]
    o_ref[...] = x_ref[...] + gf_ref[...] * acc


def _combine(y, slot, pw, x, g_f):
    B, S, D = x.shape
    t = COMBINE_TILE
    per = S // t
    return pl.pallas_call(
        _combine_kernel,
        out_shape=jax.ShapeDtypeStruct((B, S, D), F32),
        grid=(B * per,),
        in_specs=[pl.BlockSpec((t * TOP_K,), lambda i: (i,), memory_space=pltpu.SMEM),
                  pl.BlockSpec(memory_space=pl.ANY),
                  pl.BlockSpec((None, t, LANES), lambda i: (i // per, i % per, 0)),
                  pl.BlockSpec((None, t, D), lambda i: (i // per, i % per, 0)),
                  pl.BlockSpec((None, 1, D), lambda i: (i // per, 0, 0))],
        out_specs=pl.BlockSpec((None, t, D), lambda i: (i // per, i % per, 0)),
        scratch_shapes=[pltpu.VMEM((TOP_K, t, D), F32), pltpu.SemaphoreType.DMA],
        compiler_params=pltpu.CompilerParams(dimension_semantics=("arbitrary",)),
        name="moe_combine",
    )(slot, y, pw, x, g_f)


def _moe(h, pe, pr, pw, counts, x, g_f, w_gate, b_gate, w_up, b_up, w_down, b_down):
    B, S, D = x.shape
    N = B * S
    E = N_EXPERTS
    tm = EXPERT_TILE
    n_tiles = N * TOP_K // tm + E
    n_rows = n_tiles * tm
    cnt = counts[0, :E].astype(jnp.int32)
    tiles_per = (cnt + tm - 1) // tm
    tile_end = jnp.cumsum(tiles_per)
    row_start = (tile_end - tiles_per) * tm
    nact = tile_end[-1:]
    jj = jnp.minimum(jnp.arange(n_tiles, dtype=jnp.int32), nact[0] - 1)
    tile_e = jnp.sum((jj[:, None] >= tile_end[None, :]).astype(jnp.int32), axis=1)
    pe4 = pe.reshape(N, LANES)[:, :TOP_K]
    pr4 = pr.reshape(N, LANES)[:, :TOP_K]
    onehot = pe4[:, :, None] == jnp.arange(E, dtype=jnp.int32)[None, None, :]
    slot = (jnp.sum(jnp.where(onehot, row_start[None, None, :], 0), axis=-1) + pr4).reshape(N * TOP_K)
    xs = _dispatch(h.reshape(N, D), slot, row_start + cnt, tiles_per * tm - cnt, n_rows)
    y = _experts(xs, tile_e, jj, nact, w_gate, b_gate, w_up, b_up, w_down, b_down)
    return _combine(y, slot, pw, x, g_f)


def _moba_proj_kernel(x_ref, sha_ref, sca_ref, ga_ref, shk_ref, sck_ref, gk_ref, wq_ref, wkv_ref,
                      cos_ref, sin_ref, q_ref, k_ref, v_ref, km_ref):
    x = x_ref[...]
    xn = x * _rms_scale(x)
    h = ((xn * ga_ref[...]) * (1.0 + sca_ref[...]) + sha_ref[...]).astype(BF16)
    hkv = ((xn * gk_ref[...]) * (1.0 + sck_ref[...]) + shk_ref[...]).astype(BF16)
    q = jnp.dot(h, wq_ref[...], preferred_element_type=F32)
    kv = jnp.dot(hkv, wkv_ref[...], preferred_element_type=F32)
    cos = cos_ref[...]
    sin = sin_ref[...]
    scale = LOG2_E * MOBA_HEAD_DIM ** -0.5
    hd = MOBA_HEAD_DIM
    t = x.shape[0]
    width = MOBA_HEADS * hd
    ones = _ones_column(t)
    seq_row = pl.program_id(1) * t + lax.broadcasted_iota(jnp.int32, (t, LANES), 0)
    lane = lax.broadcasted_iota(jnp.int32, (t, LANES), 1)
    block_id = jnp.where(seq_row // MOBA_BLOCK == lane, 1.0, 0.0).astype(BF16)
    for hh in range(MOBA_HEADS):
        hs = slice(hh * hd, (hh + 1) * hd)
        lo = slice(hh * ATTN_WIDTH, hh * ATTN_WIDTH + LANES)
        hi = slice(hh * ATTN_WIDTH + LANES, (hh + 1) * ATTN_WIDTH)
        qh = q[:, hs]
        kh = kv[:, hs]
        q_ref[:, hs] = ((qh * cos + pltpu.roll(qh, hd // 2, 1) * sin) * scale).astype(BF16)
        kr = kh * cos + pltpu.roll(kh, hd // 2, 1) * sin
        k_ref[:, lo] = kr.astype(BF16)
        k_ref[:, hi] = block_id
        km_ref[:, hs] = jnp.mean(kr.reshape(t // MOBA_BLOCK, MOBA_BLOCK, hd), axis=1)
        v_ref[:, lo] = kv[:, width + hh * hd:width + (hh + 1) * hd].astype(BF16)
        v_ref[:, hi] = ones


def _moba_proj(x, sh_a, sc_a, g_a, sh_k, sc_k, g_k, w_q, w_kv, cos, sin):
    B, S, D = x.shape
    t = TOKEN_TILE
    H = MOBA_HEADS
    width = H * MOBA_HEAD_DIM
    per = t // MOBA_BLOCK
    assert S // MOBA_BLOCK <= LANES
    tok = lambda w: pl.BlockSpec((None, t, w), lambda b, i: (b, i, 0))
    vec = lambda w: pl.BlockSpec((None, 1, w), lambda b, i: (b, 0, 0))
    full = lambda a: pl.BlockSpec(a.shape, lambda b, i: (0,) * a.ndim)
    ga2 = g_a.reshape(1, D)
    gk2 = g_k.reshape(1, D)
    wq = w_q.astype(BF16)
    wkv = w_kv.astype(BF16)
    wide = jax.ShapeDtypeStruct((B, S, H * ATTN_WIDTH), BF16)
    q, k, v, km = pl.pallas_call(
        _moba_proj_kernel,
        out_shape=(jax.ShapeDtypeStruct((B, S, width), BF16), wide, wide,
                   jax.ShapeDtypeStruct((B, S // t, per, width), F32)),
        grid=(B, S // t),
        in_specs=[tok(D), vec(D), vec(D), full(ga2), vec(D), vec(D), full(gk2), full(wq), full(wkv),
                  tok(LANES), tok(LANES)],
        out_specs=(tok(width), tok(H * ATTN_WIDTH), tok(H * ATTN_WIDTH),
                   pl.BlockSpec((None, None, per, width), lambda b, i: (b, i, 0, 0))),
        name="moba_proj",
    )(x, sh_a, sc_a, ga2, sh_k, sc_k, gk2, wq, wkv, cos, sin)
    return q, k, v, km.reshape(B, S // MOBA_BLOCK, width)


def _final_kernel(x_ref, g_ref, sh_ref, sc_ref, o_ref):
    x = x_ref[...]
    y = x * _rms_scale(x) * g_ref[...]
    o_ref[...] = y * (1.0 + sc_ref[...]) + sh_ref[...]


def _final_norm(x, g, sh, sc):
    B, S, D = x.shape
    t = TOKEN_TILE
    tok = pl.BlockSpec((None, t, D), lambda b, i: (b, i, 0))
    vec = pl.BlockSpec((None, 1, D), lambda b, i: (b, 0, 0))
    return pl.pallas_call(
        _final_kernel,
        out_shape=jax.ShapeDtypeStruct((B, S, D), F32),
        grid=(B, S // t),
        in_specs=[tok, pl.BlockSpec((1, D), lambda b, i: (0, 0)), vec, vec],
        out_specs=tok,
        name="final_norm",
    )(x, g.reshape(1, D), sh, sc)


def _split_mod(mod, n):
    return [mod[:, None, j * D_MODEL:(j + 1) * D_MODEL] for j in range(n)]


def kernel(x, c, positions, ada_w, ada_b, norm_attn_g, norm_ffn_g, mla_w_in, mla_q_norm_g, mla_w_uq, mla_kv_norm_g, mla_w_ukv, mla_w_o, kv_ada_w, kv_ada_b, kv_norm_g, moba_w_kv, moba_w_q, moba_w_o, router_w, router_b, w_gate, b_gate, w_up, b_up, w_down, b_down, final_ada_w, final_ada_b, final_norm_g):
    mods = _ada_linear(c, ada_w, ada_b)
    kv_mod = _ada_linear(c, kv_ada_w[None], kv_ada_b[None])[0]
    f_mod = _ada_linear(c, final_ada_w[None], final_ada_b[None])[0]
    cos_a, sin_a, cos_b, sin_b = _rope_tables(positions)
    shared = None
    for layer in range(DEPTH):
        sh_a, sc_a, g_a, sh_f, sc_f, g_f = _split_mod(mods[layer], 6)
        if layer < N_A:
            q, k, v = _mla_proj(x, sh_a, sc_a, norm_attn_g[layer], mla_w_in[layer], mla_q_norm_g[layer],
                                mla_w_uq[layer], mla_kv_norm_g[layer], mla_w_ukv[layer], cos_a, sin_a)
            attn = _attention(q, k, v, None, heads=MLA_HEADS, dq=ATTN_WIDTH, dv=MLA_V, moba=False)
            w_o = mla_w_o[layer]
        else:
            j = layer - N_A
            kv_sh, kv_sc = _split_mod(kv_mod, 2)
            q, k, v, km = _moba_proj(x, sh_a, sc_a, norm_attn_g[layer], kv_sh, kv_sc, kv_norm_g,
                                     moba_w_q[j], moba_w_kv, cos_b, sin_b)
            if shared is None:
                shared = (k, v, km)
            attn = _attention(q, shared[0], shared[1], shared[2], heads=MOBA_HEADS, dq=MOBA_HEAD_DIM,
                              dv=MOBA_HEAD_DIM, moba=True)
            w_o = moba_w_o[j]
        x, h, pe, pr, pw, counts = _attn_out(attn, w_o, x, g_a, norm_ffn_g[layer], sh_f, sc_f,
                                             router_w[layer], router_b[layer])
        x = _moe(h, pe, pr, pw, counts, x, g_f, w_gate[layer], b_gate[layer], w_up[layer],
                 b_up[layer], w_down[layer], b_down[layer])
    f_sh, f_sc = _split_mod(f_mod, 2)
    return _final_norm(x, final_norm_g, f_sh, f_sc)
```

```python
import functools

import jax
import jax.numpy as jnp
from jax import lax
from jax.experimental import pallas as pl
from jax.experimental.pallas import tpu as pltpu

D_MODEL = 1024
DEPTH = 2
N_A = DEPTH // 2

MLA_HEADS = 8
MLA_Q_LORA = 256
MLA_KV_LORA = 128
MLA_NOPE = 128
MLA_ROPE = 64
MLA_V = 128

MOBA_HEADS = 8
MOBA_HEAD_DIM = D_MODEL // MOBA_HEADS
MOBA_BLOCK = 256
MOBA_TOPK = 3

N_EXPERTS = 32
TOP_K = 4
SWIGLU_LIMIT = 7.0
SWIGLU_ALPHA = 1.702

ROPE_THETA = 10000.0
NORM_EPS = 1e-6
NEG_INF = -1e30
LOG2_E = 1.4426950408889634
SOFTMAX_M_INIT = -1e20

LANES = 128
TOKEN_TILE = 512
ATTN_Q_TILE = 1024
ATTN_KV_TILE = 512
ATTN_WIDTH = 2 * LANES
EXPERT_TILE = 512
SEG_ALIGN = 8
SEG_BUF_ROWS = TOKEN_TILE * TOP_K + N_EXPERTS * SEG_ALIGN
VMEM_LIMIT_BYTES = 56 * 1024 * 1024

F32 = jnp.float32
BF16 = jnp.bfloat16


def _rms_scale(x):
    return lax.rsqrt(jnp.mean(x * x, axis=-1, keepdims=True) + NORM_EPS)


def _nt_dot(a, b):
    return lax.dot_general(a, b, (((1,), (1,)), ((), ())), preferred_element_type=F32)


def _ones_column(rows):
    lane = lax.broadcasted_iota(jnp.int32, (rows, LANES), 1)
    return jnp.where(lane == 0, 1.0, 0.0).astype(BF16)


def _ada_kernel(c_ref, w_ref, b_ref, o_ref):
    c = c_ref[...]
    ca = c * jax.nn.sigmoid(c)
    o_ref[...] = jnp.dot(ca, w_ref[...], precision=lax.Precision.HIGHEST,
                         preferred_element_type=F32) + b_ref[...]


def _ada_linear(c, w, b):
    L, D, M = w.shape
    B = c.shape[0]
    bn = 1024
    return pl.pallas_call(
        _ada_kernel,
        out_shape=jax.ShapeDtypeStruct((L, B, M), F32),
        grid=(L, M // bn),
        in_specs=[
            pl.BlockSpec((B, D), lambda l, j: (0, 0)),
            pl.BlockSpec((None, D, bn), lambda l, j: (l, 0, j)),
            pl.BlockSpec((None, 1, bn), lambda l, j: (l, 0, j)),
        ],
        out_specs=pl.BlockSpec((None, B, bn), lambda l, j: (l, 0, j)),
        name="ada_linear",
    )(c, w, b.reshape(L, 1, M))


def _rope_table_kernel(pos_ref, inv_a_ref, inv_b_ref, sign_ref, ca_ref, sa_ref, cb_ref, sb_ref):
    pos = pos_ref[...]
    ang_a = pos * inv_a_ref[...]
    ang_b = pos * inv_b_ref[...]
    ca_ref[...] = jnp.cos(ang_a)
    sa_ref[...] = jnp.sin(ang_a)
    cb_ref[...] = jnp.cos(ang_b)
    sb_ref[...] = jnp.sin(ang_b) * sign_ref[...]


def _rope_tables(positions):
    B, S = positions.shape
    t = 1024

    def inv_freq(d):
        half = d // 2
        return ROPE_THETA ** (-jnp.arange(half, dtype=F32) * (2.0 / d))

    inv_a = jnp.tile(inv_freq(MLA_ROPE), LANES // (MLA_ROPE // 2)).reshape(1, LANES)
    inv_b = jnp.tile(inv_freq(MOBA_HEAD_DIM), 2).reshape(1, LANES)
    half = MOBA_HEAD_DIM // 2
    sign = jnp.concatenate([-jnp.ones((half,), F32), jnp.ones((half,), F32)]).reshape(1, LANES)
    pos = positions.astype(F32).reshape(B, S, 1)
    row = pl.BlockSpec((1, LANES), lambda b, i: (0, 0))
    out = pl.BlockSpec((None, t, LANES), lambda b, i: (b, i, 0))
    shp = jax.ShapeDtypeStruct((B, S, LANES), F32)
    return pl.pallas_call(
        _rope_table_kernel,
        out_shape=(shp, shp, shp, shp),
        grid=(B, S // t),
        in_specs=[pl.BlockSpec((None, t, 1), lambda b, i: (b, i, 0)), row, row, row],
        out_specs=(out, out, out, out),
        name="rope_tables",
    )(pos, inv_a, inv_b, sign)


def _mla_proj_kernel(x_ref, sh_ref, sc_ref, g_ref, win_ref, qg_ref, wqn_ref, wqr_ref, wqrr_ref,
                     kvg_ref, wkn_ref, wv_ref, cos_ref, sin_ref, q_ref, k_ref, v_ref):
    x = x_ref[...]
    y = x * _rms_scale(x) * g_ref[...]
    h = (y * (1.0 + sc_ref[...]) + sh_ref[...]).astype(BF16)
    proj = jnp.dot(h, win_ref[...], preferred_element_type=F32)
    c_q = proj[:, :MLA_Q_LORA]
    c_kv = proj[:, MLA_Q_LORA:MLA_Q_LORA + MLA_KV_LORA]
    kr_a = proj[:, 384:512]
    kr_b = proj[:, 512:640]
    cq = (c_q * _rms_scale(c_q) * qg_ref[...]).astype(BF16)
    ckv = (c_kv * _rms_scale(c_kv) * kvg_ref[...]).astype(BF16)
    cos = cos_ref[...]
    sin = sin_ref[...]
    scale = LOG2_E * (MLA_NOPE + MLA_ROPE) ** -0.5
    kr = (kr_a * cos + kr_b * sin).astype(BF16)
    q_nope = jnp.dot(cq, wqn_ref[...], preferred_element_type=F32) * scale
    q_ra = jnp.dot(cq, wqr_ref[...], preferred_element_type=F32)
    q_rb = jnp.dot(cq, wqrr_ref[...], preferred_element_type=F32)
    k_nope = jnp.dot(ckv, wkn_ref[...], preferred_element_type=F32)
    v = jnp.dot(ckv, wv_ref[...], preferred_element_type=F32)
    ones = _ones_column(x.shape[0])
    for hh in range(MLA_HEADS):
        hs = slice(hh * LANES, (hh + 1) * LANES)
        lo = slice(hh * ATTN_WIDTH, hh * ATTN_WIDTH + LANES)
        hi = slice(hh * ATTN_WIDTH + LANES, (hh + 1) * ATTN_WIDTH)
        q_ref[:, lo] = q_nope[:, hs].astype(BF16)
        q_ref[:, hi] = ((q_ra[:, hs] * cos + q_rb[:, hs] * sin) * scale).astype(BF16)
        k_ref[:, lo] = k_nope[:, hs].astype(BF16)
        k_ref[:, hi] = kr
        v_ref[:, lo] = v[:, hs].astype(BF16)
        v_ref[:, hi] = ones


def _rot_half_cols(w, half):
    return jnp.concatenate([-w[..., half:], w[..., :half]], axis=-1)


def _mla_proj(x, sh, sc, g, w_in, q_norm_g, w_uq, kv_norm_g, w_ukv, cos, sin):
    B, S, D = x.shape
    H = MLA_HEADS
    t = TOKEN_TILE
    half = MLA_ROPE // 2
    w_kr = w_in[:, MLA_Q_LORA + MLA_KV_LORA:]
    zpad = jnp.zeros((D, LANES - MLA_ROPE), F32)
    w_in_ext = jnp.concatenate(
        [w_in[:, :MLA_Q_LORA + MLA_KV_LORA], w_kr, zpad, _rot_half_cols(w_kr, half), zpad],
        axis=1).astype(BF16)
    wq = w_uq.reshape(MLA_Q_LORA, H, MLA_NOPE + MLA_ROPE)
    wq_nope = wq[..., :MLA_NOPE].reshape(MLA_Q_LORA, H * MLA_NOPE).astype(BF16)
    wq_r = wq[..., MLA_NOPE:]
    pad = ((0, 0), (0, 0), (0, LANES - MLA_ROPE))
    wq_rope = jnp.pad(wq_r, pad).reshape(MLA_Q_LORA, H * LANES).astype(BF16)
    wq_rope_rot = jnp.pad(_rot_half_cols(wq_r, half), pad).reshape(MLA_Q_LORA, H * LANES).astype(BF16)
    wkv = w_ukv.reshape(MLA_KV_LORA, H, MLA_NOPE + MLA_V)
    wk_nope = wkv[..., :MLA_NOPE].reshape(MLA_KV_LORA, H * MLA_NOPE).astype(BF16)
    wv = wkv[..., MLA_NOPE:].reshape(MLA_KV_LORA, H * MLA_V).astype(BF16)

    tok = lambda w: pl.BlockSpec((None, t, w), lambda b, i: (b, i, 0))
    vec = lambda w: pl.BlockSpec((None, 1, w), lambda b, i: (b, 0, 0))
    full = lambda a: pl.BlockSpec(a.shape, lambda b, i: (0,) * a.ndim)
    g2 = g.reshape(1, D)
    qg2 = q_norm_g.reshape(1, MLA_Q_LORA)
    kvg2 = kv_norm_g.reshape(1, MLA_KV_LORA)
    wide = jax.ShapeDtypeStruct((B, S, H * ATTN_WIDTH), BF16)
    return pl.pallas_call(
        _mla_proj_kernel,
        out_shape=(wide, wide, wide),
        grid=(B, S // t),
        in_specs=[tok(D), vec(D), vec(D), full(g2), full(w_in_ext), full(qg2), full(wq_nope),
                  full(wq_rope), full(wq_rope_rot), full(kvg2), full(wk_nope), full(wv),
                  tok(LANES), tok(LANES)],
        out_specs=(tok(H * ATTN_WIDTH), tok(H * ATTN_WIDTH), tok(H * ATTN_WIDTH)),
        name="mla_proj",
    )(x, sh, sc, g2, w_in_ext, qg2, wq_nope, wq_rope, wq_rope_rot, kvg2, wk_nope, wv, cos, sin)


def _attn_kernel(*refs, moba, tq, bk, dv):
    if moba:
        (q_ref, k_ref, v_ref, km_ref, o_ref,
         qx_ref, s_a, s_b, p_a, p_b, al_a, al_b, m_sc, acc_sc) = refs
    else:
        q_ref, k_ref, v_ref, o_ref, s_a, s_b, p_a, p_b, al_a, al_b, m_sc, acc_sc = refs
        qx_ref = q_ref
    i = pl.program_id(2)

    if moba:
        q = q_ref[...]
        nb = km_ref.shape[0]
        gate = _nt_dot(km_ref[...].astype(BF16), q)
        blk = lax.broadcasted_iota(jnp.int32, (nb, tq), 0)
        blk_f = blk.astype(F32)
        own = (i * tq + lax.broadcasted_iota(jnp.int32, (1, tq), 1)) // MOBA_BLOCK
        gate = jnp.where(blk < own, gate, -jnp.inf)
        sel = jnp.where(blk == own, 1.0, 0.0)
        for _ in range(MOBA_TOPK):
            mx = jnp.max(gate, axis=0, keepdims=True)
            first = jnp.min(jnp.where(gate == mx, blk_f, float(nb)), axis=0, keepdims=True)
            hit = blk_f == first
            sel = jnp.where(hit & (mx > -jnp.inf), 1.0, sel)
            gate = jnp.where(hit, -jnp.inf, gate)
        bias = jnp.where(sel > 0.0, 0.0, NEG_INF)
        bias = jnp.concatenate([bias, jnp.zeros((LANES - nb, tq), F32)], axis=0)
        qx_ref[:, :LANES] = q
        qx_ref[:, LANES:] = bias.T.astype(BF16)

    m_sc[...] = jnp.full(m_sc.shape, SOFTMAX_M_INIT, F32)
    acc_sc[...] = jnp.zeros(acc_sc.shape, F32)
    p_b[...] = jnp.zeros(p_b.shape, BF16)
    al_b[...] = jnp.ones(al_b.shape, F32)
    row_id = lax.broadcasted_iota(jnp.int32, (tq, 1), 0)
    col_id = lax.broadcasted_iota(jnp.int32, (1, bk), 1)

    def scores(j, s_dst):
        start = pl.multiple_of(j * bk, bk)
        s_dst[...] = _nt_dot(qx_ref[...], k_ref[pl.ds(start, bk), :])

    def soft(s_src, p_dst, al_dst, diag_offset=None):
        s = s_src[...]
        if diag_offset is not None:
            s = jnp.where(col_id + diag_offset <= row_id, s, -jnp.inf)
        m_prev = m_sc[...]
        m_new = jnp.maximum(m_prev, jnp.max(s, axis=1, keepdims=True))
        p_dst[...] = jnp.exp2(s - m_new).astype(BF16)
        al_dst[...] = jnp.exp2(m_prev - m_new)
        m_sc[...] = m_new

    def fold(j, p_src, al_src):
        start = pl.multiple_of(j * bk, bk)
        acc_sc[...] = al_src[...] * acc_sc[...] + jnp.dot(
            p_src[...], v_ref[pl.ds(start, bk), :], preferred_element_type=F32)

    scores(0, s_a)

    def body(t, carry):
        j = 2 * t
        scores(j + 1, s_b)
        soft(s_a, p_a, al_a)
        fold(jnp.maximum(j - 1, 0), p_b, al_b)
        scores(j + 2, s_a)
        soft(s_b, p_b, al_b)
        fold(j, p_a, al_a)
        return carry

    lax.fori_loop(0, i, body, 0)
    j = 2 * i
    scores(j + 1, s_b)
    soft(s_a, p_a, al_a, diag_offset=0)
    fold(jnp.maximum(j - 1, 0), p_b, al_b)
    soft(s_b, p_b, al_b, diag_offset=bk)
    fold(j, p_a, al_a)
    fold(j + 1, p_b, al_b)
    acc = acc_sc[...]
    o_ref[...] = (acc[:, :dv] / acc[:, dv:dv + 1]).astype(o_ref.dtype)


def _attention(q, k, v, k_mean, *, heads, dq, dv, moba):
    B, S, _ = q.shape
    tq, bk, w = ATTN_Q_TILE, ATTN_KV_TILE, ATTN_WIDTH
    assert tq == 2 * bk and tq % MOBA_BLOCK == 0 and S % tq == 0
    in_specs = [
        pl.BlockSpec((None, tq, dq), lambda b, h, i: (b, i, h)),
        pl.BlockSpec((None, S, w), lambda b, h, i: (b, 0, h)),
        pl.BlockSpec((None, S, w), lambda b, h, i: (b, 0, h)),
    ]
    args = [q, k, v]
    scratch = []
    if moba:
        nb = k_mean.shape[1]
        in_specs.append(pl.BlockSpec((None, nb, dq), lambda b, h, i: (b, 0, h)))
        args.append(k_mean)
        scratch.append(pltpu.VMEM((tq, w), BF16))
    scratch += [pltpu.VMEM((tq, bk), F32), pltpu.VMEM((tq, bk), F32),
                pltpu.VMEM((tq, bk), BF16), pltpu.VMEM((tq, bk), BF16),
                pltpu.VMEM((tq, 1), F32), pltpu.VMEM((tq, 1), F32),
                pltpu.VMEM((tq, 1), F32), pltpu.VMEM((tq, w), F32)]
    return pl.pallas_call(
        functools.partial(_attn_kernel, moba=moba, tq=tq, bk=bk, dv=dv),
        out_shape=jax.ShapeDtypeStruct((B, S, heads * dv), BF16),
        grid=(B, heads, S // tq),
        in_specs=in_specs,
        out_specs=pl.BlockSpec((None, tq, dv), lambda b, h, i: (b, i, h)),
        scratch_shapes=scratch,
        name="moba_attention" if moba else "mla_attention",
    )(*args)


def _attn_out_kernel(a_ref, wo_ref, x_ref, ga_ref, g_ref, sh_ref, sc_ref, rw_ref, rb_ref,
                     x1_ref, h_ref, loc_ref, pw_ref, seg_ref, tot_ref, run_ref):
    first_step = (pl.program_id(0) == 0) & (pl.program_id(1) == 0)

    @pl.when(first_step)
    def _():
        run_ref[...] = jnp.zeros(run_ref.shape, F32)

    a = jnp.dot(a_ref[...], wo_ref[...], preferred_element_type=F32)
    x1 = x_ref[...] + ga_ref[...] * a
    x1_ref[...] = x1
    y = x1 * _rms_scale(x1) * g_ref[...]
    h = (y * (1.0 + sc_ref[...]) + sh_ref[...]).astype(BF16)
    h_ref[...] = h
    logits = jnp.dot(h, rw_ref[...], preferred_element_type=F32) + rb_ref[...]
    t = logits.shape[0]
    lane_i = lax.broadcasted_iota(jnp.int32, logits.shape, 1)
    lane = lane_i.astype(F32)
    logits = jnp.where(lane_i < N_EXPERTS, logits, -jnp.inf)
    top = None
    den = jnp.zeros((t, 1), F32)
    sel = jnp.zeros(logits.shape, F32)
    hits, ws = [], []
    for _ in range(TOP_K):
        mx = jnp.max(logits, axis=1, keepdims=True)
        first = jnp.min(jnp.where(logits == mx, lane, float(LANES)), axis=1, keepdims=True)
        hit = lane == first
        if top is None:
            top = mx
        w = jnp.exp(mx - top)
        den = den + w
        sel = jnp.where(hit, 1.0, sel)
        logits = jnp.where(hit, -jnp.inf, logits)
        hits.append(hit)
        ws.append(w)
    r_id = lax.broadcasted_iota(jnp.int32, (t, t), 0)
    c_id = lax.broadcasted_iota(jnp.int32, (t, t), 1)
    before = jnp.where(c_id < r_id, 1.0, 0.0).astype(BF16)
    prefix = jnp.dot(before, sel.astype(BF16), preferred_element_type=F32)
    cnt = jnp.sum(sel, axis=0, keepdims=True)
    units = jnp.floor((cnt + (SEG_ALIGN - 1.0)) * (1.0 / SEG_ALIGN))
    cnt_pad = units * SEG_ALIGN
    a_id = lax.broadcasted_iota(jnp.int32, (LANES, LANES), 0)
    b_id = lax.broadcasted_iota(jnp.int32, (LANES, LANES), 1)
    earlier = jnp.where(a_id < b_id, 1.0, 0.0).astype(BF16)
    seg_off = jnp.dot(jnp.broadcast_to(units, (8, LANES)).astype(BF16), earlier,
                      preferred_element_type=F32)[0:1] * SEG_ALIGN
    base = run_ref[...]
    run_ref[...] = base + cnt_pad
    tot_ref[...] = base + cnt_pad
    row8 = lax.broadcasted_iota(jnp.int32, (8, LANES), 0)
    seg = jnp.where(row8 == 0, base, jnp.where(row8 == 1, cnt_pad, jnp.where(row8 == 2, seg_off, 0.0)))
    seg_ref[...] = seg.astype(jnp.int32)
    local = prefix + seg_off
    loc = jnp.zeros(logits.shape, F32)
    pw = jnp.zeros(logits.shape, F32)
    for kk in range(TOP_K):
        row = jnp.sum(jnp.where(hits[kk], local, 0.0), axis=1, keepdims=True)
        loc = jnp.where(lane_i == kk, row, loc)
        pw = jnp.where(lane_i == kk, ws[kk] / den, pw)
    loc_ref[...] = loc.astype(jnp.int32)
    pw_ref[...] = pw


def _attn_out(attn, w_o, x, g_a, norm_g, sh_f, sc_f, router_w, router_b):
    B, S, D = x.shape
    t = TOKEN_TILE
    per = S // t
    rw = jnp.pad(router_w, ((0, 0), (0, LANES - N_EXPERTS))).astype(BF16)
    rb = jnp.pad(router_b, (0, LANES - N_EXPERTS)).reshape(1, LANES)
    wo = w_o.astype(BF16)
    g2 = norm_g.reshape(1, D)
    tok = lambda w: pl.BlockSpec((None, t, w), lambda b, i: (b, i, 0))
    vec = lambda w: pl.BlockSpec((None, 1, w), lambda b, i: (b, 0, 0))
    full = lambda a: pl.BlockSpec(a.shape, lambda b, i: (0,) * a.ndim)
    return pl.pallas_call(
        _attn_out_kernel,
        out_shape=(jax.ShapeDtypeStruct((B, S, D), F32),
                   jax.ShapeDtypeStruct((B, S, D), BF16),
                   jax.ShapeDtypeStruct((B, S, LANES), jnp.int32),
                   jax.ShapeDtypeStruct((B, S, LANES), F32),
                   jax.ShapeDtypeStruct((B * per, 8, LANES), jnp.int32),
                   jax.ShapeDtypeStruct((1, LANES), F32)),
        grid=(B, per),
        in_specs=[tok(attn.shape[-1]), full(wo), tok(D), vec(D), full(g2), vec(D), vec(D),
                  full(rw), full(rb)],
        out_specs=(tok(D), tok(D), tok(LANES), tok(LANES),
                   pl.BlockSpec((None, 8, LANES), lambda b, i: (b * per + i, 0, 0)),
                   pl.BlockSpec((1, LANES), lambda b, i: (0, 0))),
        scratch_shapes=[pltpu.VMEM((1, LANES), F32)],
        compiler_params=pltpu.CompilerParams(dimension_semantics=("arbitrary", "arbitrary")),
        name="attn_out_router",
    )(attn, wo, x, g_a, g2, sh_f, sc_f, rw, rb)


def _for_each_chunk(cnt, max_rows, fn):
    rows = max_rows
    while rows >= SEG_ALIGN:
        shift = rows.bit_length()
        done = (cnt >> shift) << shift

        @pl.when((cnt & rows) != 0)
        def _(done=done, rows=rows):
            fn(pl.multiple_of(done, SEG_ALIGN), rows)

        rows //= 2


def _dispatch_kernel(dst_ref, cnt_ref, off_ref, pad_start_ref, pad_cnt_ref, h_ref, loc_ref,
                     xs_ref, seg_buf, zero_buf, sem):
    i = pl.program_id(0)
    t = h_ref.shape[0]
    n_loc = seg_buf.shape[0]

    def copy_out(src_ref, src_row, dst_row, rows):
        return pltpu.make_async_copy(src_ref.at[pl.ds(src_row, rows), :],
                                     xs_ref.at[pl.ds(dst_row, rows), :], sem)

    @pl.when(i == 0)
    def _():
        zero_buf[...] = jnp.zeros(zero_buf.shape, F32)
        for wait in (False, True):
            def per_expert(e, carry, wait=wait):
                start = pad_start_ref[e]

                def chunk(done, rows):
                    cp = copy_out(zero_buf, 0, pl.multiple_of(start + done, SEG_ALIGN), rows)
                    cp.wait() if wait else cp.start()

                _for_each_chunk(pad_cnt_ref[e], zero_buf.shape[0], chunk)
                return carry

            lax.fori_loop(0, N_EXPERTS, per_expert, 0)

    loc_t = loc_ref[...].astype(F32).T
    l_id = lax.broadcasted_iota(jnp.int32, (n_loc, t), 0).astype(F32)
    onehot = jnp.zeros((n_loc, t), F32)
    for kk in range(TOP_K):
        onehot = jnp.where(l_id == loc_t[kk:kk + 1, :], 1.0, onehot)
    seg_buf[...] = jnp.dot(onehot.astype(BF16), h_ref[...], preferred_element_type=F32)

    for wait in (False, True):
        def per_expert(e, carry, wait=wait):
            idx = i * N_EXPERTS + e
            off = off_ref[idx]
            dst = dst_ref[idx]

            def chunk(done, rows):
                cp = copy_out(seg_buf, pl.multiple_of(off + done, SEG_ALIGN),
                              pl.multiple_of(dst + done, SEG_ALIGN), rows)
                cp.wait() if wait else cp.start()

            _for_each_chunk(cnt_ref[idx], t, chunk)
            return carry

        lax.fori_loop(0, N_EXPERTS, per_expert, 0)


def _dispatch(h, loc, dst, seg_cnt, seg_off, pad_start, pad_cnt, n_rows):
    N, D = h.shape
    t = TOKEN_TILE
    return pl.pallas_call(
        _dispatch_kernel,
        out_shape=jax.ShapeDtypeStruct((n_rows, D), F32),
        grid_spec=pltpu.PrefetchScalarGridSpec(
            num_scalar_prefetch=5,
            grid=(N // t,),
            in_specs=[pl.BlockSpec((t, D), lambda i, *_: (i, 0)),
                      pl.BlockSpec((t, LANES), lambda i, *_: (i, 0))],
            out_specs=pl.BlockSpec(memory_space=pl.ANY),
            scratch_shapes=[pltpu.VMEM((SEG_BUF_ROWS, D), F32),
                            pltpu.VMEM((EXPERT_TILE // 2, D), F32),
                            pltpu.SemaphoreType.DMA],
        ),
        compiler_params=pltpu.CompilerParams(dimension_semantics=("arbitrary",),
                                             vmem_limit_bytes=VMEM_LIMIT_BYTES),
        name="moe_dispatch",
    )(dst, seg_cnt, seg_off, pad_start, pad_cnt, h, loc)


def _experts_kernel(tile_e_ref, tile_blk_ref, nact_ref, xs_ref, wg_ref, bg_ref, wu_ref, bu_ref,
                    wd_ref, bd_ref, y_ref, wg_bf, wu_bf, wd_bf):
    j = pl.program_id(0)

    @pl.when(j < nact_ref[0])
    def _():
        prev = tile_e_ref[jnp.maximum(j - 1, 0)]

        @pl.when((j == 0) | (tile_e_ref[j] != prev))
        def _():
            wg_bf[...] = wg_ref[...].astype(BF16)
            wu_bf[...] = wu_ref[...].astype(BF16)
            wd_bf[...] = wd_ref[...].astype(BF16)

        x = xs_ref[...].astype(BF16)
        g = jnp.minimum(jnp.dot(x, wg_bf[...], preferred_element_type=F32) + bg_ref[...], SWIGLU_LIMIT)
        u = jnp.clip(jnp.dot(x, wu_bf[...], preferred_element_type=F32) + bu_ref[...],
                     -SWIGLU_LIMIT, SWIGLU_LIMIT)
        a = g * jax.nn.sigmoid(SWIGLU_ALPHA * g) * (u + 1.0)
        y_ref[...] = jnp.dot(a.astype(BF16), wd_bf[...], preferred_element_type=F32) + bd_ref[...]


def _experts(xs, tile_e, tile_blk, nact, layer, w_gate, b_gate, w_up, b_up, w_down, b_down):
    P, D = xs.shape
    L, E, _, F = w_gate.shape
    tm = EXPERT_TILE
    rows = pl.BlockSpec((tm, D), lambda j, te, tb, na: (tb[j], 0))
    wspec = lambda r, c: pl.BlockSpec((None, None, r, c), lambda j, te, tb, na: (layer, te[j], 0, 0))
    return pl.pallas_call(
        _experts_kernel,
        out_shape=jax.ShapeDtypeStruct((P, D), F32),
        grid_spec=pltpu.PrefetchScalarGridSpec(
            num_scalar_prefetch=3,
            grid=(P // tm,),
            in_specs=[rows, wspec(D, F), wspec(1, F), wspec(D, F), wspec(1, F), wspec(F, D), wspec(1, D)],
            out_specs=rows,
            scratch_shapes=[pltpu.VMEM((D, F), BF16), pltpu.VMEM((D, F), BF16), pltpu.VMEM((F, D), BF16)],
        ),
        compiler_params=pltpu.CompilerParams(dimension_semantics=("arbitrary",),
                                             vmem_limit_bytes=VMEM_LIMIT_BYTES),
        name="moe_experts",
    )(tile_e, tile_blk, nact, xs, w_gate, b_gate.reshape(L, E, 1, F), w_up, b_up.reshape(L, E, 1, F),
      w_down, b_down.reshape(L, E, 1, D))


def _combine_kernel(*refs, final_norm):
    if final_norm:
        (src_ref, cnt_ref, off_ref, y_ref, loc_ref, pw_ref, x_ref, gf_ref, fg_ref, fsh_ref, fsc_ref,
         o_ref, seg_buf, sem) = refs
    else:
        src_ref, cnt_ref, off_ref, y_ref, loc_ref, pw_ref, x_ref, gf_ref, o_ref, seg_buf, sem = refs
    i = pl.program_id(0)
    t = x_ref.shape[0]
    n_loc = seg_buf.shape[0]

    @pl.when(i == 0)
    def _():
        seg_buf[...] = jnp.zeros(seg_buf.shape, F32)

    for wait in (False, True):
        def per_expert(e, carry, wait=wait):
            idx = i * N_EXPERTS + e
            off = off_ref[idx]
            src = src_ref[idx]

            def chunk(done, rows):
                cp = pltpu.make_async_copy(
                    y_ref.at[pl.ds(pl.multiple_of(src + done, SEG_ALIGN), rows), :],
                    seg_buf.at[pl.ds(pl.multiple_of(off + done, SEG_ALIGN), rows), :], sem)
                cp.wait() if wait else cp.start()

            _for_each_chunk(cnt_ref[idx], t, chunk)
            return carry

        lax.fori_loop(0, N_EXPERTS, per_expert, 0)

    loc = loc_ref[...].astype(F32)
    pw = pw_ref[...]
    l_id = lax.broadcasted_iota(jnp.int32, (t, n_loc), 1).astype(F32)
    weights = jnp.zeros((t, n_loc), F32)
    for kk in range(TOP_K):
        weights = jnp.where(l_id == loc[:, kk:kk + 1], pw[:, kk:kk + 1], weights)
    f = jnp.dot(weights.astype(BF16), seg_buf[...].astype(BF16), preferred_element_type=F32)
    x2 = x_ref[...] + gf_ref[...] * f
    if final_norm:
        y = x2 * _rms_scale(x2) * fg_ref[...]
        x2 = y * (1.0 + fsc_ref[...]) + fsh_ref[...]
    o_ref[...] = x2


def _combine(y, loc, pw, src, seg_cnt, seg_off, x, g_f, final):
    B, S, D = x.shape
    t = TOKEN_TILE
    per = S // t
    tok = lambda w: pl.BlockSpec((None, t, w), lambda i, *_: (i // per, i % per, 0))
    vec = pl.BlockSpec((None, 1, D), lambda i, *_: (i // per, 0, 0))
    in_specs = [pl.BlockSpec(memory_space=pl.ANY), tok(LANES), tok(LANES), tok(D), vec]
    args = [y, loc, pw, x, g_f]
    if final is not None:
        fg, fsh, fsc = final
        in_specs += [pl.BlockSpec((1, D), lambda i, *_: (0, 0)), vec, vec]
        args += [fg.reshape(1, D), fsh, fsc]
    return pl.pallas_call(
        functools.partial(_combine_kernel, final_norm=final is not None),
        out_shape=jax.ShapeDtypeStruct((B, S, D), F32),
        grid_spec=pltpu.PrefetchScalarGridSpec(
            num_scalar_prefetch=3,
            grid=(B * per,),
            in_specs=in_specs,
            out_specs=tok(D),
            scratch_shapes=[pltpu.VMEM((SEG_BUF_ROWS, D), F32), pltpu.SemaphoreType.DMA],
        ),
        compiler_params=pltpu.CompilerParams(dimension_semantics=("arbitrary",),
                                             vmem_limit_bytes=VMEM_LIMIT_BYTES),
        name="moe_combine",
    )(src, seg_cnt, seg_off, *args)


def _moe(h, loc, pw, seg, totals, x, g_f, layer, w_gate, b_gate, w_up, b_up, w_down, b_down, final):
    B, S, D = x.shape
    N = B * S
    E = N_EXPERTS
    tm = EXPERT_TILE
    n_tok_tiles = N // TOKEN_TILE
    max_rows = N * TOP_K + n_tok_tiles * E * (SEG_ALIGN - 1)
    n_tiles = -(-max_rows // tm) + E
    tot = totals[0, :E].astype(jnp.int32)
    tiles_per = (tot + tm - 1) // tm
    tile_end = jnp.cumsum(tiles_per)
    row_start = (tile_end - tiles_per) * tm
    nact = tile_end[-1:]
    jj = jnp.minimum(jnp.arange(n_tiles, dtype=jnp.int32), nact[0] - 1)
    tile_e = jnp.sum((jj[:, None] >= tile_end[None, :]).astype(jnp.int32), axis=1)
    place = (row_start[None, :] + seg[:, 0, :E]).reshape(-1)
    seg_cnt = seg[:, 1, :E].reshape(-1)
    seg_off = seg[:, 2, :E].reshape(-1)
    loc2 = loc.reshape(N, LANES)
    xs = _dispatch(h.reshape(N, D), loc2, place, seg_cnt, seg_off, row_start + tot,
                   tiles_per * tm - tot, n_tiles * tm)
    y = _experts(xs, tile_e, jj, nact, layer, w_gate, b_gate, w_up, b_up, w_down, b_down)
    return _combine(y, loc, pw, place, seg_cnt, seg_off, x, g_f, final)


def _moba_proj_kernel(x_ref, sha_ref, sca_ref, ga_ref, shk_ref, sck_ref, gk_ref, wq_ref, wkv_ref,
                      cos_ref, sin_ref, q_ref, k_ref, v_ref, km_ref):
    x = x_ref[...]
    xn = x * _rms_scale(x)
    h = ((xn * ga_ref[...]) * (1.0 + sca_ref[...]) + sha_ref[...]).astype(BF16)
    hkv = ((xn * gk_ref[...]) * (1.0 + sck_ref[...]) + shk_ref[...]).astype(BF16)
    q = jnp.dot(h, wq_ref[...], preferred_element_type=F32)
    kv = jnp.dot(hkv, wkv_ref[...], preferred_element_type=F32)
    cos = cos_ref[...]
    sin = sin_ref[...]
    scale = LOG2_E * MOBA_HEAD_DIM ** -0.5
    hd = MOBA_HEAD_DIM
    t = x.shape[0]
    width = MOBA_HEADS * hd
    ones = _ones_column(t)
    seq_row = pl.program_id(1) * t + lax.broadcasted_iota(jnp.int32, (t, LANES), 0)
    lane = lax.broadcasted_iota(jnp.int32, (t, LANES), 1)
    block_id = jnp.where(seq_row // MOBA_BLOCK == lane, 1.0, 0.0).astype(BF16)
    for hh in range(MOBA_HEADS):
        hs = slice(hh * hd, (hh + 1) * hd)
        lo = slice(hh * ATTN_WIDTH, hh * ATTN_WIDTH + LANES)
        hi = slice(hh * ATTN_WIDTH + LANES, (hh + 1) * ATTN_WIDTH)
        qh = q[:, hs]
        kh = kv[:, hs]
        q_ref[:, hs] = ((qh * cos + pltpu.roll(qh, hd // 2, 1) * sin) * scale).astype(BF16)
        kr = kh * cos + pltpu.roll(kh, hd // 2, 1) * sin
        k_ref[:, lo] = kr.astype(BF16)
        k_ref[:, hi] = block_id
        km_ref[:, hs] = jnp.mean(kr.reshape(t // MOBA_BLOCK, MOBA_BLOCK, hd), axis=1)
        v_ref[:, lo] = kv[:, width + hh * hd:width + (hh + 1) * hd].astype(BF16)
        v_ref[:, hi] = ones


def _moba_proj(x, sh_a, sc_a, g_a, sh_k, sc_k, g_k, w_q, w_kv, cos, sin):
    B, S, D = x.shape
    t = TOKEN_TILE
    H = MOBA_HEADS
    width = H * MOBA_HEAD_DIM
    per = t // MOBA_BLOCK
    assert S // MOBA_BLOCK <= LANES
    tok = lambda w: pl.BlockSpec((None, t, w), lambda b, i: (b, i, 0))
    vec = lambda w: pl.BlockSpec((None, 1, w), lambda b, i: (b, 0, 0))
    full = lambda a: pl.BlockSpec(a.shape, lambda b, i: (0,) * a.ndim)
    ga2 = g_a.reshape(1, D)
    gk2 = g_k.reshape(1, D)
    wq = w_q.astype(BF16)
    wkv = w_kv.astype(BF16)
    wide = jax.ShapeDtypeStruct((B, S, H * ATTN_WIDTH), BF16)
    q, k, v, km = pl.pallas_call(
        _moba_proj_kernel,
        out_shape=(jax.ShapeDtypeStruct((B, S, width), BF16), wide, wide,
                   jax.ShapeDtypeStruct((B, S // t, per, width), F32)),
        grid=(B, S // t),
        in_specs=[tok(D), vec(D), vec(D), full(ga2), vec(D), vec(D), full(gk2), full(wq), full(wkv),
                  tok(LANES), tok(LANES)],
        out_specs=(tok(width), tok(H * ATTN_WIDTH), tok(H * ATTN_WIDTH),
                   pl.BlockSpec((None, None, per, width), lambda b, i: (b, i, 0, 0))),
        name="moba_proj",
    )(x, sh_a, sc_a, ga2, sh_k, sc_k, gk2, wq, wkv, cos, sin)
    return q, k, v, km.reshape(B, S // MOBA_BLOCK, width)


def _split_mod(mod, n):
    return [mod[:, None, j * D_MODEL:(j + 1) * D_MODEL] for j in range(n)]


def kernel(x, c, positions, ada_w, ada_b, norm_attn_g, norm_ffn_g, mla_w_in, mla_q_norm_g, mla_w_uq, mla_kv_norm_g, mla_w_ukv, mla_w_o, kv_ada_w, kv_ada_b, kv_norm_g, moba_w_kv, moba_w_q, moba_w_o, router_w, router_b, w_gate, b_gate, w_up, b_up, w_down, b_down, final_ada_w, final_ada_b, final_norm_g):
    mods = _ada_linear(c, ada_w, ada_b)
    kv_mod = _ada_linear(c, kv_ada_w[None], kv_ada_b[None])[0]
    f_mod = _ada_linear(c, final_ada_w[None], final_ada_b[None])[0]
    cos_a, sin_a, cos_b, sin_b = _rope_tables(positions)
    f_sh, f_sc = _split_mod(f_mod, 2)
    shared = None
    for layer in range(DEPTH):
        sh_a, sc_a, g_a, sh_f, sc_f, g_f = _split_mod(mods[layer], 6)
        if layer < N_A:
            q, k, v = _mla_proj(x, sh_a, sc_a, norm_attn_g[layer], mla_w_in[layer], mla_q_norm_g[layer],
                                mla_w_uq[layer], mla_kv_norm_g[layer], mla_w_ukv[layer], cos_a, sin_a)
            attn = _attention(q, k, v, None, heads=MLA_HEADS, dq=ATTN_WIDTH, dv=MLA_V, moba=False)
            w_o = mla_w_o[layer]
        else:
            j = layer - N_A
            kv_sh, kv_sc = _split_mod(kv_mod, 2)
            q, k, v, km = _moba_proj(x, sh_a, sc_a, norm_attn_g[layer], kv_sh, kv_sc, kv_norm_g,
                                     moba_w_q[j], moba_w_kv, cos_b, sin_b)
            if shared is None:
                shared = (k, v, km)
            attn = _attention(q, shared[0], shared[1], shared[2], heads=MOBA_HEADS, dq=MOBA_HEAD_DIM,
                              dv=MOBA_HEAD_DIM, moba=True)
            w_o = moba_w_o[j]
        x, h, loc, pw, seg, totals = _attn_out(attn, w_o, x, g_a, norm_ffn_g[layer], sh_f, sc_f,
                                               router_w[layer], router_b[layer])
        final = (final_norm_g, f_sh, f_sc) if layer == DEPTH - 1 else None
        x = _moe(h, loc, pw, seg, totals, x, g_f, layer, w_gate, b_gate, w_up, b_up, w_down, b_down,
                 final)
    return x
```

```python
import functools

import jax
import jax.numpy as jnp
from jax import lax
from jax.experimental import pallas as pl
from jax.experimental.pallas import tpu as pltpu

D_MODEL = 1024
DEPTH = 2
N_A = DEPTH // 2

MLA_HEADS = 8
MLA_Q_LORA = 256
MLA_KV_LORA = 128
MLA_NOPE = 128
MLA_ROPE = 64
MLA_V = 128

MOBA_HEADS = 8
MOBA_HEAD_DIM = D_MODEL // MOBA_HEADS
MOBA_BLOCK = 256
MOBA_TOPK = 3

N_EXPERTS = 32
TOP_K = 4
SWIGLU_LIMIT = 7.0
SWIGLU_ALPHA = 1.702

ROPE_THETA = 10000.0
NORM_EPS = 1e-6
NEG_INF = -1e30
LOG2_E = 1.4426950408889634
SOFTMAX_M_INIT = -1e20

LANES = 128
TOKEN_TILE = 512
ATTN_Q_TILE = 1024
ATTN_KV_TILE = 512
ATTN_WIDTH = 2 * LANES
ATTN_HEADS_PER_STEP = 2
EXPERT_TILE = 512
SEG_ALIGN = 8
SEG_BUF_ROWS = TOKEN_TILE * TOP_K + N_EXPERTS * SEG_ALIGN
VMEM_LIMIT_BYTES = 56 * 1024 * 1024

F32 = jnp.float32
BF16 = jnp.bfloat16


def _rms_scale(x):
    return lax.rsqrt(jnp.mean(x * x, axis=-1, keepdims=True) + NORM_EPS)


def _nt_dot(a, b):
    return lax.dot_general(a, b, (((1,), (1,)), ((), ())), preferred_element_type=F32)


def _ones_column(rows):
    lane = lax.broadcasted_iota(jnp.int32, (rows, LANES), 1)
    return jnp.where(lane == 0, 1.0, 0.0).astype(BF16)


def _ada_kernel(c_ref, w_ref, b_ref, o_ref):
    c = c_ref[...]
    ca = c * jax.nn.sigmoid(c)
    o_ref[...] = jnp.dot(ca, w_ref[...], precision=lax.Precision.HIGHEST,
                         preferred_element_type=F32) + b_ref[...]


def _ada_linear(c, w, b):
    L, D, M = w.shape
    B = c.shape[0]
    bn = 1024
    return pl.pallas_call(
        _ada_kernel,
        out_shape=jax.ShapeDtypeStruct((L, B, M), F32),
        grid=(L, M // bn),
        in_specs=[
            pl.BlockSpec((B, D), lambda l, j: (0, 0)),
            pl.BlockSpec((None, D, bn), lambda l, j: (l, 0, j)),
            pl.BlockSpec((None, 1, bn), lambda l, j: (l, 0, j)),
        ],
        out_specs=pl.BlockSpec((None, B, bn), lambda l, j: (l, 0, j)),
        name="ada_linear",
    )(c, w, b.reshape(L, 1, M))


def _rope_table_kernel(pos_ref, inv_a_ref, inv_b_ref, sign_ref, ca_ref, sa_ref, cb_ref, sb_ref):
    pos = pos_ref[...]
    ang_a = pos * inv_a_ref[...]
    ang_b = pos * inv_b_ref[...]
    ca_ref[...] = jnp.cos(ang_a)
    sa_ref[...] = jnp.sin(ang_a)
    cb_ref[...] = jnp.cos(ang_b)
    sb_ref[...] = jnp.sin(ang_b) * sign_ref[...]


def _rope_tables(positions):
    B, S = positions.shape
    t = 1024

    def inv_freq(d):
        half = d // 2
        return ROPE_THETA ** (-jnp.arange(half, dtype=F32) * (2.0 / d))

    inv_a = jnp.tile(inv_freq(MLA_ROPE), LANES // (MLA_ROPE // 2)).reshape(1, LANES)
    inv_b = jnp.tile(inv_freq(MOBA_HEAD_DIM), 2).reshape(1, LANES)
    half = MOBA_HEAD_DIM // 2
    sign = jnp.concatenate([-jnp.ones((half,), F32), jnp.ones((half,), F32)]).reshape(1, LANES)
    pos = positions.astype(F32).reshape(B, S, 1)
    row = pl.BlockSpec((1, LANES), lambda b, i: (0, 0))
    out = pl.BlockSpec((None, t, LANES), lambda b, i: (b, i, 0))
    shp = jax.ShapeDtypeStruct((B, S, LANES), F32)
    return pl.pallas_call(
        _rope_table_kernel,
        out_shape=(shp, shp, shp, shp),
        grid=(B, S // t),
        in_specs=[pl.BlockSpec((None, t, 1), lambda b, i: (b, i, 0)), row, row, row],
        out_specs=(out, out, out, out),
        name="rope_tables",
    )(pos, inv_a, inv_b, sign)


def _mla_proj_kernel(x_ref, sh_ref, sc_ref, g_ref, win_ref, qg_ref, wqn_ref, wqr_ref, wqrr_ref,
                     kvg_ref, wkn_ref, wv_ref, cos_ref, sin_ref, q_ref, k_ref, v_ref):
    x = x_ref[...]
    y = x * _rms_scale(x) * g_ref[...]
    h = (y * (1.0 + sc_ref[...]) + sh_ref[...]).astype(BF16)
    proj = jnp.dot(h, win_ref[...], preferred_element_type=F32)
    c_q = proj[:, :MLA_Q_LORA]
    c_kv = proj[:, MLA_Q_LORA:MLA_Q_LORA + MLA_KV_LORA]
    kr_a = proj[:, 384:512]
    kr_b = proj[:, 512:640]
    cq = (c_q * _rms_scale(c_q) * qg_ref[...]).astype(BF16)
    ckv = (c_kv * _rms_scale(c_kv) * kvg_ref[...]).astype(BF16)
    cos = cos_ref[...]
    sin = sin_ref[...]
    scale = LOG2_E * (MLA_NOPE + MLA_ROPE) ** -0.5
    kr = (kr_a * cos + kr_b * sin).astype(BF16)
    q_nope = jnp.dot(cq, wqn_ref[...], preferred_element_type=F32) * scale
    q_ra = jnp.dot(cq, wqr_ref[...], preferred_element_type=F32)
    q_rb = jnp.dot(cq, wqrr_ref[...], preferred_element_type=F32)
    k_nope = jnp.dot(ckv, wkn_ref[...], preferred_element_type=F32)
    v = jnp.dot(ckv, wv_ref[...], preferred_element_type=F32)
    ones = _ones_column(x.shape[0])
    for hh in range(MLA_HEADS):
        hs = slice(hh * LANES, (hh + 1) * LANES)
        lo = slice(hh * ATTN_WIDTH, hh * ATTN_WIDTH + LANES)
        hi = slice(hh * ATTN_WIDTH + LANES, (hh + 1) * ATTN_WIDTH)
        q_ref[:, lo] = q_nope[:, hs].astype(BF16)
        q_ref[:, hi] = ((q_ra[:, hs] * cos + q_rb[:, hs] * sin) * scale).astype(BF16)
        k_ref[:, lo] = k_nope[:, hs].astype(BF16)
        k_ref[:, hi] = kr
        v_ref[:, lo] = v[:, hs].astype(BF16)
        v_ref[:, hi] = ones


def _rot_half_cols(w, half):
    return jnp.concatenate([-w[..., half:], w[..., :half]], axis=-1)


def _mla_proj(x, sh, sc, g, w_in, q_norm_g, w_uq, kv_norm_g, w_ukv, cos, sin):
    B, S, D = x.shape
    H = MLA_HEADS
    t = TOKEN_TILE
    half = MLA_ROPE // 2
    w_kr = w_in[:, MLA_Q_LORA + MLA_KV_LORA:]
    zpad = jnp.zeros((D, LANES - MLA_ROPE), F32)
    w_in_ext = jnp.concatenate(
        [w_in[:, :MLA_Q_LORA + MLA_KV_LORA], w_kr, zpad, _rot_half_cols(w_kr, half), zpad],
        axis=1).astype(BF16)
    wq = w_uq.reshape(MLA_Q_LORA, H, MLA_NOPE + MLA_ROPE)
    wq_nope = wq[..., :MLA_NOPE].reshape(MLA_Q_LORA, H * MLA_NOPE).astype(BF16)
    wq_r = wq[..., MLA_NOPE:]
    pad = ((0, 0), (0, 0), (0, LANES - MLA_ROPE))
    wq_rope = jnp.pad(wq_r, pad).reshape(MLA_Q_LORA, H * LANES).astype(BF16)
    wq_rope_rot = jnp.pad(_rot_half_cols(wq_r, half), pad).reshape(MLA_Q_LORA, H * LANES).astype(BF16)
    wkv = w_ukv.reshape(MLA_KV_LORA, H, MLA_NOPE + MLA_V)
    wk_nope = wkv[..., :MLA_NOPE].reshape(MLA_KV_LORA, H * MLA_NOPE).astype(BF16)
    wv = wkv[..., MLA_NOPE:].reshape(MLA_KV_LORA, H * MLA_V).astype(BF16)

    tok = lambda w: pl.BlockSpec((None, t, w), lambda b, i: (b, i, 0))
    vec = lambda w: pl.BlockSpec((None, 1, w), lambda b, i: (b, 0, 0))
    full = lambda a: pl.BlockSpec(a.shape, lambda b, i: (0,) * a.ndim)
    g2 = g.reshape(1, D)
    qg2 = q_norm_g.reshape(1, MLA_Q_LORA)
    kvg2 = kv_norm_g.reshape(1, MLA_KV_LORA)
    wide = jax.ShapeDtypeStruct((B, S, H * ATTN_WIDTH), BF16)
    return pl.pallas_call(
        _mla_proj_kernel,
        out_shape=(wide, wide, wide),
        grid=(B, S // t),
        in_specs=[tok(D), vec(D), vec(D), full(g2), full(w_in_ext), full(qg2), full(wq_nope),
                  full(wq_rope), full(wq_rope_rot), full(kvg2), full(wk_nope), full(wv),
                  tok(LANES), tok(LANES)],
        out_specs=(tok(H * ATTN_WIDTH), tok(H * ATTN_WIDTH), tok(H * ATTN_WIDTH)),
        name="mla_proj",
    )(x, sh, sc, g2, w_in_ext, qg2, wq_nope, wq_rope, wq_rope_rot, kvg2, wk_nope, wv, cos, sin)


def _attn_kernel(*refs, moba, tq, bk, dq, dv, nh):
    n_in = 4 if moba else 3
    q_ref, k_ref, v_ref = refs[:3]
    o_ref = refs[n_in]
    scratch = refs[n_in + 1:]
    per_head = len(scratch) // nh
    w = ATTN_WIDTH
    i = pl.program_id(2)
    row_id = lax.broadcasted_iota(jnp.int32, (tq, 1), 0)
    col_id = lax.broadcasted_iota(jnp.int32, (1, bk), 1)

    def head_scratch(hh):
        sc = scratch[hh * per_head:(hh + 1) * per_head]
        return sc if moba else (None,) + tuple(sc)

    def prepare(hh):
        qx_ref, s_a, s_b, p_a, p_b, al_a, al_b, m_sc, acc_sc = head_scratch(hh)
        if moba:
            km_ref = refs[3]
            q = q_ref[:, hh * dq:(hh + 1) * dq]
            nb = km_ref.shape[0]
            gate = _nt_dot(km_ref[:, hh * dq:(hh + 1) * dq].astype(BF16), q)
            blk = lax.broadcasted_iota(jnp.int32, (nb, tq), 0)
            blk_f = blk.astype(F32)
            own = (i * tq + lax.broadcasted_iota(jnp.int32, (1, tq), 1)) // MOBA_BLOCK
            gate = jnp.where(blk < own, gate, -jnp.inf)
            sel = jnp.where(blk == own, 1.0, 0.0)
            for _ in range(MOBA_TOPK):
                mx = jnp.max(gate, axis=0, keepdims=True)
                first = jnp.min(jnp.where(gate == mx, blk_f, float(nb)), axis=0, keepdims=True)
                hit = blk_f == first
                sel = jnp.where(hit & (mx > -jnp.inf), 1.0, sel)
                gate = jnp.where(hit, -jnp.inf, gate)
            bias = jnp.where(sel > 0.0, 0.0, NEG_INF)
            bias = jnp.concatenate([bias, jnp.zeros((LANES - nb, tq), F32)], axis=0)
            qx_ref[:, :LANES] = q
            qx_ref[:, LANES:] = bias.T.astype(BF16)
        m_sc[...] = jnp.full(m_sc.shape, SOFTMAX_M_INIT, F32)
        acc_sc[...] = jnp.zeros(acc_sc.shape, F32)
        p_b[...] = jnp.zeros(p_b.shape, BF16)
        al_b[...] = jnp.ones(al_b.shape, F32)

    def scores(hh, j, slot, rows=slice(None)):
        sc = head_scratch(hh)
        qx = sc[0][rows, :] if moba else q_ref[rows, hh * w:(hh + 1) * w]
        start = pl.multiple_of(j * bk, bk)
        sc[1 + slot][rows, :] = _nt_dot(qx, k_ref[pl.ds(start, bk), hh * w:(hh + 1) * w])

    def soft(hh, slot, diag_offset=None, rows=slice(None)):
        sc = head_scratch(hh)
        s_src, p_dst, al_dst, m_sc = sc[1 + slot], sc[3 + slot], sc[5 + slot], sc[7]
        s = s_src[rows, :]
        if diag_offset is not None:
            s = jnp.where(col_id + diag_offset <= row_id[rows, :], s, -jnp.inf)
        m_prev = m_sc[rows, :]
        m_new = jnp.maximum(m_prev, jnp.max(s, axis=1, keepdims=True))
        p_dst[rows, :] = jnp.exp2(s - m_new).astype(BF16)
        al_dst[rows, :] = jnp.exp2(m_prev - m_new)
        m_sc[rows, :] = m_new

    def fold(hh, j, slot, rows=slice(None)):
        sc = head_scratch(hh)
        p_src, al_src, acc_sc = sc[3 + slot], sc[5 + slot], sc[8]
        start = pl.multiple_of(j * bk, bk)
        acc_sc[rows, :] = al_src[rows, :] * acc_sc[rows, :] + jnp.dot(
            p_src[rows, :], v_ref[pl.ds(start, bk), hh * w:(hh + 1) * w], preferred_element_type=F32)

    hs = range(nh)
    for hh in hs:
        prepare(hh)
    for hh in hs:
        scores(hh, 0, 0)

    def body(t, carry):
        j = 2 * t
        for hh in hs:
            scores(hh, j + 1, 1)
        for hh in hs:
            soft(hh, 0)
        for hh in hs:
            fold(hh, jnp.maximum(j - 1, 0), 1)
        for hh in hs:
            scores(hh, j + 2, 0)
        for hh in hs:
            soft(hh, 1)
        for hh in hs:
            fold(hh, j, 0)
        return carry

    lax.fori_loop(0, i, body, 0)
    j = 2 * i
    late = slice(bk, tq)
    for hh in hs:
        scores(hh, j + 1, 1, late)
    for hh in hs:
        soft(hh, 0, diag_offset=0)
    for hh in hs:
        fold(hh, jnp.maximum(j - 1, 0), 1)
    for hh in hs:
        soft(hh, 1, diag_offset=bk, rows=late)
    for hh in hs:
        fold(hh, j, 0)
    for hh in hs:
        fold(hh, j + 1, 1, late)
    for hh in hs:
        acc = head_scratch(hh)[8][...]
        o_ref[:, hh * dv:(hh + 1) * dv] = (acc[:, :dv] / acc[:, dv:dv + 1]).astype(o_ref.dtype)


def _attention(q, k, v, k_mean, *, heads, dq, dv, moba):
    B, S, _ = q.shape
    tq, bk, w, nh = ATTN_Q_TILE, ATTN_KV_TILE, ATTN_WIDTH, ATTN_HEADS_PER_STEP
    assert tq == 2 * bk and tq % MOBA_BLOCK == 0 and S % tq == 0 and heads % nh == 0
    in_specs = [
        pl.BlockSpec((None, tq, nh * dq), lambda b, h, i: (b, i, h)),
        pl.BlockSpec((None, S, nh * w), lambda b, h, i: (b, 0, h)),
        pl.BlockSpec((None, S, nh * w), lambda b, h, i: (b, 0, h)),
    ]
    args = [q, k, v]
    scratch = []
    if moba:
        nb = k_mean.shape[1]
        in_specs.append(pl.BlockSpec((None, nb, nh * dq), lambda b, h, i: (b, 0, h)))
        args.append(k_mean)
    for _ in range(nh):
        if moba:
            scratch.append(pltpu.VMEM((tq, w), BF16))
        scratch += [pltpu.VMEM((tq, bk), F32), pltpu.VMEM((tq, bk), F32),
                    pltpu.VMEM((tq, bk), BF16), pltpu.VMEM((tq, bk), BF16),
                    pltpu.VMEM((tq, 1), F32), pltpu.VMEM((tq, 1), F32),
                    pltpu.VMEM((tq, 1), F32), pltpu.VMEM((tq, w), F32)]
    return pl.pallas_call(
        functools.partial(_attn_kernel, moba=moba, tq=tq, bk=bk, dq=dq, dv=dv, nh=nh),
        out_shape=jax.ShapeDtypeStruct((B, S, heads * dv), BF16),
        grid=(B, heads // nh, S // tq),
        in_specs=in_specs,
        out_specs=pl.BlockSpec((None, tq, nh * dv), lambda b, h, i: (b, i, h)),
        scratch_shapes=scratch,
        compiler_params=pltpu.CompilerParams(vmem_limit_bytes=VMEM_LIMIT_BYTES),
        name="moba_attention" if moba else "mla_attention",
    )(*args)


def _attn_out_kernel(a_ref, wo_ref, x_ref, ga_ref, g_ref, sh_ref, sc_ref, rw_ref, rb_ref,
                     x1_ref, h_ref, loc_ref, pw_ref, seg_ref, tot_ref, run_ref):
    first_step = (pl.program_id(0) == 0) & (pl.program_id(1) == 0)

    @pl.when(first_step)
    def _():
        run_ref[...] = jnp.zeros(run_ref.shape, F32)

    a = jnp.dot(a_ref[...], wo_ref[...], preferred_element_type=F32)
    x1 = x_ref[...] + ga_ref[...] * a
    x1_ref[...] = x1
    y = x1 * _rms_scale(x1) * g_ref[...]
    h = (y * (1.0 + sc_ref[...]) + sh_ref[...]).astype(BF16)
    h_ref[...] = h
    logits = jnp.dot(h, rw_ref[...], preferred_element_type=F32) + rb_ref[...]
    t = logits.shape[0]
    lane_i = lax.broadcasted_iota(jnp.int32, logits.shape, 1)
    lane = lane_i.astype(F32)
    logits = jnp.where(lane_i < N_EXPERTS, logits, -jnp.inf)
    top = None
    den = jnp.zeros((t, 1), F32)
    sel = jnp.zeros(logits.shape, F32)
    hits, ws = [], []
    for _ in range(TOP_K):
        mx = jnp.max(logits, axis=1, keepdims=True)
        first = jnp.min(jnp.where(logits == mx, lane, float(LANES)), axis=1, keepdims=True)
        hit = lane == first
        if top is None:
            top = mx
        w = jnp.exp(mx - top)
        den = den + w
        sel = jnp.where(hit, 1.0, sel)
        logits = jnp.where(hit, -jnp.inf, logits)
        hits.append(hit)
        ws.append(w)
    r_id = lax.broadcasted_iota(jnp.int32, (t, t), 0)
    c_id = lax.broadcasted_iota(jnp.int32, (t, t), 1)
    before = jnp.where(c_id < r_id, 1.0, 0.0).astype(BF16)
    prefix = jnp.dot(before, sel.astype(BF16), preferred_element_type=F32)
    cnt = jnp.sum(sel, axis=0, keepdims=True)
    units = jnp.floor((cnt + (SEG_ALIGN - 1.0)) * (1.0 / SEG_ALIGN))
    cnt_pad = units * SEG_ALIGN
    a_id = lax.broadcasted_iota(jnp.int32, (LANES, LANES), 0)
    b_id = lax.broadcasted_iota(jnp.int32, (LANES, LANES), 1)
    earlier = jnp.where(a_id < b_id, 1.0, 0.0).astype(BF16)
    seg_off = jnp.dot(jnp.broadcast_to(units, (8, LANES)).astype(BF16), earlier,
                      preferred_element_type=F32)[0:1] * SEG_ALIGN
    base = run_ref[...]
    run_ref[...] = base + cnt_pad
    tot_ref[...] = base + cnt_pad
    row8 = lax.broadcasted_iota(jnp.int32, (8, LANES), 0)
    seg = jnp.where(row8 == 0, base, jnp.where(row8 == 1, cnt_pad, jnp.where(row8 == 2, seg_off, 0.0)))
    seg_ref[...] = seg.astype(jnp.int32)
    local = prefix + seg_off
    loc = jnp.zeros(logits.shape, F32)
    pw = jnp.zeros(logits.shape, F32)
    for kk in range(TOP_K):
        row = jnp.sum(jnp.where(hits[kk], local, 0.0), axis=1, keepdims=True)
        loc = jnp.where(lane_i == kk, row, loc)
        pw = jnp.where(lane_i == kk, ws[kk] / den, pw)
    loc_ref[...] = loc.astype(jnp.int32)
    pw_ref[...] = pw


def _attn_out(attn, w_o, x, g_a, norm_g, sh_f, sc_f, router_w, router_b):
    B, S, D = x.shape
    t = TOKEN_TILE
    per = S // t
    rw = jnp.pad(router_w, ((0, 0), (0, LANES - N_EXPERTS))).astype(BF16)
    rb = jnp.pad(router_b, (0, LANES - N_EXPERTS)).reshape(1, LANES)
    wo = w_o.astype(BF16)
    g2 = norm_g.reshape(1, D)
    tok = lambda w: pl.BlockSpec((None, t, w), lambda b, i: (b, i, 0))
    vec = lambda w: pl.BlockSpec((None, 1, w), lambda b, i: (b, 0, 0))
    full = lambda a: pl.BlockSpec(a.shape, lambda b, i: (0,) * a.ndim)
    return pl.pallas_call(
        _attn_out_kernel,
        out_shape=(jax.ShapeDtypeStruct((B, S, D), F32),
                   jax.ShapeDtypeStruct((B, S, D), BF16),
                   jax.ShapeDtypeStruct((B, S, LANES), jnp.int32),
                   jax.ShapeDtypeStruct((B, S, LANES), F32),
                   jax.ShapeDtypeStruct((B * per, 8, LANES), jnp.int32),
                   jax.ShapeDtypeStruct((1, LANES), F32)),
        grid=(B, per),
        in_specs=[tok(attn.shape[-1]), full(wo), tok(D), vec(D), full(g2), vec(D), vec(D),
                  full(rw), full(rb)],
        out_specs=(tok(D), tok(D), tok(LANES), tok(LANES),
                   pl.BlockSpec((None, 8, LANES), lambda b, i: (b * per + i, 0, 0)),
                   pl.BlockSpec((1, LANES), lambda b, i: (0, 0))),
        scratch_shapes=[pltpu.VMEM((1, LANES), F32)],
        compiler_params=pltpu.CompilerParams(dimension_semantics=("arbitrary", "arbitrary")),
        name="attn_out_router",
    )(attn, wo, x, g_a, g2, sh_f, sc_f, rw, rb)


def _for_each_chunk(cnt, max_rows, fn):
    rows = max_rows
    while rows >= SEG_ALIGN:
        shift = rows.bit_length()
        done = (cnt >> shift) << shift

        @pl.when((cnt & rows) != 0)
        def _(done=done, rows=rows):
            fn(pl.multiple_of(done, SEG_ALIGN), rows)

        rows //= 2


def _dispatch_kernel(dst_ref, cnt_ref, off_ref, pad_start_ref, pad_cnt_ref, h_ref, loc_ref,
                     xs_ref, seg_buf, zero_buf, sems):
    i = pl.program_id(0)
    last = pl.num_programs(0) - 1
    slot = lax.rem(i, 2)
    t = h_ref.shape[0]
    n_loc = seg_buf.shape[1]
    pad_sem = sems.at[2]

    @pl.when(i == 0)
    def _():
        zero_buf[...] = jnp.zeros(zero_buf.shape, F32)
        for wait in (False, True):
            def per_expert(e, carry, wait=wait):
                start = pad_start_ref[e]

                def chunk(done, rows):
                    cp = pltpu.make_async_copy(
                        zero_buf.at[pl.ds(0, rows), :],
                        xs_ref.at[pl.ds(pl.multiple_of(start + done, SEG_ALIGN), rows), :], pad_sem)
                    cp.wait() if wait else cp.start()

                _for_each_chunk(pad_cnt_ref[e], zero_buf.shape[0], chunk)
                return carry

            lax.fori_loop(0, N_EXPERTS, per_expert, 0)

    def segment_copies(tile, buf, wait):
        def per_expert(e, carry):
            idx = tile * N_EXPERTS + e
            off = off_ref[idx]
            dst = dst_ref[idx]

            def chunk(done, rows):
                cp = pltpu.make_async_copy(
                    seg_buf.at[buf, pl.ds(pl.multiple_of(off + done, SEG_ALIGN), rows), :],
                    xs_ref.at[pl.ds(pl.multiple_of(dst + done, SEG_ALIGN), rows), :], sems.at[buf])
                cp.wait() if wait else cp.start()

            _for_each_chunk(cnt_ref[idx], t, chunk)
            return carry

        lax.fori_loop(0, N_EXPERTS, per_expert, 0)

    loc_t = loc_ref[...].astype(F32).T
    l_id = lax.broadcasted_iota(jnp.int32, (n_loc, t), 0).astype(F32)
    onehot = jnp.zeros((n_loc, t), F32)
    for kk in range(TOP_K):
        onehot = jnp.where(l_id == loc_t[kk:kk + 1, :], 1.0, onehot)
    seg_buf[slot] = jnp.dot(onehot.astype(BF16), h_ref[...], preferred_element_type=F32)
    segment_copies(i, slot, wait=False)

    @pl.when(i > 0)
    def _():
        segment_copies(i - 1, 1 - slot, wait=True)

    @pl.when(i == last)
    def _():
        segment_copies(i, slot, wait=True)


def _dispatch(h, loc, dst, seg_cnt, seg_off, pad_start, pad_cnt, n_rows):
    N, D = h.shape
    t = TOKEN_TILE
    return pl.pallas_call(
        _dispatch_kernel,
        out_shape=jax.ShapeDtypeStruct((n_rows, D), F32),
        grid_spec=pltpu.PrefetchScalarGridSpec(
            num_scalar_prefetch=5,
            grid=(N // t,),
            in_specs=[pl.BlockSpec((t, D), lambda i, *_: (i, 0)),
                      pl.BlockSpec((t, LANES), lambda i, *_: (i, 0))],
            out_specs=pl.BlockSpec(memory_space=pl.ANY),
            scratch_shapes=[pltpu.VMEM((2, SEG_BUF_ROWS, D), F32),
                            pltpu.VMEM((EXPERT_TILE // 2, D), F32),
                            pltpu.SemaphoreType.DMA((3,))],
        ),
        compiler_params=pltpu.CompilerParams(dimension_semantics=("arbitrary",),
                                             vmem_limit_bytes=VMEM_LIMIT_BYTES),
        name="moe_dispatch",
    )(dst, seg_cnt, seg_off, pad_start, pad_cnt, h, loc)


def _experts_kernel(tile_e_ref, tile_blk_ref, nact_ref, xs_ref, wg_ref, bg_ref, wu_ref, bu_ref,
                    wd_ref, bd_ref, y_ref, wg_bf, wu_bf, wd_bf):
    j = pl.program_id(0)

    @pl.when(j < nact_ref[0])
    def _():
        prev = tile_e_ref[jnp.maximum(j - 1, 0)]

        @pl.when((j == 0) | (tile_e_ref[j] != prev))
        def _():
            wg_bf[...] = wg_ref[...].astype(BF16)
            wu_bf[...] = wu_ref[...].astype(BF16)
            wd_bf[...] = wd_ref[...].astype(BF16)

        x = xs_ref[...].astype(BF16)
        g = jnp.minimum(jnp.dot(x, wg_bf[...], preferred_element_type=F32) + bg_ref[...], SWIGLU_LIMIT)
        u = jnp.clip(jnp.dot(x, wu_bf[...], preferred_element_type=F32) + bu_ref[...],
                     -SWIGLU_LIMIT, SWIGLU_LIMIT)
        a = g * jax.nn.sigmoid(SWIGLU_ALPHA * g) * (u + 1.0)
        y_ref[...] = jnp.dot(a.astype(BF16), wd_bf[...], preferred_element_type=F32) + bd_ref[...]


def _experts(xs, tile_e, tile_blk, nact, layer, w_gate, b_gate, w_up, b_up, w_down, b_down):
    P, D = xs.shape
    L, E, _, F = w_gate.shape
    tm = EXPERT_TILE
    rows = pl.BlockSpec((tm, D), lambda j, te, tb, na: (tb[j], 0))
    wspec = lambda r, c: pl.BlockSpec((None, None, r, c), lambda j, te, tb, na: (layer, te[j], 0, 0))
    return pl.pallas_call(
        _experts_kernel,
        out_shape=jax.ShapeDtypeStruct((P, D), F32),
        grid_spec=pltpu.PrefetchScalarGridSpec(
            num_scalar_prefetch=3,
            grid=(P // tm,),
            in_specs=[rows, wspec(D, F), wspec(1, F), wspec(D, F), wspec(1, F), wspec(F, D), wspec(1, D)],
            out_specs=rows,
            scratch_shapes=[pltpu.VMEM((D, F), BF16), pltpu.VMEM((D, F), BF16), pltpu.VMEM((F, D), BF16)],
        ),
        compiler_params=pltpu.CompilerParams(dimension_semantics=("arbitrary",),
                                             vmem_limit_bytes=VMEM_LIMIT_BYTES),
        name="moe_experts",
    )(tile_e, tile_blk, nact, xs, w_gate, b_gate.reshape(L, E, 1, F), w_up, b_up.reshape(L, E, 1, F),
      w_down, b_down.reshape(L, E, 1, D))


def _combine_kernel(*refs, final_norm):
    if final_norm:
        (src_ref, cnt_ref, off_ref, y_ref, loc_ref, pw_ref, x_ref, gf_ref, fg_ref, fsh_ref, fsc_ref,
         o_ref, seg_buf, sems) = refs
    else:
        src_ref, cnt_ref, off_ref, y_ref, loc_ref, pw_ref, x_ref, gf_ref, o_ref, seg_buf, sems = refs
    i = pl.program_id(0)
    last = pl.num_programs(0) - 1
    slot = lax.rem(i, 2)
    t = x_ref.shape[0]
    n_loc = seg_buf.shape[1]

    def segment_copies(tile, buf, wait):
        def per_expert(e, carry):
            idx = tile * N_EXPERTS + e
            off = off_ref[idx]
            src = src_ref[idx]

            def chunk(done, rows):
                cp = pltpu.make_async_copy(
                    y_ref.at[pl.ds(pl.multiple_of(src + done, SEG_ALIGN), rows), :],
                    seg_buf.at[buf, pl.ds(pl.multiple_of(off + done, SEG_ALIGN), rows), :], sems.at[buf])
                cp.wait() if wait else cp.start()

            _for_each_chunk(cnt_ref[idx], t, chunk)
            return carry

        lax.fori_loop(0, N_EXPERTS, per_expert, 0)

    @pl.when(i == 0)
    def _():
        seg_buf[...] = jnp.zeros(seg_buf.shape, F32)
        segment_copies(0, 0, wait=False)

    @pl.when(i < last)
    def _():
        segment_copies(i + 1, 1 - slot, wait=False)

    segment_copies(i, slot, wait=True)
    loc = loc_ref[...].astype(F32)
    pw = pw_ref[...]
    l_id = lax.broadcasted_iota(jnp.int32, (t, n_loc), 1).astype(F32)
    weights = jnp.zeros((t, n_loc), F32)
    for kk in range(TOP_K):
        weights = jnp.where(l_id == loc[:, kk:kk + 1], pw[:, kk:kk + 1], weights)
    f = jnp.dot(weights.astype(BF16), seg_buf[slot].astype(BF16), preferred_element_type=F32)
    x2 = x_ref[...] + gf_ref[...] * f
    if final_norm:
        y = x2 * _rms_scale(x2) * fg_ref[...]
        x2 = y * (1.0 + fsc_ref[...]) + fsh_ref[...]
    o_ref[...] = x2


def _combine(y, loc, pw, src, seg_cnt, seg_off, x, g_f, final):
    B, S, D = x.shape
    t = TOKEN_TILE
    per = S // t
    tok = lambda w: pl.BlockSpec((None, t, w), lambda i, *_: (i // per, i % per, 0))
    vec = pl.BlockSpec((None, 1, D), lambda i, *_: (i // per, 0, 0))
    in_specs = [pl.BlockSpec(memory_space=pl.ANY), tok(LANES), tok(LANES), tok(D), vec]
    args = [y, loc, pw, x, g_f]
    if final is not None:
        fg, fsh, fsc = final
        in_specs += [pl.BlockSpec((1, D), lambda i, *_: (0, 0)), vec, vec]
        args += [fg.reshape(1, D), fsh, fsc]
    return pl.pallas_call(
        functools.partial(_combine_kernel, final_norm=final is not None),
        out_shape=jax.ShapeDtypeStruct((B, S, D), F32),
        grid_spec=pltpu.PrefetchScalarGridSpec(
            num_scalar_prefetch=3,
            grid=(B * per,),
            in_specs=in_specs,
            out_specs=tok(D),
            scratch_shapes=[pltpu.VMEM((2, SEG_BUF_ROWS, D), F32), pltpu.SemaphoreType.DMA((2,))],
        ),
        compiler_params=pltpu.CompilerParams(dimension_semantics=("arbitrary",),
                                             vmem_limit_bytes=VMEM_LIMIT_BYTES),
        name="moe_combine",
    )(src, seg_cnt, seg_off, *args)


def _moe(h, loc, pw, seg, totals, x, g_f, layer, w_gate, b_gate, w_up, b_up, w_down, b_down, final):
    B, S, D = x.shape
    N = B * S
    E = N_EXPERTS
    tm = EXPERT_TILE
    n_tok_tiles = N // TOKEN_TILE
    max_rows = N * TOP_K + n_tok_tiles * E * (SEG_ALIGN - 1)
    n_tiles = -(-max_rows // tm) + E
    tot = totals[0, :E].astype(jnp.int32)
    tiles_per = (tot + tm - 1) // tm
    tile_end = jnp.cumsum(tiles_per)
    row_start = (tile_end - tiles_per) * tm
    nact = tile_end[-1:]
    jj = jnp.minimum(jnp.arange(n_tiles, dtype=jnp.int32), nact[0] - 1)
    tile_e = jnp.sum((jj[:, None] >= tile_end[None, :]).astype(jnp.int32), axis=1)
    place = (row_start[None, :] + seg[:, 0, :E]).reshape(-1)
    seg_cnt = seg[:, 1, :E].reshape(-1)
    seg_off = seg[:, 2, :E].reshape(-1)
    loc2 = loc.reshape(N, LANES)
    xs = _dispatch(h.reshape(N, D), loc2, place, seg_cnt, seg_off, row_start + tot,
                   tiles_per * tm - tot, n_tiles * tm)
    y = _experts(xs, tile_e, jj, nact, layer, w_gate, b_gate, w_up, b_up, w_down, b_down)
    return _combine(y, loc, pw, place, seg_cnt, seg_off, x, g_f, final)


def _moba_proj_kernel(x_ref, sha_ref, sca_ref, ga_ref, shk_ref, sck_ref, gk_ref, wq_ref, wkv_ref,
                      cos_ref, sin_ref, q_ref, k_ref, v_ref, km_ref):
    x = x_ref[...]
    xn = x * _rms_scale(x)
    h = ((xn * ga_ref[...]) * (1.0 + sca_ref[...]) + sha_ref[...]).astype(BF16)
    hkv = ((xn * gk_ref[...]) * (1.0 + sck_ref[...]) + shk_ref[...]).astype(BF16)
    q = jnp.dot(h, wq_ref[...], preferred_element_type=F32)
    kv = jnp.dot(hkv, wkv_ref[...], preferred_element_type=F32)
    cos = cos_ref[...]
    sin = sin_ref[...]
    scale = LOG2_E * MOBA_HEAD_DIM ** -0.5
    hd = MOBA_HEAD_DIM
    t = x.shape[0]
    width = MOBA_HEADS * hd
    ones = _ones_column(t)
    seq_row = pl.program_id(1) * t + lax.broadcasted_iota(jnp.int32, (t, LANES), 0)
    lane = lax.broadcasted_iota(jnp.int32, (t, LANES), 1)
    block_id = jnp.where(seq_row // MOBA_BLOCK == lane, 1.0, 0.0).astype(BF16)
    for hh in range(MOBA_HEADS):
        hs = slice(hh * hd, (hh + 1) * hd)
        lo = slice(hh * ATTN_WIDTH, hh * ATTN_WIDTH + LANES)
        hi = slice(hh * ATTN_WIDTH + LANES, (hh + 1) * ATTN_WIDTH)
        qh = q[:, hs]
        kh = kv[:, hs]
        q_ref[:, hs] = ((qh * cos + pltpu.roll(qh, hd // 2, 1) * sin) * scale).astype(BF16)
        kr = kh * cos + pltpu.roll(kh, hd // 2, 1) * sin
        k_ref[:, lo] = kr.astype(BF16)
        k_ref[:, hi] = block_id
        km_ref[:, hs] = jnp.mean(kr.reshape(t // MOBA_BLOCK, MOBA_BLOCK, hd), axis=1)
        v_ref[:, lo] = kv[:, width + hh * hd:width + (hh + 1) * hd].astype(BF16)
        v_ref[:, hi] = ones


def _moba_proj(x, sh_a, sc_a, g_a, sh_k, sc_k, g_k, w_q, w_kv, cos, sin):
    B, S, D = x.shape
    t = TOKEN_TILE
    H = MOBA_HEADS
    width = H * MOBA_HEAD_DIM
    per = t // MOBA_BLOCK
    assert S // MOBA_BLOCK <= LANES
    tok = lambda w: pl.BlockSpec((None, t, w), lambda b, i: (b, i, 0))
    vec = lambda w: pl.BlockSpec((None, 1, w), lambda b, i: (b, 0, 0))
    full = lambda a: pl.BlockSpec(a.shape, lambda b, i: (0,) * a.ndim)
    ga2 = g_a.reshape(1, D)
    gk2 = g_k.reshape(1, D)
    wq = w_q.astype(BF16)
    wkv = w_kv.astype(BF16)
    wide = jax.ShapeDtypeStruct((B, S, H * ATTN_WIDTH), BF16)
    q, k, v, km = pl.pallas_call(
        _moba_proj_kernel,
        out_shape=(jax.ShapeDtypeStruct((B, S, width), BF16), wide, wide,
                   jax.ShapeDtypeStruct((B, S // t, per, width), F32)),
        grid=(B, S // t),
        in_specs=[tok(D), vec(D), vec(D), full(ga2), vec(D), vec(D), full(gk2), full(wq), full(wkv),
                  tok(LANES), tok(LANES)],
        out_specs=(tok(width), tok(H * ATTN_WIDTH), tok(H * ATTN_WIDTH),
                   pl.BlockSpec((None, None, per, width), lambda b, i: (b, i, 0, 0))),
        name="moba_proj",
    )(x, sh_a, sc_a, ga2, sh_k, sc_k, gk2, wq, wkv, cos, sin)
    return q, k, v, km.reshape(B, S // MOBA_BLOCK, width)


def _split_mod(mod, n):
    return [mod[:, None, j * D_MODEL:(j + 1) * D_MODEL] for j in range(n)]


def kernel(x, c, positions, ada_w, ada_b, norm_attn_g, norm_ffn_g, mla_w_in, mla_q_norm_g, mla_w_uq, mla_kv_norm_g, mla_w_ukv, mla_w_o, kv_ada_w, kv_ada_b, kv_norm_g, moba_w_kv, moba_w_q, moba_w_o, router_w, router_b, w_gate, b_gate, w_up, b_up, w_down, b_down, final_ada_w, final_ada_b, final_norm_g):
    mods = _ada_linear(c, ada_w, ada_b)
    kv_mod = _ada_linear(c, kv_ada_w[None], kv_ada_b[None])[0]
    f_mod = _ada_linear(c, final_ada_w[None], final_ada_b[None])[0]
    cos_a, sin_a, cos_b, sin_b = _rope_tables(positions)
    f_sh, f_sc = _split_mod(f_mod, 2)
    shared = None
    for layer in range(DEPTH):
        sh_a, sc_a, g_a, sh_f, sc_f, g_f = _split_mod(mods[layer], 6)
        if layer < N_A:
            q, k, v = _mla_proj(x, sh_a, sc_a, norm_attn_g[layer], mla_w_in[layer], mla_q_norm_g[layer],
                                mla_w_uq[layer], mla_kv_norm_g[layer], mla_w_ukv[layer], cos_a, sin_a)
            attn = _attention(q, k, v, None, heads=MLA_HEADS, dq=ATTN_WIDTH, dv=MLA_V, moba=False)
            w_o = mla_w_o[layer]
        else:
            j = layer - N_A
            kv_sh, kv_sc = _split_mod(kv_mod, 2)
            q, k, v, km = _moba_proj(x, sh_a, sc_a, norm_attn_g[layer], kv_sh, kv_sc, kv_norm_g,
                                     moba_w_q[j], moba_w_kv, cos_b, sin_b)
            if shared is None:
                shared = (k, v, km)
            attn = _attention(q, shared[0], shared[1], shared[2], heads=MOBA_HEADS, dq=MOBA_HEAD_DIM,
                              dv=MOBA_HEAD_DIM, moba=True)
            w_o = moba_w_o[j]
        x, h, loc, pw, seg, totals = _attn_out(attn, w_o, x, g_a, norm_ffn_g[layer], sh_f, sc_f,
                                               router_w[layer], router_b[layer])
        final = (final_norm_g, f_sh, f_sc) if layer == DEPTH - 1 else None
        x = _moe(h, loc, pw, seg, totals, x, g_f, layer, w_gate, b_gate, w_up, b_up, w_down, b_down,
                 final)
    return x
```

```python
import functools

import jax
import jax.numpy as jnp
from jax import lax
from jax.experimental import pallas as pl
from jax.experimental.pallas import tpu as pltpu

D_MODEL = 1024
DEPTH = 2
N_A = DEPTH // 2

MLA_HEADS = 8
MLA_Q_LORA = 256
MLA_KV_LORA = 128
MLA_NOPE = 128
MLA_ROPE = 64
MLA_V = 128

MOBA_HEADS = 8
MOBA_HEAD_DIM = D_MODEL // MOBA_HEADS
MOBA_BLOCK = 256
MOBA_TOPK = 3

N_EXPERTS = 32
TOP_K = 4
SWIGLU_LIMIT = 7.0
SWIGLU_ALPHA = 1.702

ROPE_THETA = 10000.0
NORM_EPS = 1e-6
NEG_INF = -1e30
LOG2_E = 1.4426950408889634
SOFTMAX_M_INIT = -1e20

LANES = 128
TOKEN_TILE = 512
ATTN_Q_TILE = 1024
ATTN_KV_TILE = 512
ATTN_WIDTH = 2 * LANES
ATTN_HEADS_PER_STEP = 2
EXPERT_TILE = 512
SEG_ALIGN = 8
SEG_CHUNK = 64
ROUTE_TILE = 512
SEG_BUF_ROWS = ROUTE_TILE * TOP_K + N_EXPERTS * SEG_ALIGN
VMEM_LIMIT_BYTES = 56 * 1024 * 1024

F32 = jnp.float32
BF16 = jnp.bfloat16


def _rms_scale(x):
    return lax.rsqrt(jnp.mean(x * x, axis=-1, keepdims=True) + NORM_EPS)


def _nt_dot(a, b):
    return lax.dot_general(a, b, (((1,), (1,)), ((), ())), preferred_element_type=F32)


def _ones_column(rows):
    lane = lax.broadcasted_iota(jnp.int32, (rows, LANES), 1)
    return jnp.where(lane == 0, 1.0, 0.0).astype(BF16)


def _ada_kernel(c_ref, w_ref, b_ref, o_ref):
    c = c_ref[...]
    ca = c * jax.nn.sigmoid(c)
    o_ref[...] = jnp.dot(ca, w_ref[...], precision=lax.Precision.HIGHEST,
                         preferred_element_type=F32) + b_ref[...]


def _ada_linear(c, w, b):
    L, D, M = w.shape
    B = c.shape[0]
    bn = 1024
    return pl.pallas_call(
        _ada_kernel,
        out_shape=jax.ShapeDtypeStruct((L, B, M), F32),
        grid=(L, M // bn),
        in_specs=[
            pl.BlockSpec((B, D), lambda l, j: (0, 0)),
            pl.BlockSpec((None, D, bn), lambda l, j: (l, 0, j)),
            pl.BlockSpec((None, 1, bn), lambda l, j: (l, 0, j)),
        ],
        out_specs=pl.BlockSpec((None, B, bn), lambda l, j: (l, 0, j)),
        name="ada_linear",
    )(c, w, b.reshape(L, 1, M))


def _rope_table_kernel(pos_ref, inv_ref, sign_ref, ca_ref, sa_ref, cb_ref, sb_ref):
    ang = pos_ref[...] * inv_ref[...]
    lane = lax.broadcasted_iota(jnp.int32, ang.shape, 1)
    na, nb = MLA_ROPE // 2, MOBA_HEAD_DIM // 2

    def spread(v):
        a = jnp.where(lane < na, v, 0.0)
        a = a + pltpu.roll(a, na, 1)
        a = a + pltpu.roll(a, 2 * na, 1)
        b = jnp.where((lane >= na) & (lane < na + nb), v, 0.0)
        b = pltpu.roll(b, LANES - na, 1)
        b = b + pltpu.roll(b, nb, 1)
        return a, b

    ca_ref[...], cb_ref[...] = spread(jnp.cos(ang))
    sa, sb = spread(jnp.sin(ang))
    sa_ref[...] = sa
    sb_ref[...] = sb * sign_ref[...]


def _rope_tables(positions):
    B, S = positions.shape
    t = 1024
    assert MLA_ROPE // 2 + MOBA_HEAD_DIM // 2 <= LANES and MLA_ROPE * 2 == LANES == MOBA_HEAD_DIM

    def inv_freq(d):
        half = d // 2
        return ROPE_THETA ** (-jnp.arange(half, dtype=F32) * (2.0 / d))

    inv = jnp.concatenate([inv_freq(MLA_ROPE), inv_freq(MOBA_HEAD_DIM),
                           jnp.zeros((LANES - MLA_ROPE // 2 - MOBA_HEAD_DIM // 2,), F32)]).reshape(1, LANES)
    half = MOBA_HEAD_DIM // 2
    sign = jnp.concatenate([-jnp.ones((half,), F32), jnp.ones((half,), F32)]).reshape(1, LANES)
    pos = positions.astype(F32).reshape(B, S, 1)
    row = pl.BlockSpec((1, LANES), lambda b, i: (0, 0))
    out = pl.BlockSpec((None, t, LANES), lambda b, i: (b, i, 0))
    shp = jax.ShapeDtypeStruct((B, S, LANES), F32)
    return pl.pallas_call(
        _rope_table_kernel,
        out_shape=(shp, shp, shp, shp),
        grid=(B, S // t),
        in_specs=[pl.BlockSpec((None, t, 1), lambda b, i: (b, i, 0)), row, row],
        out_specs=(out, out, out, out),
        name="rope_tables",
    )(pos, inv, sign)


def _mla_proj_kernel(x_ref, sh_ref, sc_ref, g_ref, win_ref, qg_ref, wqn_ref, wqr_ref, wqrr_ref,
                     kvg_ref, wkn_ref, wv_ref, cos_ref, sin_ref, q_ref, k_ref, v_ref):
    x = x_ref[...]
    y = x * _rms_scale(x) * g_ref[...]
    h = (y * (1.0 + sc_ref[...]) + sh_ref[...]).astype(BF16)
    proj = jnp.dot(h, win_ref[...], preferred_element_type=F32)
    c_q = proj[:, :MLA_Q_LORA]
    c_kv = proj[:, MLA_Q_LORA:MLA_Q_LORA + MLA_KV_LORA]
    kr_a = proj[:, 384:512]
    kr_b = proj[:, 512:640]
    cq = (c_q * _rms_scale(c_q) * qg_ref[...]).astype(BF16)
    ckv = (c_kv * _rms_scale(c_kv) * kvg_ref[...]).astype(BF16)
    cos = cos_ref[...]
    sin = sin_ref[...]
    scale = LOG2_E * (MLA_NOPE + MLA_ROPE) ** -0.5
    kr = (kr_a * cos + kr_b * sin).astype(BF16)
    q_nope = jnp.dot(cq, wqn_ref[...], preferred_element_type=F32) * scale
    q_ra = jnp.dot(cq, wqr_ref[...], preferred_element_type=F32)
    q_rb = jnp.dot(cq, wqrr_ref[...], preferred_element_type=F32)
    k_nope = jnp.dot(ckv, wkn_ref[...], preferred_element_type=F32)
    v = jnp.dot(ckv, wv_ref[...], preferred_element_type=F32)
    ones = _ones_column(x.shape[0])
    for hh in range(MLA_HEADS):
        hs = slice(hh * LANES, (hh + 1) * LANES)
        lo = slice(hh * ATTN_WIDTH, hh * ATTN_WIDTH + LANES)
        hi = slice(hh * ATTN_WIDTH + LANES, (hh + 1) * ATTN_WIDTH)
        q_ref[:, lo] = q_nope[:, hs].astype(BF16)
        q_ref[:, hi] = ((q_ra[:, hs] * cos + q_rb[:, hs] * sin) * scale).astype(BF16)
        k_ref[:, lo] = k_nope[:, hs].astype(BF16)
        k_ref[:, hi] = kr
        v_ref[:, lo] = v[:, hs].astype(BF16)
        v_ref[:, hi] = ones


def _rot_half_cols(w, half):
    return jnp.concatenate([-w[..., half:], w[..., :half]], axis=-1)


def _mla_proj(x, sh, sc, g, w_in, q_norm_g, w_uq, kv_norm_g, w_ukv, cos, sin):
    B, S, D = x.shape
    H = MLA_HEADS
    t = TOKEN_TILE
    half = MLA_ROPE // 2
    w_kr = w_in[:, MLA_Q_LORA + MLA_KV_LORA:]
    zpad = jnp.zeros((D, LANES - MLA_ROPE), F32)
    w_in_ext = jnp.concatenate(
        [w_in[:, :MLA_Q_LORA + MLA_KV_LORA], w_kr, zpad, _rot_half_cols(w_kr, half), zpad],
        axis=1).astype(BF16)
    wq = w_uq.reshape(MLA_Q_LORA, H, MLA_NOPE + MLA_ROPE)
    wq_nope = wq[..., :MLA_NOPE].reshape(MLA_Q_LORA, H * MLA_NOPE).astype(BF16)
    wq_r = wq[..., MLA_NOPE:]
    pad = ((0, 0), (0, 0), (0, LANES - MLA_ROPE))
    wq_rope = jnp.pad(wq_r, pad).reshape(MLA_Q_LORA, H * LANES).astype(BF16)
    wq_rope_rot = jnp.pad(_rot_half_cols(wq_r, half), pad).reshape(MLA_Q_LORA, H * LANES).astype(BF16)
    wkv = w_ukv.reshape(MLA_KV_LORA, H, MLA_NOPE + MLA_V)
    wk_nope = wkv[..., :MLA_NOPE].reshape(MLA_KV_LORA, H * MLA_NOPE).astype(BF16)
    wv = wkv[..., MLA_NOPE:].reshape(MLA_KV_LORA, H * MLA_V).astype(BF16)

    tok = lambda w: pl.BlockSpec((None, t, w), lambda b, i: (b, i, 0))
    vec = lambda w: pl.BlockSpec((None, 1, w), lambda b, i: (b, 0, 0))
    full = lambda a: pl.BlockSpec(a.shape, lambda b, i: (0,) * a.ndim)
    g2 = g.reshape(1, D)
    qg2 = q_norm_g.reshape(1, MLA_Q_LORA)
    kvg2 = kv_norm_g.reshape(1, MLA_KV_LORA)
    wide = jax.ShapeDtypeStruct((B, S, H * ATTN_WIDTH), BF16)
    return pl.pallas_call(
        _mla_proj_kernel,
        out_shape=(wide, wide, wide),
        grid=(B, S // t),
        in_specs=[tok(D), vec(D), vec(D), full(g2), full(w_in_ext), full(qg2), full(wq_nope),
                  full(wq_rope), full(wq_rope_rot), full(kvg2), full(wk_nope), full(wv),
                  tok(LANES), tok(LANES)],
        out_specs=(tok(H * ATTN_WIDTH), tok(H * ATTN_WIDTH), tok(H * ATTN_WIDTH)),
        name="mla_proj",
    )(x, sh, sc, g2, w_in_ext, qg2, wq_nope, wq_rope, wq_rope_rot, kvg2, wk_nope, wv, cos, sin)


def _attn_kernel(*refs, moba, tq, bk, dq, dv, nh):
    n_in = 4 if moba else 3
    q_ref, k_ref, v_ref = refs[:3]
    o_ref = refs[n_in]
    scratch = refs[n_in + 1:]
    per_head = len(scratch) // nh
    w = ATTN_WIDTH
    i = pl.program_id(2)
    row_id = lax.broadcasted_iota(jnp.int32, (tq, 1), 0)
    col_id = lax.broadcasted_iota(jnp.int32, (1, bk), 1)

    def head_scratch(hh):
        sc = scratch[hh * per_head:(hh + 1) * per_head]
        return sc if moba else (None,) + tuple(sc)

    def prepare(hh):
        qx_ref, s_a, s_b, p_a, p_b, al_a, al_b, m_sc, acc_sc = head_scratch(hh)
        if moba:
            km_ref = refs[3]
            q = q_ref[:, hh * dq:(hh + 1) * dq]
            nb = km_ref.shape[0]
            gate = _nt_dot(km_ref[:, hh * dq:(hh + 1) * dq].astype(BF16), q)
            blk = lax.broadcasted_iota(jnp.int32, (nb, tq), 0)
            blk_f = blk.astype(F32)
            own = (i * tq + lax.broadcasted_iota(jnp.int32, (1, tq), 1)) // MOBA_BLOCK
            gate = jnp.where(blk < own, gate, -jnp.inf)
            sel = jnp.where(blk == own, 1.0, 0.0)
            for _ in range(MOBA_TOPK):
                mx = jnp.max(gate, axis=0, keepdims=True)
                first = jnp.min(jnp.where(gate == mx, blk_f, float(nb)), axis=0, keepdims=True)
                hit = blk_f == first
                sel = jnp.where(hit & (mx > -jnp.inf), 1.0, sel)
                gate = jnp.where(hit, -jnp.inf, gate)
            bias = jnp.where(sel > 0.0, 0.0, NEG_INF)
            bias = jnp.concatenate([bias, jnp.zeros((LANES - nb, tq), F32)], axis=0)
            qx_ref[:, :LANES] = q
            qx_ref[:, LANES:] = bias.T.astype(BF16)
        m_sc[...] = jnp.full(m_sc.shape, SOFTMAX_M_INIT, F32)
        acc_sc[...] = jnp.zeros(acc_sc.shape, F32)
        p_b[...] = jnp.zeros(p_b.shape, BF16)
        al_b[...] = jnp.ones(al_b.shape, F32)

    def scores(hh, j, slot, rows=slice(None)):
        sc = head_scratch(hh)
        qx = sc[0][rows, :] if moba else q_ref[rows, hh * w:(hh + 1) * w]
        start = pl.multiple_of(j * bk, bk)
        sc[1 + slot][rows, :] = _nt_dot(qx, k_ref[pl.ds(start, bk), hh * w:(hh + 1) * w])

    def soft(hh, slot, diag_offset=None, rows=slice(None)):
        sc = head_scratch(hh)
        s_src, p_dst, al_dst, m_sc = sc[1 + slot], sc[3 + slot], sc[5 + slot], sc[7]
        s = s_src[rows, :]
        if diag_offset is not None:
            s = jnp.where(col_id + diag_offset <= row_id[rows, :], s, -jnp.inf)
        m_prev = m_sc[rows, :]
        m_new = jnp.maximum(m_prev, jnp.max(s, axis=1, keepdims=True))
        p_dst[rows, :] = jnp.exp2(s - m_new).astype(BF16)
        al_dst[rows, :] = jnp.exp2(m_prev - m_new)
        m_sc[rows, :] = m_new

    def fold(hh, j, slot, rows=slice(None)):
        sc = head_scratch(hh)
        p_src, al_src, acc_sc = sc[3 + slot], sc[5 + slot], sc[8]
        start = pl.multiple_of(j * bk, bk)
        acc_sc[rows, :] = al_src[rows, :] * acc_sc[rows, :] + jnp.dot(
            p_src[rows, :], v_ref[pl.ds(start, bk), hh * w:(hh + 1) * w], preferred_element_type=F32)

    hs = range(nh)
    for hh in hs:
        prepare(hh)
    for hh in hs:
        scores(hh, 0, 0)

    def body(t, carry):
        j = 2 * t
        for hh in hs:
            scores(hh, j + 1, 1)
        for hh in hs:
            soft(hh, 0)
        for hh in hs:
            fold(hh, jnp.maximum(j - 1, 0), 1)
        for hh in hs:
            scores(hh, j + 2, 0)
        for hh in hs:
            soft(hh, 1)
        for hh in hs:
            fold(hh, j, 0)
        return carry

    lax.fori_loop(0, i, body, 0)
    j = 2 * i
    late = slice(bk, tq)
    for hh in hs:
        scores(hh, j + 1, 1, late)
    for hh in hs:
        soft(hh, 0, diag_offset=0)
    for hh in hs:
        fold(hh, jnp.maximum(j - 1, 0), 1)
    for hh in hs:
        soft(hh, 1, diag_offset=bk, rows=late)
    for hh in hs:
        fold(hh, j, 0)
    for hh in hs:
        fold(hh, j + 1, 1, late)
    for hh in hs:
        acc = head_scratch(hh)[8][...]
        o_ref[:, hh * dv:(hh + 1) * dv] = (acc[:, :dv] / acc[:, dv:dv + 1]).astype(o_ref.dtype)


def _attention(q, k, v, k_mean, *, heads, dq, dv, moba):
    B, S, _ = q.shape
    tq, bk, w, nh = ATTN_Q_TILE, ATTN_KV_TILE, ATTN_WIDTH, ATTN_HEADS_PER_STEP
    assert tq == 2 * bk and tq % MOBA_BLOCK == 0 and S % tq == 0 and heads % nh == 0
    in_specs = [
        pl.BlockSpec((None, tq, nh * dq), lambda b, h, i: (b, i, h)),
        pl.BlockSpec((None, S, nh * w), lambda b, h, i: (b, 0, h)),
        pl.BlockSpec((None, S, nh * w), lambda b, h, i: (b, 0, h)),
    ]
    args = [q, k, v]
    scratch = []
    if moba:
        nb = k_mean.shape[1]
        in_specs.append(pl.BlockSpec((None, nb, nh * dq), lambda b, h, i: (b, 0, h)))
        args.append(k_mean)
    for _ in range(nh):
        if moba:
            scratch.append(pltpu.VMEM((tq, w), BF16))
        scratch += [pltpu.VMEM((tq, bk), F32), pltpu.VMEM((tq, bk), F32),
                    pltpu.VMEM((tq, bk), BF16), pltpu.VMEM((tq, bk), BF16),
                    pltpu.VMEM((tq, 1), F32), pltpu.VMEM((tq, 1), F32),
                    pltpu.VMEM((tq, 1), F32), pltpu.VMEM((tq, w), F32)]
    return pl.pallas_call(
        functools.partial(_attn_kernel, moba=moba, tq=tq, bk=bk, dq=dq, dv=dv, nh=nh),
        out_shape=jax.ShapeDtypeStruct((B, S, heads * dv), BF16),
        grid=(B, heads // nh, S // tq),
        in_specs=in_specs,
        out_specs=pl.BlockSpec((None, tq, nh * dv), lambda b, h, i: (b, i, h)),
        scratch_shapes=scratch,
        compiler_params=pltpu.CompilerParams(vmem_limit_bytes=VMEM_LIMIT_BYTES),
        name="moba_attention" if moba else "mla_attention",
    )(*args)


def _attn_out_kernel(a_ref, wo_ref, x_ref, ga_ref, g_ref, sh_ref, sc_ref, rw_ref, rb_ref,
                     x1_ref, h_ref, loc_ref, pw_ref, seg_ref, tot_ref, run_ref):
    first_step = (pl.program_id(0) == 0) & (pl.program_id(1) == 0)

    @pl.when(first_step)
    def _():
        run_ref[...] = jnp.zeros(run_ref.shape, F32)

    a = jnp.dot(a_ref[...], wo_ref[...], preferred_element_type=F32)
    x1 = x_ref[...] + ga_ref[...] * a
    x1_ref[...] = x1
    y = x1 * _rms_scale(x1) * g_ref[...]
    h = (y * (1.0 + sc_ref[...]) + sh_ref[...]).astype(BF16)
    h_ref[...] = h
    logits = jnp.dot(h, rw_ref[...], preferred_element_type=F32) + rb_ref[...]
    t = logits.shape[0]
    lane_i = lax.broadcasted_iota(jnp.int32, logits.shape, 1)
    lane = lane_i.astype(F32)
    logits = jnp.where(lane_i < N_EXPERTS, logits, -jnp.inf)
    top = None
    den = jnp.zeros((t, 1), F32)
    sel = jnp.zeros(logits.shape, F32)
    hits, ws = [], []
    for _ in range(TOP_K):
        mx = jnp.max(logits, axis=1, keepdims=True)
        first = jnp.min(jnp.where(logits == mx, lane, float(LANES)), axis=1, keepdims=True)
        hit = lane == first
        if top is None:
            top = mx
        w = jnp.exp(mx - top)
        den = den + w
        sel = jnp.where(hit, 1.0, sel)
        logits = jnp.where(hit, -jnp.inf, logits)
        hits.append(hit)
        ws.append(w)
    rt = ROUTE_TILE
    r_id = lax.broadcasted_iota(jnp.int32, (rt, rt), 0)
    c_id = lax.broadcasted_iota(jnp.int32, (rt, rt), 1)
    before = jnp.where(c_id < r_id, 1.0, 0.0).astype(BF16)
    a_id = lax.broadcasted_iota(jnp.int32, (LANES, LANES), 0)
    b_id = lax.broadcasted_iota(jnp.int32, (LANES, LANES), 1)
    earlier = jnp.where(a_id < b_id, 1.0, 0.0).astype(BF16)
    row8 = lax.broadcasted_iota(jnp.int32, (8, LANES), 0)
    local = []
    for sub in range(t // rt):
        sel_s = sel[sub * rt:(sub + 1) * rt]
        prefix = jnp.dot(before, sel_s.astype(BF16), preferred_element_type=F32)
        cnt = jnp.sum(sel_s, axis=0, keepdims=True)
        units = jnp.floor((cnt + (SEG_ALIGN - 1.0)) * (1.0 / SEG_ALIGN))
        cnt_pad = units * SEG_ALIGN
        seg_off = jnp.dot(jnp.broadcast_to(units, (8, LANES)).astype(BF16), earlier,
                          preferred_element_type=F32)[0:1] * SEG_ALIGN
        base = run_ref[...]
        run_ref[...] = base + cnt_pad
        seg = jnp.where(row8 == 0, base, jnp.where(row8 == 1, cnt_pad, jnp.where(row8 == 2, seg_off, 0.0)))
        seg_ref[sub] = seg.astype(jnp.int32)
        local.append(prefix + seg_off)
    tot_ref[...] = run_ref[...]
    local = jnp.concatenate(local, axis=0)
    loc = jnp.zeros(logits.shape, F32)
    pw = jnp.zeros(logits.shape, F32)
    for kk in range(TOP_K):
        row = jnp.sum(jnp.where(hits[kk], local, 0.0), axis=1, keepdims=True)
        loc = jnp.where(lane_i == kk, row, loc)
        pw = jnp.where(lane_i == kk, ws[kk] / den, pw)
    loc_ref[...] = loc.astype(jnp.int32)
    pw_ref[...] = pw


def _attn_out(attn, w_o, x, g_a, norm_g, sh_f, sc_f, router_w, router_b):
    B, S, D = x.shape
    t = TOKEN_TILE
    per = S // t
    sub = t // ROUTE_TILE
    rw = jnp.pad(router_w, ((0, 0), (0, LANES - N_EXPERTS))).astype(BF16)
    rb = jnp.pad(router_b, (0, LANES - N_EXPERTS)).reshape(1, LANES)
    wo = w_o.astype(BF16)
    g2 = norm_g.reshape(1, D)
    tok = lambda w: pl.BlockSpec((None, t, w), lambda b, i: (b, i, 0))
    vec = lambda w: pl.BlockSpec((None, 1, w), lambda b, i: (b, 0, 0))
    full = lambda a: pl.BlockSpec(a.shape, lambda b, i: (0,) * a.ndim)
    return pl.pallas_call(
        _attn_out_kernel,
        out_shape=(jax.ShapeDtypeStruct((B, S, D), F32),
                   jax.ShapeDtypeStruct((B, S, D), BF16),
                   jax.ShapeDtypeStruct((B, S, LANES), jnp.int32),
                   jax.ShapeDtypeStruct((B, S, LANES), F32),
                   jax.ShapeDtypeStruct((B * per * sub, 8, LANES), jnp.int32),
                   jax.ShapeDtypeStruct((1, LANES), F32)),
        grid=(B, per),
        in_specs=[tok(attn.shape[-1]), full(wo), tok(D), vec(D), full(g2), vec(D), vec(D),
                  full(rw), full(rb)],
        out_specs=(tok(D), tok(D), tok(LANES), tok(LANES),
                   pl.BlockSpec((sub, 8, LANES), lambda b, i: (b * per + i, 0, 0)),
                   pl.BlockSpec((1, LANES), lambda b, i: (0, 0))),
        scratch_shapes=[pltpu.VMEM((1, LANES), F32)],
        compiler_params=pltpu.CompilerParams(dimension_semantics=("arbitrary", "arbitrary")),
        name="attn_out_router",
    )(attn, wo, x, g_a, g2, sh_f, sc_f, rw, rb)


def _for_each_chunk(cnt, fn):
    shift = SEG_CHUNK.bit_length() - 1
    whole = cnt >> shift

    def piece(c, carry):
        fn(pl.multiple_of(c * SEG_CHUNK, SEG_CHUNK), SEG_CHUNK)
        return carry

    lax.fori_loop(0, whole, piece, 0)
    rows = SEG_CHUNK // 2
    while rows >= SEG_ALIGN:
        shift = rows.bit_length()
        done = (cnt >> shift) << shift

        @pl.when((cnt & rows) != 0)
        def _(done=done, rows=rows):
            fn(pl.multiple_of(done, SEG_ALIGN), rows)

        rows //= 2


def _dispatch_kernel(dst_ref, cnt_ref, off_ref, pad_start_ref, pad_cnt_ref, h_ref, loc_ref,
                     xs_ref, seg_buf, zero_buf, sems):
    i = pl.program_id(0)
    last = pl.num_programs(0) - 1
    slot = lax.rem(i, 2)
    t = h_ref.shape[0]
    n_loc = seg_buf.shape[1]
    pad_sem = sems.at[2]

    @pl.when(i == 0)
    def _():
        zero_buf[...] = jnp.zeros(zero_buf.shape, F32)
        for wait in (False, True):
            def per_expert(e, carry, wait=wait):
                start = pad_start_ref[e]

                def chunk(done, rows):
                    cp = pltpu.make_async_copy(
                        zero_buf.at[pl.ds(0, rows), :],
                        xs_ref.at[pl.ds(pl.multiple_of(start + done, SEG_ALIGN), rows), :], pad_sem)
                    cp.wait() if wait else cp.start()

                _for_each_chunk(pad_cnt_ref[e], chunk)
                return carry

            lax.fori_loop(0, N_EXPERTS, per_expert, 0)

    def segment_copies(tile, buf, wait):
        def per_expert(e, carry):
            idx = tile * N_EXPERTS + e
            off = off_ref[idx]
            dst = dst_ref[idx]

            def chunk(done, rows):
                cp = pltpu.make_async_copy(
                    seg_buf.at[buf, pl.ds(pl.multiple_of(off + done, SEG_ALIGN), rows), :],
                    xs_ref.at[pl.ds(pl.multiple_of(dst + done, SEG_ALIGN), rows), :], sems.at[buf])
                cp.wait() if wait else cp.start()

            _for_each_chunk(cnt_ref[idx], chunk)
            return carry

        lax.fori_loop(0, N_EXPERTS, per_expert, 0)

    loc_t = loc_ref[...].astype(F32).T
    l_id = lax.broadcasted_iota(jnp.int32, (n_loc, t), 0).astype(F32)
    onehot = jnp.zeros((n_loc, t), F32)
    for kk in range(TOP_K):
        onehot = jnp.where(l_id == loc_t[kk:kk + 1, :], 1.0, onehot)
    seg_buf[slot] = jnp.dot(onehot.astype(BF16), h_ref[...], preferred_element_type=F32)
    segment_copies(i, slot, wait=False)

    @pl.when(i > 0)
    def _():
        segment_copies(i - 1, 1 - slot, wait=True)

    @pl.when(i == last)
    def _():
        segment_copies(i, slot, wait=True)


def _dispatch(h, loc, dst, seg_cnt, seg_off, pad_start, pad_cnt, n_rows):
    N, D = h.shape
    t = ROUTE_TILE
    return pl.pallas_call(
        _dispatch_kernel,
        out_shape=jax.ShapeDtypeStruct((n_rows, D), F32),
        grid_spec=pltpu.PrefetchScalarGridSpec(
            num_scalar_prefetch=5,
            grid=(N // t,),
            in_specs=[pl.BlockSpec((t, D), lambda i, *_: (i, 0)),
                      pl.BlockSpec((t, LANES), lambda i, *_: (i, 0))],
            out_specs=pl.BlockSpec(memory_space=pl.ANY),
            scratch_shapes=[pltpu.VMEM((2, SEG_BUF_ROWS, D), F32),
                            pltpu.VMEM((SEG_CHUNK, D), F32),
                            pltpu.SemaphoreType.DMA((3,))],
        ),
        compiler_params=pltpu.CompilerParams(dimension_semantics=("arbitrary",),
                                             vmem_limit_bytes=VMEM_LIMIT_BYTES),
        name="moe_dispatch",
    )(dst, seg_cnt, seg_off, pad_start, pad_cnt, h, loc)


def _experts_kernel(tile_e_ref, tile_blk_ref, nact_ref, xs_ref, wg_ref, bg_ref, wu_ref, bu_ref,
                    wd_ref, bd_ref, y_ref, wg_bf, wu_bf, wd_bf):
    j = pl.program_id(0)

    @pl.when(j < nact_ref[0])
    def _():
        prev = tile_e_ref[jnp.maximum(j - 1, 0)]

        @pl.when((j == 0) | (tile_e_ref[j] != prev))
        def _():
            wg_bf[...] = wg_ref[...].astype(BF16)
            wu_bf[...] = wu_ref[...].astype(BF16)
            wd_bf[...] = wd_ref[...].astype(BF16)

        x = xs_ref[...].astype(BF16)
        g = jnp.minimum(jnp.dot(x, wg_bf[...], preferred_element_type=F32) + bg_ref[...], SWIGLU_LIMIT)
        u = jnp.clip(jnp.dot(x, wu_bf[...], preferred_element_type=F32) + bu_ref[...],
                     -SWIGLU_LIMIT, SWIGLU_LIMIT)
        a = g * jax.nn.sigmoid(SWIGLU_ALPHA * g) * (u + 1.0)
        y_ref[...] = jnp.dot(a.astype(BF16), wd_bf[...], preferred_element_type=F32) + bd_ref[...]


def _experts(xs, tile_e, tile_blk, nact, layer, w_gate, b_gate, w_up, b_up, w_down, b_down):
    P, D = xs.shape
    L, E, _, F = w_gate.shape
    tm = EXPERT_TILE
    rows = pl.BlockSpec((tm, D), lambda j, te, tb, na: (tb[j], 0))
    wspec = lambda r, c: pl.BlockSpec((None, None, r, c), lambda j, te, tb, na: (layer, te[j], 0, 0))
    return pl.pallas_call(
        _experts_kernel,
        out_shape=jax.ShapeDtypeStruct((P, D), F32),
        grid_spec=pltpu.PrefetchScalarGridSpec(
            num_scalar_prefetch=3,
            grid=(P // tm,),
            in_specs=[rows, wspec(D, F), wspec(1, F), wspec(D, F), wspec(1, F), wspec(F, D), wspec(1, D)],
            out_specs=rows,
            scratch_shapes=[pltpu.VMEM((D, F), BF16), pltpu.VMEM((D, F), BF16), pltpu.VMEM((F, D), BF16)],
        ),
        compiler_params=pltpu.CompilerParams(dimension_semantics=("arbitrary",),
                                             vmem_limit_bytes=VMEM_LIMIT_BYTES),
        name="moe_experts",
    )(tile_e, tile_blk, nact, xs, w_gate, b_gate.reshape(L, E, 1, F), w_up, b_up.reshape(L, E, 1, F),
      w_down, b_down.reshape(L, E, 1, D))


def _combine_kernel(*refs, final_norm):
    if final_norm:
        (src_ref, cnt_ref, off_ref, y_ref, loc_ref, pw_ref, x_ref, gf_ref, fg_ref, fsh_ref, fsc_ref,
         o_ref, seg_buf, sems) = refs
    else:
        src_ref, cnt_ref, off_ref, y_ref, loc_ref, pw_ref, x_ref, gf_ref, o_ref, seg_buf, sems = refs
    i = pl.program_id(0)
    last = pl.num_programs(0) - 1
    slot = lax.rem(i, 2)
    t = x_ref.shape[0]
    n_loc = seg_buf.shape[1]

    def segment_copies(tile, buf, wait):
        def per_expert(e, carry):
            idx = tile * N_EXPERTS + e
            off = off_ref[idx]
            src = src_ref[idx]

            def chunk(done, rows):
                cp = pltpu.make_async_copy(
                    y_ref.at[pl.ds(pl.multiple_of(src + done, SEG_ALIGN), rows), :],
                    seg_buf.at[buf, pl.ds(pl.multiple_of(off + done, SEG_ALIGN), rows), :], sems.at[buf])
                cp.wait() if wait else cp.start()

            _for_each_chunk(cnt_ref[idx], chunk)
            return carry

        lax.fori_loop(0, N_EXPERTS, per_expert, 0)

    @pl.when(i == 0)
    def _():
        seg_buf[...] = jnp.zeros(seg_buf.shape, F32)
        segment_copies(0, 0, wait=False)

    @pl.when(i < last)
    def _():
        segment_copies(i + 1, 1 - slot, wait=False)

    segment_copies(i, slot, wait=True)
    loc = loc_ref[...].astype(F32)
    pw = pw_ref[...]
    l_id = lax.broadcasted_iota(jnp.int32, (t, n_loc), 1).astype(F32)
    weights = jnp.zeros((t, n_loc), F32)
    for kk in range(TOP_K):
        weights = jnp.where(l_id == loc[:, kk:kk + 1], pw[:, kk:kk + 1], weights)
    f = jnp.dot(weights.astype(BF16), seg_buf[slot].astype(BF16), preferred_element_type=F32)
    x2 = x_ref[...] + gf_ref[...] * f
    if final_norm:
        y = x2 * _rms_scale(x2) * fg_ref[...]
        x2 = y * (1.0 + fsc_ref[...]) + fsh_ref[...]
    o_ref[...] = x2


def _combine(y, loc, pw, src, seg_cnt, seg_off, x, g_f, final):
    B, S, D = x.shape
    t = ROUTE_TILE
    per = S // t
    tok = lambda w: pl.BlockSpec((None, t, w), lambda i, *_: (i // per, i % per, 0))
    vec = pl.BlockSpec((None, 1, D), lambda i, *_: (i // per, 0, 0))
    in_specs = [pl.BlockSpec(memory_space=pl.ANY), tok(LANES), tok(LANES), tok(D), vec]
    args = [y, loc, pw, x, g_f]
    if final is not None:
        fg, fsh, fsc = final
        in_specs += [pl.BlockSpec((1, D), lambda i, *_: (0, 0)), vec, vec]
        args += [fg.reshape(1, D), fsh, fsc]
    return pl.pallas_call(
        functools.partial(_combine_kernel, final_norm=final is not None),
        out_shape=jax.ShapeDtypeStruct((B, S, D), F32),
        grid_spec=pltpu.PrefetchScalarGridSpec(
            num_scalar_prefetch=3,
            grid=(B * per,),
            in_specs=in_specs,
            out_specs=tok(D),
            scratch_shapes=[pltpu.VMEM((2, SEG_BUF_ROWS, D), F32), pltpu.SemaphoreType.DMA((2,))],
        ),
        compiler_params=pltpu.CompilerParams(dimension_semantics=("arbitrary",),
                                             vmem_limit_bytes=VMEM_LIMIT_BYTES),
        name="moe_combine",
    )(src, seg_cnt, seg_off, *args)


def _moe(h, loc, pw, seg, totals, x, g_f, layer, w_gate, b_gate, w_up, b_up, w_down, b_down, final):
    B, S, D = x.shape
    N = B * S
    E = N_EXPERTS
    tm = EXPERT_TILE
    n_tok_tiles = N // ROUTE_TILE
    max_rows = N * TOP_K + n_tok_tiles * E * (SEG_ALIGN - 1)
    n_tiles = -(-max_rows // tm) + E
    tot = totals[0, :E].astype(jnp.int32)
    tiles_per = (tot + tm - 1) // tm
    tile_end = jnp.cumsum(tiles_per)
    row_start = (tile_end - tiles_per) * tm
    nact = tile_end[-1:]
    jj = jnp.minimum(jnp.arange(n_tiles, dtype=jnp.int32), nact[0] - 1)
    tile_e = jnp.sum((jj[:, None] >= tile_end[None, :]).astype(jnp.int32), axis=1)
    place = (row_start[None, :] + seg[:, 0, :E]).reshape(-1)
    seg_cnt = seg[:, 1, :E].reshape(-1)
    seg_off = seg[:, 2, :E].reshape(-1)
    loc2 = loc.reshape(N, LANES)
    xs = _dispatch(h.reshape(N, D), loc2, place, seg_cnt, seg_off, row_start + tot,
                   tiles_per * tm - tot, n_tiles * tm)
    y = _experts(xs, tile_e, jj, nact, layer, w_gate, b_gate, w_up, b_up, w_down, b_down)
    return _combine(y, loc, pw, place, seg_cnt, seg_off, x, g_f, final)


def _moba_proj_kernel(x_ref, sha_ref, sca_ref, ga_ref, shk_ref, sck_ref, gk_ref, wq_ref, wkv_ref,
                      cos_ref, sin_ref, q_ref, k_ref, v_ref, km_ref):
    x = x_ref[...]
    xn = x * _rms_scale(x)
    h = ((xn * ga_ref[...]) * (1.0 + sca_ref[...]) + sha_ref[...]).astype(BF16)
    hkv = ((xn * gk_ref[...]) * (1.0 + sck_ref[...]) + shk_ref[...]).astype(BF16)
    q = jnp.dot(h, wq_ref[...], preferred_element_type=F32)
    kv = jnp.dot(hkv, wkv_ref[...], preferred_element_type=F32)
    cos = cos_ref[...]
    sin = sin_ref[...]
    scale = LOG2_E * MOBA_HEAD_DIM ** -0.5
    hd = MOBA_HEAD_DIM
    t = x.shape[0]
    width = MOBA_HEADS * hd
    ones = _ones_column(t)
    seq_row = pl.program_id(1) * t + lax.broadcasted_iota(jnp.int32, (t, LANES), 0)
    lane = lax.broadcasted_iota(jnp.int32, (t, LANES), 1)
    block_id = jnp.where(seq_row // MOBA_BLOCK == lane, 1.0, 0.0).astype(BF16)
    for hh in range(MOBA_HEADS):
        hs = slice(hh * hd, (hh + 1) * hd)
        lo = slice(hh * ATTN_WIDTH, hh * ATTN_WIDTH + LANES)
        hi = slice(hh * ATTN_WIDTH + LANES, (hh + 1) * ATTN_WIDTH)
        qh = q[:, hs]
        kh = kv[:, hs]
        q_ref[:, hs] = ((qh * cos + pltpu.roll(qh, hd // 2, 1) * sin) * scale).astype(BF16)
        kr = kh * cos + pltpu.roll(kh, hd // 2, 1) * sin
        k_ref[:, lo] = kr.astype(BF16)
        k_ref[:, hi] = block_id
        km_ref[:, hs] = jnp.mean(kr.reshape(t // MOBA_BLOCK, MOBA_BLOCK, hd), axis=1)
        v_ref[:, lo] = kv[:, width + hh * hd:width + (hh + 1) * hd].astype(BF16)
        v_ref[:, hi] = ones


def _moba_proj(x, sh_a, sc_a, g_a, sh_k, sc_k, g_k, w_q, w_kv, cos, sin):
    B, S, D = x.shape
    t = TOKEN_TILE
    H = MOBA_HEADS
    width = H * MOBA_HEAD_DIM
    per = t // MOBA_BLOCK
    assert S // MOBA_BLOCK <= LANES
    tok = lambda w: pl.BlockSpec((None, t, w), lambda b, i: (b, i, 0))
    vec = lambda w: pl.BlockSpec((None, 1, w), lambda b, i: (b, 0, 0))
    full = lambda a: pl.BlockSpec(a.shape, lambda b, i: (0,) * a.ndim)
    ga2 = g_a.reshape(1, D)
    gk2 = g_k.reshape(1, D)
    wq = w_q.astype(BF16)
    wkv = w_kv.astype(BF16)
    wide = jax.ShapeDtypeStruct((B, S, H * ATTN_WIDTH), BF16)
    q, k, v, km = pl.pallas_call(
        _moba_proj_kernel,
        out_shape=(jax.ShapeDtypeStruct((B, S, width), BF16), wide, wide,
                   jax.ShapeDtypeStruct((B, S // t, per, width), F32)),
        grid=(B, S // t),
        in_specs=[tok(D), vec(D), vec(D), full(ga2), vec(D), vec(D), full(gk2), full(wq), full(wkv),
                  tok(LANES), tok(LANES)],
        out_specs=(tok(width), tok(H * ATTN_WIDTH), tok(H * ATTN_WIDTH),
                   pl.BlockSpec((None, None, per, width), lambda b, i: (b, i, 0, 0))),
        name="moba_proj",
    )(x, sh_a, sc_a, ga2, sh_k, sc_k, gk2, wq, wkv, cos, sin)
    return q, k, v, km.reshape(B, S // MOBA_BLOCK, width)


def _split_mod(mod, n):
    return [mod[:, None, j * D_MODEL:(j + 1) * D_MODEL] for j in range(n)]


def kernel(x, c, positions, ada_w, ada_b, norm_attn_g, norm_ffn_g, mla_w_in, mla_q_norm_g, mla_w_uq, mla_kv_norm_g, mla_w_ukv, mla_w_o, kv_ada_w, kv_ada_b, kv_norm_g, moba_w_kv, moba_w_q, moba_w_o, router_w, router_b, w_gate, b_gate, w_up, b_up, w_down, b_down, final_ada_w, final_ada_b, final_norm_g):
    mods = _ada_linear(c, ada_w, ada_b)
    kv_mod = _ada_linear(c, kv_ada_w[None], kv_ada_b[None])[0]
    f_mod = _ada_linear(c, final_ada_w[None], final_ada_b[None])[0]
    cos_a, sin_a, cos_b, sin_b = _rope_tables(positions)
    f_sh, f_sc = _split_mod(f_mod, 2)
    shared = None
    for layer in range(DEPTH):
        sh_a, sc_a, g_a, sh_f, sc_f, g_f = _split_mod(mods[layer], 6)
        if layer < N_A:
            q, k, v = _mla_proj(x, sh_a, sc_a, norm_attn_g[layer], mla_w_in[layer], mla_q_norm_g[layer],
                                mla_w_uq[layer], mla_kv_norm_g[layer], mla_w_ukv[layer], cos_a, sin_a)
            attn = _attention(q, k, v, None, heads=MLA_HEADS, dq=ATTN_WIDTH, dv=MLA_V, moba=False)
            w_o = mla_w_o[layer]
        else:
            j = layer - N_A
            kv_sh, kv_sc = _split_mod(kv_mod, 2)
            q, k, v, km = _moba_proj(x, sh_a, sc_a, norm_attn_g[layer], kv_sh, kv_sc, kv_norm_g,
                                     moba_w_q[j], moba_w_kv, cos_b, sin_b)
            if shared is None:
                shared = (k, v, km)
            attn = _attention(q, shared[0], shared[1], shared[2], heads=MOBA_HEADS, dq=MOBA_HEAD_DIM,
                              dv=MOBA_HEAD_DIM, moba=True)
            w_o = moba_w_o[j]
        x, h, loc, pw, seg, totals = _attn_out(attn, w_o, x, g_a, norm_ffn_g[layer], sh_f, sc_f,
                                               router_w[layer], router_b[layer])
        final = (final_norm_g, f_sh, f_sc) if layer == DEPTH - 1 else None
        x = _moe(h, loc, pw, seg, totals, x, g_f, layer, w_gate, b_gate, w_up, b_up, w_down, b_down,
                 final)
    return x
```

```python
import functools

import jax
import jax.numpy as jnp
from jax import lax
from jax.experimental import pallas as pl
from jax.experimental.pallas import tpu as pltpu

D_MODEL = 1024
DEPTH = 2
N_A = DEPTH // 2

MLA_HEADS = 8
MLA_Q_LORA = 256
MLA_KV_LORA = 128
MLA_NOPE = 128
MLA_ROPE = 64
MLA_V = 128

MOBA_HEADS = 8
MOBA_HEAD_DIM = D_MODEL // MOBA_HEADS
MOBA_BLOCK = 256
MOBA_TOPK = 3

N_EXPERTS = 32
TOP_K = 4
SWIGLU_LIMIT = 7.0
SWIGLU_ALPHA = 1.702

ROPE_THETA = 10000.0
NORM_EPS = 1e-6
NEG_INF = -1e30
LOG2_E = 1.4426950408889634
SOFTMAX_M_INIT = -1e20

LANES = 128
TOKEN_TILE = 512
ATTN_Q_TILE = 1024
ATTN_KV_TILE = 512
ATTN_WIDTH = 2 * LANES
ATTN_HEADS_PER_STEP = 2
EXPERT_TILE = 512
SEG_ALIGN = 8
SEG_CHUNK = 64
ROUTE_TILE = 512
SEG_BUF_ROWS = ROUTE_TILE * TOP_K + N_EXPERTS * SEG_ALIGN
VMEM_LIMIT_BYTES = 56 * 1024 * 1024

F32 = jnp.float32
BF16 = jnp.bfloat16


def _rms_scale(x):
    return lax.rsqrt(jnp.mean(x * x, axis=-1, keepdims=True) + NORM_EPS)


def _nt_dot(a, b):
    return lax.dot_general(a, b, (((1,), (1,)), ((), ())), preferred_element_type=F32)


def _ones_column(rows):
    lane = lax.broadcasted_iota(jnp.int32, (rows, LANES), 1)
    return jnp.where(lane == 0, 1.0, 0.0).astype(BF16)


def _ada_kernel(c_ref, w_ref, b_ref, o_ref):
    c = c_ref[...]
    ca = c * jax.nn.sigmoid(c)
    o_ref[...] = jnp.dot(ca, w_ref[...], precision=lax.Precision.HIGHEST,
                         preferred_element_type=F32) + b_ref[...]


def _ada_linear(c, w, b):
    L, D, M = w.shape
    B = c.shape[0]
    bn = 1024
    return pl.pallas_call(
        _ada_kernel,
        out_shape=jax.ShapeDtypeStruct((L, B, M), F32),
        grid=(L, M // bn),
        in_specs=[
            pl.BlockSpec((B, D), lambda l, j: (0, 0)),
            pl.BlockSpec((None, D, bn), lambda l, j: (l, 0, j)),
            pl.BlockSpec((None, 1, bn), lambda l, j: (l, 0, j)),
        ],
        out_specs=pl.BlockSpec((None, B, bn), lambda l, j: (l, 0, j)),
        name="ada_linear",
    )(c, w, b.reshape(L, 1, M))


def _rope_table_kernel(pos_ref, inv_ref, sign_ref, ca_ref, sa_ref, cb_ref, sb_ref):
    ang = pos_ref[...] * inv_ref[...]
    lane = lax.broadcasted_iota(jnp.int32, ang.shape, 1)
    na, nb = MLA_ROPE // 2, MOBA_HEAD_DIM // 2

    def spread(v):
        a = jnp.where(lane < na, v, 0.0)
        a = a + pltpu.roll(a, na, 1)
        a = a + pltpu.roll(a, 2 * na, 1)
        b = jnp.where((lane >= na) & (lane < na + nb), v, 0.0)
        b = pltpu.roll(b, LANES - na, 1)
        b = b + pltpu.roll(b, nb, 1)
        return a, b

    ca_ref[...], cb_ref[...] = spread(jnp.cos(ang))
    sa, sb = spread(jnp.sin(ang))
    sa_ref[...] = sa
    sb_ref[...] = sb * sign_ref[...]


def _rope_tables(positions):
    B, S = positions.shape
    t = 1024
    assert MLA_ROPE // 2 + MOBA_HEAD_DIM // 2 <= LANES and MLA_ROPE * 2 == LANES == MOBA_HEAD_DIM

    def inv_freq(d):
        half = d // 2
        return ROPE_THETA ** (-jnp.arange(half, dtype=F32) * (2.0 / d))

    inv = jnp.concatenate([inv_freq(MLA_ROPE), inv_freq(MOBA_HEAD_DIM),
                           jnp.zeros((LANES - MLA_ROPE // 2 - MOBA_HEAD_DIM // 2,), F32)]).reshape(1, LANES)
    half = MOBA_HEAD_DIM // 2
    sign = jnp.concatenate([-jnp.ones((half,), F32), jnp.ones((half,), F32)]).reshape(1, LANES)
    pos = positions.astype(F32).reshape(B, S, 1)
    row = pl.BlockSpec((1, LANES), lambda b, i: (0, 0))
    out = pl.BlockSpec((None, t, LANES), lambda b, i: (b, i, 0))
    shp = jax.ShapeDtypeStruct((B, S, LANES), F32)
    return pl.pallas_call(
        _rope_table_kernel,
        out_shape=(shp, shp, shp, shp),
        grid=(B, S // t),
        in_specs=[pl.BlockSpec((None, t, 1), lambda b, i: (b, i, 0)), row, row],
        out_specs=(out, out, out, out),
        name="rope_tables",
    )(pos, inv, sign)


def _mla_proj_kernel(x_ref, sh_ref, sc_ref, g_ref, win_ref, qg_ref, wqn_ref, wqr_ref, wqrr_ref,
                     kvg_ref, wkn_ref, wv_ref, cos_ref, sin_ref, q_ref, k_ref, v_ref):
    x = x_ref[...]
    y = x * _rms_scale(x) * g_ref[...]
    h = (y * (1.0 + sc_ref[...]) + sh_ref[...]).astype(BF16)
    proj = jnp.dot(h, win_ref[...], preferred_element_type=F32)
    c_q = proj[:, :MLA_Q_LORA]
    c_kv = proj[:, MLA_Q_LORA:MLA_Q_LORA + MLA_KV_LORA]
    kr_a = proj[:, 384:512]
    kr_b = proj[:, 512:640]
    cq = (c_q * _rms_scale(c_q) * qg_ref[...]).astype(BF16)
    ckv = (c_kv * _rms_scale(c_kv) * kvg_ref[...]).astype(BF16)
    cos = cos_ref[...]
    sin = sin_ref[...]
    scale = LOG2_E * (MLA_NOPE + MLA_ROPE) ** -0.5
    kr = (kr_a * cos + kr_b * sin).astype(BF16)
    q_nope = jnp.dot(cq, wqn_ref[...], preferred_element_type=F32) * scale
    q_ra = jnp.dot(cq, wqr_ref[...], preferred_element_type=F32)
    q_rb = jnp.dot(cq, wqrr_ref[...], preferred_element_type=F32)
    k_nope = jnp.dot(ckv, wkn_ref[...], preferred_element_type=F32)
    v_ref[...] = jnp.dot(ckv, wv_ref[...], preferred_element_type=F32).astype(BF16)
    for hh in range(MLA_HEADS):
        hs = slice(hh * LANES, (hh + 1) * LANES)
        lo = slice(hh * ATTN_WIDTH, hh * ATTN_WIDTH + LANES)
        hi = slice(hh * ATTN_WIDTH + LANES, (hh + 1) * ATTN_WIDTH)
        q_ref[:, lo] = q_nope[:, hs].astype(BF16)
        q_ref[:, hi] = ((q_ra[:, hs] * cos + q_rb[:, hs] * sin) * scale).astype(BF16)
        k_ref[:, lo] = k_nope[:, hs].astype(BF16)
        k_ref[:, hi] = kr


def _rot_half_cols(w, half):
    return jnp.concatenate([-w[..., half:], w[..., :half]], axis=-1)


def _mla_proj(x, sh, sc, g, w_in, q_norm_g, w_uq, kv_norm_g, w_ukv, cos, sin):
    B, S, D = x.shape
    H = MLA_HEADS
    t = TOKEN_TILE
    half = MLA_ROPE // 2
    w_kr = w_in[:, MLA_Q_LORA + MLA_KV_LORA:]
    zpad = jnp.zeros((D, LANES - MLA_ROPE), F32)
    w_in_ext = jnp.concatenate(
        [w_in[:, :MLA_Q_LORA + MLA_KV_LORA], w_kr, zpad, _rot_half_cols(w_kr, half), zpad],
        axis=1).astype(BF16)
    wq = w_uq.reshape(MLA_Q_LORA, H, MLA_NOPE + MLA_ROPE)
    wq_nope = wq[..., :MLA_NOPE].reshape(MLA_Q_LORA, H * MLA_NOPE).astype(BF16)
    wq_r = wq[..., MLA_NOPE:]
    pad = ((0, 0), (0, 0), (0, LANES - MLA_ROPE))
    wq_rope = jnp.pad(wq_r, pad).reshape(MLA_Q_LORA, H * LANES).astype(BF16)
    wq_rope_rot = jnp.pad(_rot_half_cols(wq_r, half), pad).reshape(MLA_Q_LORA, H * LANES).astype(BF16)
    wkv = w_ukv.reshape(MLA_KV_LORA, H, MLA_NOPE + MLA_V)
    wk_nope = wkv[..., :MLA_NOPE].reshape(MLA_KV_LORA, H * MLA_NOPE).astype(BF16)
    wv = wkv[..., MLA_NOPE:].reshape(MLA_KV_LORA, H * MLA_V).astype(BF16)

    tok = lambda w: pl.BlockSpec((None, t, w), lambda b, i: (b, i, 0))
    vec = lambda w: pl.BlockSpec((None, 1, w), lambda b, i: (b, 0, 0))
    full = lambda a: pl.BlockSpec(a.shape, lambda b, i: (0,) * a.ndim)
    g2 = g.reshape(1, D)
    qg2 = q_norm_g.reshape(1, MLA_Q_LORA)
    kvg2 = kv_norm_g.reshape(1, MLA_KV_LORA)
    wide = jax.ShapeDtypeStruct((B, S, H * ATTN_WIDTH), BF16)
    return pl.pallas_call(
        _mla_proj_kernel,
        out_shape=(wide, wide, jax.ShapeDtypeStruct((B, S, H * MLA_V), BF16)),
        grid=(B, S // t),
        in_specs=[tok(D), vec(D), vec(D), full(g2), full(w_in_ext), full(qg2), full(wq_nope),
                  full(wq_rope), full(wq_rope_rot), full(kvg2), full(wk_nope), full(wv),
                  tok(LANES), tok(LANES)],
        out_specs=(tok(H * ATTN_WIDTH), tok(H * ATTN_WIDTH), tok(H * MLA_V)),
        name="mla_proj",
    )(x, sh, sc, g2, w_in_ext, qg2, wq_nope, wq_rope, wq_rope_rot, kvg2, wk_nope, wv, cos, sin)


def _attn_kernel(*refs, moba, tq, bk, dq, dv, nh):
    n_in = 4 if moba else 3
    q_ref, k_ref, v_ref = refs[:3]
    o_ref = refs[n_in]
    scratch = refs[n_in + 1:]
    if moba:
        kx_ref, vx_ref = scratch[:2]
        scratch = scratch[2:]
    else:
        kx_ref, vx_ref = k_ref, scratch[0]
        scratch = scratch[1:]
    per_head = len(scratch) // nh
    w = ATTN_WIDTH
    i = pl.program_id(2)
    seq = vx_ref.shape[0]
    row_id = lax.broadcasted_iota(jnp.int32, (tq, 1), 0)
    col_id = lax.broadcasted_iota(jnp.int32, (1, bk), 1)

    @pl.when((pl.program_id(0) == 0) & (pl.program_id(1) == 0) & (i == 0))
    def _():
        ones = _ones_column(seq)
        if moba:
            key_row = lax.broadcasted_iota(jnp.int32, (seq, LANES), 0)
            lane = lax.broadcasted_iota(jnp.int32, (seq, LANES), 1)
            block_id = jnp.where(key_row // MOBA_BLOCK == lane, 1.0, 0.0).astype(BF16)
        for hh in range(nh):
            vx_ref[:, hh * w + dv:(hh + 1) * w] = ones
            if moba:
                kx_ref[:, hh * w + dq:(hh + 1) * w] = block_id

    @pl.when(i == 0)
    def _():
        for hh in range(nh):
            vx_ref[:, hh * w:hh * w + dv] = v_ref[:, hh * dv:(hh + 1) * dv]
            if moba:
                kx_ref[:, hh * w:hh * w + dq] = k_ref[:, hh * dq:(hh + 1) * dq]

    def head_scratch(hh):
        sc = scratch[hh * per_head:(hh + 1) * per_head]
        return sc if moba else (None,) + tuple(sc)

    def prepare(hh):
        qx_ref, s_a, s_b, p_a, p_b, al_a, al_b, m_sc, acc_sc = head_scratch(hh)
        if moba:
            km_ref = refs[3]
            q = q_ref[:, hh * dq:(hh + 1) * dq]
            nb = km_ref.shape[0]
            gate = _nt_dot(km_ref[:, hh * dq:(hh + 1) * dq].astype(BF16), q)
            blk = lax.broadcasted_iota(jnp.int32, (nb, tq), 0)
            blk_f = blk.astype(F32)
            own = (i * tq + lax.broadcasted_iota(jnp.int32, (1, tq), 1)) // MOBA_BLOCK
            gate = jnp.where(blk < own, gate, -jnp.inf)
            sel = jnp.where(blk == own, 1.0, 0.0)
            for _ in range(MOBA_TOPK):
                mx = jnp.max(gate, axis=0, keepdims=True)
                first = jnp.min(jnp.where(gate == mx, blk_f, float(nb)), axis=0, keepdims=True)
                hit = blk_f == first
                sel = jnp.where(hit & (mx > -jnp.inf), 1.0, sel)
                gate = jnp.where(hit, -jnp.inf, gate)
            bias = jnp.where(sel > 0.0, 0.0, NEG_INF)
            bias = jnp.concatenate([bias, jnp.zeros((LANES - nb, tq), F32)], axis=0)
            qx_ref[:, :LANES] = q
            qx_ref[:, LANES:] = bias.T.astype(BF16)
        m_sc[...] = jnp.full(m_sc.shape, SOFTMAX_M_INIT, F32)
        acc_sc[...] = jnp.zeros(acc_sc.shape, F32)
        p_b[...] = jnp.zeros(p_b.shape, BF16)
        al_b[...] = jnp.ones(al_b.shape, F32)

    def scores(hh, j, slot, rows=slice(None)):
        sc = head_scratch(hh)
        qx = sc[0][rows, :] if moba else q_ref[rows, hh * w:(hh + 1) * w]
        start = pl.multiple_of(j * bk, bk)
        sc[1 + slot][rows, :] = _nt_dot(qx, kx_ref[pl.ds(start, bk), hh * w:(hh + 1) * w])

    def soft(hh, slot, diag_offset=None, rows=slice(None)):
        sc = head_scratch(hh)
        s_src, p_dst, al_dst, m_sc = sc[1 + slot], sc[3 + slot], sc[5 + slot], sc[7]
        s = s_src[rows, :]
        if diag_offset is not None:
            s = jnp.where(col_id + diag_offset <= row_id[rows, :], s, -jnp.inf)
        m_prev = m_sc[rows, :]
        m_new = jnp.maximum(m_prev, jnp.max(s, axis=1, keepdims=True))
        p_dst[rows, :] = jnp.exp2(s - m_new).astype(BF16)
        al_dst[rows, :] = jnp.exp2(m_prev - m_new)
        m_sc[rows, :] = m_new

    def fold(hh, j, slot, rows=slice(None)):
        sc = head_scratch(hh)
        p_src, al_src, acc_sc = sc[3 + slot], sc[5 + slot], sc[8]
        start = pl.multiple_of(j * bk, bk)
        acc_sc[rows, :] = al_src[rows, :] * acc_sc[rows, :] + jnp.dot(
            p_src[rows, :], vx_ref[pl.ds(start, bk), hh * w:(hh + 1) * w], preferred_element_type=F32)

    hs = range(nh)
    for hh in hs:
        prepare(hh)
    for hh in hs:
        scores(hh, 0, 0)

    def body(t, carry):
        j = 2 * t
        for hh in hs:
            scores(hh, j + 1, 1)
        for hh in hs:
            soft(hh, 0)
        for hh in hs:
            fold(hh, jnp.maximum(j - 1, 0), 1)
        for hh in hs:
            scores(hh, j + 2, 0)
        for hh in hs:
            soft(hh, 1)
        for hh in hs:
            fold(hh, j, 0)
        return carry

    lax.fori_loop(0, i, body, 0)
    j = 2 * i
    late = slice(bk, tq)
    for hh in hs:
        scores(hh, j + 1, 1, late)
    for hh in hs:
        soft(hh, 0, diag_offset=0)
    for hh in hs:
        fold(hh, jnp.maximum(j - 1, 0), 1)
    for hh in hs:
        soft(hh, 1, diag_offset=bk, rows=late)
    for hh in hs:
        fold(hh, j, 0)
    for hh in hs:
        fold(hh, j + 1, 1, late)
    for hh in hs:
        acc = head_scratch(hh)[8][...]
        o_ref[:, hh * dv:(hh + 1) * dv] = (acc[:, :dv] / acc[:, dv:dv + 1]).astype(o_ref.dtype)


def _attention(q, k, v, k_mean, *, heads, dq, dv, moba):
    B, S, _ = q.shape
    tq, bk, w, nh = ATTN_Q_TILE, ATTN_KV_TILE, ATTN_WIDTH, ATTN_HEADS_PER_STEP
    assert tq == 2 * bk and tq % MOBA_BLOCK == 0 and S % tq == 0 and heads % nh == 0
    in_specs = [
        pl.BlockSpec((None, tq, nh * dq), lambda b, h, i: (b, i, h)),
        pl.BlockSpec((None, S, nh * (dq if moba else w)), lambda b, h, i: (b, 0, h)),
        pl.BlockSpec((None, S, nh * dv), lambda b, h, i: (b, 0, h)),
    ]
    args = [q, k, v]
    scratch = [pltpu.VMEM((S, nh * w), BF16)]
    if moba:
        nb = k_mean.shape[1]
        in_specs.append(pl.BlockSpec((None, nb, nh * dq), lambda b, h, i: (b, 0, h)))
        args.append(k_mean)
        scratch.insert(0, pltpu.VMEM((S, nh * w), BF16))
    for _ in range(nh):
        if moba:
            scratch.append(pltpu.VMEM((tq, w), BF16))
        scratch += [pltpu.VMEM((tq, bk), F32), pltpu.VMEM((tq, bk), F32),
                    pltpu.VMEM((tq, bk), BF16), pltpu.VMEM((tq, bk), BF16),
                    pltpu.VMEM((tq, 1), F32), pltpu.VMEM((tq, 1), F32),
                    pltpu.VMEM((tq, 1), F32), pltpu.VMEM((tq, w), F32)]
    return pl.pallas_call(
        functools.partial(_attn_kernel, moba=moba, tq=tq, bk=bk, dq=dq, dv=dv, nh=nh),
        out_shape=jax.ShapeDtypeStruct((B, S, heads * dv), BF16),
        grid=(B, heads // nh, S // tq),
        in_specs=in_specs,
        out_specs=pl.BlockSpec((None, tq, nh * dv), lambda b, h, i: (b, i, h)),
        scratch_shapes=scratch,
        compiler_params=pltpu.CompilerParams(
            dimension_semantics=("arbitrary", "arbitrary", "arbitrary"),
            vmem_limit_bytes=VMEM_LIMIT_BYTES),
        name="moba_attention" if moba else "mla_attention",
    )(*args)


def _attn_out_kernel(a_ref, wo_ref, x_ref, ga_ref, g_ref, sh_ref, sc_ref, rw_ref, rb_ref,
                     x1_ref, h_ref, loc_ref, pw_ref, seg_ref, tot_ref, run_ref):
    first_step = (pl.program_id(0) == 0) & (pl.program_id(1) == 0)

    @pl.when(first_step)
    def _():
        run_ref[...] = jnp.zeros(run_ref.shape, F32)

    a = jnp.dot(a_ref[...], wo_ref[...], preferred_element_type=F32)
    x1 = x_ref[...] + ga_ref[...] * a
    x1_ref[...] = x1
    y = x1 * _rms_scale(x1) * g_ref[...]
    h = (y * (1.0 + sc_ref[...]) + sh_ref[...]).astype(BF16)
    h_ref[...] = h
    logits = jnp.dot(h, rw_ref[...], preferred_element_type=F32) + rb_ref[...]
    t = logits.shape[0]
    lane_i = lax.broadcasted_iota(jnp.int32, logits.shape, 1)
    lane = lane_i.astype(F32)
    logits = jnp.where(lane_i < N_EXPERTS, logits, -jnp.inf)
    top = None
    den = jnp.zeros((t, 1), F32)
    sel = jnp.zeros(logits.shape, F32)
    hits, ws = [], []
    for _ in range(TOP_K):
        mx = jnp.max(logits, axis=1, keepdims=True)
        first = jnp.min(jnp.where(logits == mx, lane, float(LANES)), axis=1, keepdims=True)
        hit = lane == first
        if top is None:
            top = mx
        w = jnp.exp(mx - top)
        den = den + w
        sel = jnp.where(hit, 1.0, sel)
        logits = jnp.where(hit, -jnp.inf, logits)
        hits.append(hit)
        ws.append(w)
    rt = ROUTE_TILE
    r_id = lax.broadcasted_iota(jnp.int32, (rt, rt), 0)
    c_id = lax.broadcasted_iota(jnp.int32, (rt, rt), 1)
    before = jnp.where(c_id < r_id, 1.0, 0.0).astype(BF16)
    a_id = lax.broadcasted_iota(jnp.int32, (LANES, LANES), 0)
    b_id = lax.broadcasted_iota(jnp.int32, (LANES, LANES), 1)
    earlier = jnp.where(a_id < b_id, 1.0, 0.0).astype(BF16)
    row8 = lax.broadcasted_iota(jnp.int32, (8, LANES), 0)
    local = []
    for sub in range(t // rt):
        sel_s = sel[sub * rt:(sub + 1) * rt]
        prefix = jnp.dot(before, sel_s.astype(BF16), preferred_element_type=F32)
        cnt = jnp.sum(sel_s, axis=0, keepdims=True)
        units = jnp.floor((cnt + (SEG_ALIGN - 1.0)) * (1.0 / SEG_ALIGN))
        cnt_pad = units * SEG_ALIGN
        seg_off = jnp.dot(jnp.broadcast_to(units, (8, LANES)).astype(BF16), earlier,
                          preferred_element_type=F32)[0:1] * SEG_ALIGN
        base = run_ref[...]
        run_ref[...] = base + cnt_pad
        seg = jnp.where(row8 == 0, base, jnp.where(row8 == 1, cnt_pad, jnp.where(row8 == 2, seg_off, 0.0)))
        seg_ref[sub] = seg.astype(jnp.int32)
        local.append(prefix + seg_off)
    tot_ref[...] = run_ref[...]
    local = jnp.concatenate(local, axis=0)
    loc = jnp.zeros(logits.shape, F32)
    pw = jnp.zeros(logits.shape, F32)
    for kk in range(TOP_K):
        row = jnp.sum(jnp.where(hits[kk], local, 0.0), axis=1, keepdims=True)
        loc = jnp.where(lane_i == kk, row, loc)
        pw = jnp.where(lane_i == kk, ws[kk] / den, pw)
    loc_ref[...] = loc.astype(jnp.int32)
    pw_ref[...] = pw


def _attn_out(attn, w_o, x, g_a, norm_g, sh_f, sc_f, router_w, router_b):
    B, S, D = x.shape
    t = TOKEN_TILE
    per = S // t
    sub = t // ROUTE_TILE
    rw = jnp.pad(router_w, ((0, 0), (0, LANES - N_EXPERTS))).astype(BF16)
    rb = jnp.pad(router_b, (0, LANES - N_EXPERTS)).reshape(1, LANES)
    wo = w_o.astype(BF16)
    g2 = norm_g.reshape(1, D)
    tok = lambda w: pl.BlockSpec((None, t, w), lambda b, i: (b, i, 0))
    vec = lambda w: pl.BlockSpec((None, 1, w), lambda b, i: (b, 0, 0))
    full = lambda a: pl.BlockSpec(a.shape, lambda b, i: (0,) * a.ndim)
    return pl.pallas_call(
        _attn_out_kernel,
        out_shape=(jax.ShapeDtypeStruct((B, S, D), F32),
                   jax.ShapeDtypeStruct((B, S, D), BF16),
                   jax.ShapeDtypeStruct((B, S, LANES), jnp.int32),
                   jax.ShapeDtypeStruct((B, S, LANES), F32),
                   jax.ShapeDtypeStruct((B * per * sub, 8, LANES), jnp.int32),
                   jax.ShapeDtypeStruct((1, LANES), F32)),
        grid=(B, per),
        in_specs=[tok(attn.shape[-1]), full(wo), tok(D), vec(D), full(g2), vec(D), vec(D),
                  full(rw), full(rb)],
        out_specs=(tok(D), tok(D), tok(LANES), tok(LANES),
                   pl.BlockSpec((sub, 8, LANES), lambda b, i: (b * per + i, 0, 0)),
                   pl.BlockSpec((1, LANES), lambda b, i: (0, 0))),
        scratch_shapes=[pltpu.VMEM((1, LANES), F32)],
        compiler_params=pltpu.CompilerParams(dimension_semantics=("arbitrary", "arbitrary")),
        name="attn_out_router",
    )(attn, wo, x, g_a, g2, sh_f, sc_f, rw, rb)


def _for_each_chunk(cnt, fn):
    shift = SEG_CHUNK.bit_length() - 1
    whole = cnt >> shift

    def piece(c, carry):
        fn(pl.multiple_of(c * SEG_CHUNK, SEG_CHUNK), SEG_CHUNK)
        return carry

    lax.fori_loop(0, whole, piece, 0)
    rows = SEG_CHUNK // 2
    while rows >= SEG_ALIGN:
        shift = rows.bit_length()
        done = (cnt >> shift) << shift

        @pl.when((cnt & rows) != 0)
        def _(done=done, rows=rows):
            fn(pl.multiple_of(done, SEG_ALIGN), rows)

        rows //= 2


def _dispatch_kernel(dst_ref, cnt_ref, off_ref, pad_start_ref, pad_cnt_ref, h_ref, loc_ref,
                     xs_ref, seg_buf, zero_buf, sems):
    i = pl.program_id(0)
    last = pl.num_programs(0) - 1
    slot = lax.rem(i, 2)
    t = h_ref.shape[0]
    n_loc = seg_buf.shape[1]
    pad_sem = sems.at[2]

    @pl.when(i == 0)
    def _():
        zero_buf[...] = jnp.zeros(zero_buf.shape, F32)
        for wait in (False, True):
            def per_expert(e, carry, wait=wait):
                start = pad_start_ref[e]

                def chunk(done, rows):
                    cp = pltpu.make_async_copy(
                        zero_buf.at[pl.ds(0, rows), :],
                        xs_ref.at[pl.ds(pl.multiple_of(start + done, SEG_ALIGN), rows), :], pad_sem)
                    cp.wait() if wait else cp.start()

                _for_each_chunk(pad_cnt_ref[e], chunk)
                return carry

            lax.fori_loop(0, N_EXPERTS, per_expert, 0)

    def segment_copies(tile, buf, wait):
        def per_expert(e, carry):
            idx = tile * N_EXPERTS + e
            off = off_ref[idx]
            dst = dst_ref[idx]

            def chunk(done, rows):
                cp = pltpu.make_async_copy(
                    seg_buf.at[buf, pl.ds(pl.multiple_of(off + done, SEG_ALIGN), rows), :],
                    xs_ref.at[pl.ds(pl.multiple_of(dst + done, SEG_ALIGN), rows), :], sems.at[buf])
                cp.wait() if wait else cp.start()

            _for_each_chunk(cnt_ref[idx], chunk)
            return carry

        lax.fori_loop(0, N_EXPERTS, per_expert, 0)

    loc_t = loc_ref[...].astype(F32).T
    l_id = lax.broadcasted_iota(jnp.int32, (n_loc, t), 0).astype(F32)
    onehot = jnp.zeros((n_loc, t), F32)
    for kk in range(TOP_K):
        onehot = jnp.where(l_id == loc_t[kk:kk + 1, :], 1.0, onehot)
    seg_buf[slot] = jnp.dot(onehot.astype(BF16), h_ref[...], preferred_element_type=F32)
    segment_copies(i, slot, wait=False)

    @pl.when(i > 0)
    def _():
        segment_copies(i - 1, 1 - slot, wait=True)

    @pl.when(i == last)
    def _():
        segment_copies(i, slot, wait=True)


def _dispatch(h, loc, dst, seg_cnt, seg_off, pad_start, pad_cnt, n_rows):
    N, D = h.shape
    t = ROUTE_TILE
    return pl.pallas_call(
        _dispatch_kernel,
        out_shape=jax.ShapeDtypeStruct((n_rows, D), F32),
        grid_spec=pltpu.PrefetchScalarGridSpec(
            num_scalar_prefetch=5,
            grid=(N // t,),
            in_specs=[pl.BlockSpec((t, D), lambda i, *_: (i, 0)),
                      pl.BlockSpec((t, LANES), lambda i, *_: (i, 0))],
            out_specs=pl.BlockSpec(memory_space=pl.ANY),
            scratch_shapes=[pltpu.VMEM((2, SEG_BUF_ROWS, D), F32),
                            pltpu.VMEM((SEG_CHUNK, D), F32),
                            pltpu.SemaphoreType.DMA((3,))],
        ),
        compiler_params=pltpu.CompilerParams(dimension_semantics=("arbitrary",),
                                             vmem_limit_bytes=VMEM_LIMIT_BYTES),
        name="moe_dispatch",
    )(dst, seg_cnt, seg_off, pad_start, pad_cnt, h, loc)


def _experts_kernel(tile_e_ref, tile_blk_ref, nact_ref, xs_ref, wg_ref, bg_ref, wu_ref, bu_ref,
                    wd_ref, bd_ref, y_ref, wg_bf, wu_bf, wd_bf):
    j = pl.program_id(0)

    @pl.when(j < nact_ref[0])
    def _():
        prev = tile_e_ref[jnp.maximum(j - 1, 0)]

        @pl.when((j == 0) | (tile_e_ref[j] != prev))
        def _():
            wg_bf[...] = wg_ref[...].astype(BF16)
            wu_bf[...] = wu_ref[...].astype(BF16)
            wd_bf[...] = wd_ref[...].astype(BF16)

        x = xs_ref[...].astype(BF16)
        g = jnp.minimum(jnp.dot(x, wg_bf[...], preferred_element_type=F32) + bg_ref[...], SWIGLU_LIMIT)
        u = jnp.clip(jnp.dot(x, wu_bf[...], preferred_element_type=F32) + bu_ref[...],
                     -SWIGLU_LIMIT, SWIGLU_LIMIT)
        a = g * jax.nn.sigmoid(SWIGLU_ALPHA * g) * (u + 1.0)
        y_ref[...] = jnp.dot(a.astype(BF16), wd_bf[...], preferred_element_type=F32) + bd_ref[...]


def _experts(xs, tile_e, tile_blk, nact, layer, w_gate, b_gate, w_up, b_up, w_down, b_down):
    P, D = xs.shape
    L, E, _, F = w_gate.shape
    tm = EXPERT_TILE
    rows = pl.BlockSpec((tm, D), lambda j, te, tb, na: (tb[j], 0))
    wspec = lambda r, c: pl.BlockSpec((None, None, r, c), lambda j, te, tb, na: (layer, te[j], 0, 0))
    return pl.pallas_call(
        _experts_kernel,
        out_shape=jax.ShapeDtypeStruct((P, D), F32),
        grid_spec=pltpu.PrefetchScalarGridSpec(
            num_scalar_prefetch=3,
            grid=(P // tm,),
            in_specs=[rows, wspec(D, F), wspec(1, F), wspec(D, F), wspec(1, F), wspec(F, D), wspec(1, D)],
            out_specs=rows,
            scratch_shapes=[pltpu.VMEM((D, F), BF16), pltpu.VMEM((D, F), BF16), pltpu.VMEM((F, D), BF16)],
        ),
        compiler_params=pltpu.CompilerParams(dimension_semantics=("arbitrary",),
                                             vmem_limit_bytes=VMEM_LIMIT_BYTES),
        name="moe_experts",
    )(tile_e, tile_blk, nact, xs, w_gate, b_gate.reshape(L, E, 1, F), w_up, b_up.reshape(L, E, 1, F),
      w_down, b_down.reshape(L, E, 1, D))


def _combine_kernel(*refs, final_norm):
    if final_norm:
        (src_ref, cnt_ref, off_ref, y_ref, loc_ref, pw_ref, x_ref, gf_ref, fg_ref, fsh_ref, fsc_ref,
         o_ref, seg_buf, sems) = refs
    else:
        src_ref, cnt_ref, off_ref, y_ref, loc_ref, pw_ref, x_ref, gf_ref, o_ref, seg_buf, sems = refs
    i = pl.program_id(0)
    last = pl.num_programs(0) - 1
    slot = lax.rem(i, 2)
    t = x_ref.shape[0]
    n_loc = seg_buf.shape[1]

    def segment_copies(tile, buf, wait):
        def per_expert(e, carry):
            idx = tile * N_EXPERTS + e
            off = off_ref[idx]
            src = src_ref[idx]

            def chunk(done, rows):
                cp = pltpu.make_async_copy(
                    y_ref.at[pl.ds(pl.multiple_of(src + done, SEG_ALIGN), rows), :],
                    seg_buf.at[buf, pl.ds(pl.multiple_of(off + done, SEG_ALIGN), rows), :], sems.at[buf])
                cp.wait() if wait else cp.start()

            _for_each_chunk(cnt_ref[idx], chunk)
            return carry

        lax.fori_loop(0, N_EXPERTS, per_expert, 0)

    @pl.when(i == 0)
    def _():
        seg_buf[...] = jnp.zeros(seg_buf.shape, F32)
        segment_copies(0, 0, wait=False)

    @pl.when(i < last)
    def _():
        segment_copies(i + 1, 1 - slot, wait=False)

    segment_copies(i, slot, wait=True)
    loc = loc_ref[...].astype(F32)
    pw = pw_ref[...]
    l_id = lax.broadcasted_iota(jnp.int32, (t, n_loc), 1).astype(F32)
    weights = jnp.zeros((t, n_loc), F32)
    for kk in range(TOP_K):
        weights = jnp.where(l_id == loc[:, kk:kk + 1], pw[:, kk:kk + 1], weights)
    f = jnp.dot(weights.astype(BF16), seg_buf[slot].astype(BF16), preferred_element_type=F32)
    x2 = x_ref[...] + gf_ref[...] * f
    if final_norm:
        y = x2 * _rms_scale(x2) * fg_ref[...]
        x2 = y * (1.0 + fsc_ref[...]) + fsh_ref[...]
    o_ref[...] = x2


def _combine(y, loc, pw, src, seg_cnt, seg_off, x, g_f, final):
    B, S, D = x.shape
    t = ROUTE_TILE
    per = S // t
    tok = lambda w: pl.BlockSpec((None, t, w), lambda i, *_: (i // per, i % per, 0))
    vec = pl.BlockSpec((None, 1, D), lambda i, *_: (i // per, 0, 0))
    in_specs = [pl.BlockSpec(memory_space=pl.ANY), tok(LANES), tok(LANES), tok(D), vec]
    args = [y, loc, pw, x, g_f]
    if final is not None:
        fg, fsh, fsc = final
        in_specs += [pl.BlockSpec((1, D), lambda i, *_: (0, 0)), vec, vec]
        args += [fg.reshape(1, D), fsh, fsc]
    return pl.pallas_call(
        functools.partial(_combine_kernel, final_norm=final is not None),
        out_shape=jax.ShapeDtypeStruct((B, S, D), F32),
        grid_spec=pltpu.PrefetchScalarGridSpec(
            num_scalar_prefetch=3,
            grid=(B * per,),
            in_specs=in_specs,
            out_specs=tok(D),
            scratch_shapes=[pltpu.VMEM((2, SEG_BUF_ROWS, D), F32), pltpu.SemaphoreType.DMA((2,))],
        ),
        compiler_params=pltpu.CompilerParams(dimension_semantics=("arbitrary",),
                                             vmem_limit_bytes=VMEM_LIMIT_BYTES),
        name="moe_combine",
    )(src, seg_cnt, seg_off, *args)


def _moe(h, loc, pw, seg, totals, x, g_f, layer, w_gate, b_gate, w_up, b_up, w_down, b_down, final):
    B, S, D = x.shape
    N = B * S
    E = N_EXPERTS
    tm = EXPERT_TILE
    n_tok_tiles = N // ROUTE_TILE
    max_rows = N * TOP_K + n_tok_tiles * E * (SEG_ALIGN - 1)
    n_tiles = -(-max_rows // tm) + E
    tot = totals[0, :E].astype(jnp.int32)
    tiles_per = (tot + tm - 1) // tm
    tile_end = jnp.cumsum(tiles_per)
    row_start = (tile_end - tiles_per) * tm
    nact = tile_end[-1:]
    jj = jnp.minimum(jnp.arange(n_tiles, dtype=jnp.int32), nact[0] - 1)
    tile_e = jnp.sum((jj[:, None] >= tile_end[None, :]).astype(jnp.int32), axis=1)
    place = (row_start[None, :] + seg[:, 0, :E]).reshape(-1)
    seg_cnt = seg[:, 1, :E].reshape(-1)
    seg_off = seg[:, 2, :E].reshape(-1)
    loc2 = loc.reshape(N, LANES)
    xs = _dispatch(h.reshape(N, D), loc2, place, seg_cnt, seg_off, row_start + tot,
                   tiles_per * tm - tot, n_tiles * tm)
    y = _experts(xs, tile_e, jj, nact, layer, w_gate, b_gate, w_up, b_up, w_down, b_down)
    return _combine(y, loc, pw, place, seg_cnt, seg_off, x, g_f, final)


def _moba_proj_kernel(x_ref, sha_ref, sca_ref, ga_ref, shk_ref, sck_ref, gk_ref, wq_ref, wkv_ref,
                      cos_ref, sin_ref, q_ref, k_ref, v_ref, km_ref):
    x = x_ref[...]
    xn = x * _rms_scale(x)
    h = ((xn * ga_ref[...]) * (1.0 + sca_ref[...]) + sha_ref[...]).astype(BF16)
    hkv = ((xn * gk_ref[...]) * (1.0 + sck_ref[...]) + shk_ref[...]).astype(BF16)
    q = jnp.dot(h, wq_ref[...], preferred_element_type=F32)
    kv = jnp.dot(hkv, wkv_ref[...], preferred_element_type=F32)
    cos = cos_ref[...]
    sin = sin_ref[...]
    scale = LOG2_E * MOBA_HEAD_DIM ** -0.5
    hd = MOBA_HEAD_DIM
    t = x.shape[0]
    width = MOBA_HEADS * hd
    v_ref[...] = kv[:, width:].astype(BF16)
    for hh in range(MOBA_HEADS):
        hs = slice(hh * hd, (hh + 1) * hd)
        qh = q[:, hs]
        kh = kv[:, hs]
        q_ref[:, hs] = ((qh * cos + pltpu.roll(qh, hd // 2, 1) * sin) * scale).astype(BF16)
        kr = kh * cos + pltpu.roll(kh, hd // 2, 1) * sin
        k_ref[:, hs] = kr.astype(BF16)
        km_ref[:, hs] = jnp.mean(kr.reshape(t // MOBA_BLOCK, MOBA_BLOCK, hd), axis=1)


def _moba_proj(x, sh_a, sc_a, g_a, sh_k, sc_k, g_k, w_q, w_kv, cos, sin):
    B, S, D = x.shape
    t = TOKEN_TILE
    H = MOBA_HEADS
    width = H * MOBA_HEAD_DIM
    per = t // MOBA_BLOCK
    assert S // MOBA_BLOCK <= LANES
    tok = lambda w: pl.BlockSpec((None, t, w), lambda b, i: (b, i, 0))
    vec = lambda w: pl.BlockSpec((None, 1, w), lambda b, i: (b, 0, 0))
    full = lambda a: pl.BlockSpec(a.shape, lambda b, i: (0,) * a.ndim)
    ga2 = g_a.reshape(1, D)
    gk2 = g_k.reshape(1, D)
    wq = w_q.astype(BF16)
    wkv = w_kv.astype(BF16)
    rows = jax.ShapeDtypeStruct((B, S, width), BF16)
    q, k, v, km = pl.pallas_call(
        _moba_proj_kernel,
        out_shape=(rows, rows, rows, jax.ShapeDtypeStruct((B, S // t, per, width), F32)),
        grid=(B, S // t),
        in_specs=[tok(D), vec(D), vec(D), full(ga2), vec(D), vec(D), full(gk2), full(wq), full(wkv),
                  tok(LANES), tok(LANES)],
        out_specs=(tok(width), tok(width), tok(width),
                   pl.BlockSpec((None, None, per, width), lambda b, i: (b, i, 0, 0))),
        name="moba_proj",
    )(x, sh_a, sc_a, ga2, sh_k, sc_k, gk2, wq, wkv, cos, sin)
    return q, k, v, km.reshape(B, S // MOBA_BLOCK, width)


def _split_mod(mod, n):
    return [mod[:, None, j * D_MODEL:(j + 1) * D_MODEL] for j in range(n)]


def kernel(x, c, positions, ada_w, ada_b, norm_attn_g, norm_ffn_g, mla_w_in, mla_q_norm_g, mla_w_uq, mla_kv_norm_g, mla_w_ukv, mla_w_o, kv_ada_w, kv_ada_b, kv_norm_g, moba_w_kv, moba_w_q, moba_w_o, router_w, router_b, w_gate, b_gate, w_up, b_up, w_down, b_down, final_ada_w, final_ada_b, final_norm_g):
    mods = _ada_linear(c, ada_w, ada_b)
    kv_mod = _ada_linear(c, kv_ada_w[None], kv_ada_b[None])[0]
    f_mod = _ada_linear(c, final_ada_w[None], final_ada_b[None])[0]
    cos_a, sin_a, cos_b, sin_b = _rope_tables(positions)
    f_sh, f_sc = _split_mod(f_mod, 2)
    shared = None
    for layer in range(DEPTH):
        sh_a, sc_a, g_a, sh_f, sc_f, g_f = _split_mod(mods[layer], 6)
        if layer < N_A:
            q, k, v = _mla_proj(x, sh_a, sc_a, norm_attn_g[layer], mla_w_in[layer], mla_q_norm_g[layer],
                                mla_w_uq[layer], mla_kv_norm_g[layer], mla_w_ukv[layer], cos_a, sin_a)
            attn = _attention(q, k, v, None, heads=MLA_HEADS, dq=ATTN_WIDTH, dv=MLA_V, moba=False)
            w_o = mla_w_o[layer]
        else:
            j = layer - N_A
            kv_sh, kv_sc = _split_mod(kv_mod, 2)
            q, k, v, km = _moba_proj(x, sh_a, sc_a, norm_attn_g[layer], kv_sh, kv_sc, kv_norm_g,
                                     moba_w_q[j], moba_w_kv, cos_b, sin_b)
            if shared is None:
                shared = (k, v, km)
            attn = _attention(q, shared[0], shared[1], shared[2], heads=MOBA_HEADS, dq=MOBA_HEAD_DIM,
                              dv=MOBA_HEAD_DIM, moba=True)
            w_o = moba_w_o[j]
        x, h, loc, pw, seg, totals = _attn_out(attn, w_o, x, g_a, norm_ffn_g[layer], sh_f, sc_f,
                                               router_w[layer], router_b[layer])
        final = (final_norm_g, f_sh, f_sc) if layer == DEPTH - 1 else None
        x = _moe(h, loc, pw, seg, totals, x, g_f, layer, w_gate, b_gate, w_up, b_up, w_down, b_down,
                 final)
    return x
```

```python
import functools

import jax
import jax.numpy as jnp
from jax import lax
from jax.experimental import pallas as pl
from jax.experimental.pallas import tpu as pltpu

D_MODEL = 1024
DEPTH = 2
N_A = DEPTH // 2

MLA_HEADS = 8
MLA_Q_LORA = 256
MLA_KV_LORA = 128
MLA_NOPE = 128
MLA_ROPE = 64
MLA_V = 128

MOBA_HEADS = 8
MOBA_HEAD_DIM = D_MODEL // MOBA_HEADS
MOBA_BLOCK = 256
MOBA_TOPK = 3

N_EXPERTS = 32
TOP_K = 4
SWIGLU_LIMIT = 7.0
SWIGLU_ALPHA = 1.702

ROPE_THETA = 10000.0
NORM_EPS = 1e-6
NEG_INF = -1e30
LOG2_E = 1.4426950408889634
SOFTMAX_M_INIT = -1e20

LANES = 128
TOKEN_TILE = 1024
ATTN_Q_TILE = 1024
ATTN_KV_TILE = 512
ATTN_WIDTH = 2 * LANES
ATTN_HEADS_PER_STEP = 2
EXPERT_TILE = 512
SEG_ALIGN = 8
SEG_CHUNK = 64
ROUTE_TILE = 512
SEG_BUF_ROWS = ROUTE_TILE * TOP_K + N_EXPERTS * SEG_ALIGN
VMEM_LIMIT_BYTES = 56 * 1024 * 1024

F32 = jnp.float32
BF16 = jnp.bfloat16


def _rms_scale(x):
    return lax.rsqrt(jnp.mean(x * x, axis=-1, keepdims=True) + NORM_EPS)


def _nt_dot(a, b):
    return lax.dot_general(a, b, (((1,), (1,)), ((), ())), preferred_element_type=F32)


def _ones_column(rows):
    lane = lax.broadcasted_iota(jnp.int32, (rows, LANES), 1)
    return jnp.where(lane == 0, 1.0, 0.0).astype(BF16)


def _ada_kernel(c_ref, w_ref, b_ref, o_ref):
    c = c_ref[...]
    ca = c * jax.nn.sigmoid(c)
    o_ref[...] = jnp.dot(ca, w_ref[...], precision=lax.Precision.HIGHEST,
                         preferred_element_type=F32) + b_ref[...]


def _ada_linear(c, w, b):
    L, D, M = w.shape
    B = c.shape[0]
    bn = 1024
    return pl.pallas_call(
        _ada_kernel,
        out_shape=jax.ShapeDtypeStruct((L, B, M), F32),
        grid=(L, M // bn),
        in_specs=[
            pl.BlockSpec((B, D), lambda l, j: (0, 0)),
            pl.BlockSpec((None, D, bn), lambda l, j: (l, 0, j)),
            pl.BlockSpec((None, 1, bn), lambda l, j: (l, 0, j)),
        ],
        out_specs=pl.BlockSpec((None, B, bn), lambda l, j: (l, 0, j)),
        name="ada_linear",
    )(c, w, b.reshape(L, 1, M))


def _rope_table_kernel(pos_ref, inv_ref, sign_ref, ca_ref, sa_ref, cb_ref, sb_ref):
    ang = pos_ref[...] * inv_ref[...]
    lane = lax.broadcasted_iota(jnp.int32, ang.shape, 1)
    na, nb = MLA_ROPE // 2, MOBA_HEAD_DIM // 2

    def spread(v):
        a = jnp.where(lane < na, v, 0.0)
        a = a + pltpu.roll(a, na, 1)
        a = a + pltpu.roll(a, 2 * na, 1)
        b = jnp.where((lane >= na) & (lane < na + nb), v, 0.0)
        b = pltpu.roll(b, LANES - na, 1)
        b = b + pltpu.roll(b, nb, 1)
        return a, b

    ca_ref[...], cb_ref[...] = spread(jnp.cos(ang))
    sa, sb = spread(jnp.sin(ang))
    sa_ref[...] = sa
    sb_ref[...] = sb * sign_ref[...]


def _rope_tables(positions):
    B, S = positions.shape
    t = 1024
    assert MLA_ROPE // 2 + MOBA_HEAD_DIM // 2 <= LANES and MLA_ROPE * 2 == LANES == MOBA_HEAD_DIM

    def inv_freq(d):
        half = d // 2
        return ROPE_THETA ** (-jnp.arange(half, dtype=F32) * (2.0 / d))

    inv = jnp.concatenate([inv_freq(MLA_ROPE), inv_freq(MOBA_HEAD_DIM),
                           jnp.zeros((LANES - MLA_ROPE // 2 - MOBA_HEAD_DIM // 2,), F32)]).reshape(1, LANES)
    half = MOBA_HEAD_DIM // 2
    sign = jnp.concatenate([-jnp.ones((half,), F32), jnp.ones((half,), F32)]).reshape(1, LANES)
    pos = positions.astype(F32).reshape(B, S, 1)
    row = pl.BlockSpec((1, LANES), lambda b, i: (0, 0))
    out = pl.BlockSpec((None, t, LANES), lambda b, i: (b, i, 0))
    shp = jax.ShapeDtypeStruct((B, S, LANES), F32)
    return pl.pallas_call(
        _rope_table_kernel,
        out_shape=(shp, shp, shp, shp),
        grid=(B, S // t),
        in_specs=[pl.BlockSpec((None, t, 1), lambda b, i: (b, i, 0)), row, row],
        out_specs=(out, out, out, out),
        name="rope_tables",
    )(pos, inv, sign)


def _mla_proj_kernel(x_ref, sh_ref, sc_ref, g_ref, win_ref, qg_ref, wqn_ref, wqr_ref, wqrr_ref,
                     kvg_ref, wkn_ref, wv_ref, cos_ref, sin_ref, q_ref, k_ref, v_ref):
    x = x_ref[...]
    y = x * _rms_scale(x) * g_ref[...]
    h = (y * (1.0 + sc_ref[...]) + sh_ref[...]).astype(BF16)
    proj = jnp.dot(h, win_ref[...], preferred_element_type=F32)
    c_q = proj[:, :MLA_Q_LORA]
    c_kv = proj[:, MLA_Q_LORA:MLA_Q_LORA + MLA_KV_LORA]
    kr_a = proj[:, 384:512]
    kr_b = proj[:, 512:640]
    cq = (c_q * _rms_scale(c_q) * qg_ref[...]).astype(BF16)
    ckv = (c_kv * _rms_scale(c_kv) * kvg_ref[...]).astype(BF16)
    cos = cos_ref[...]
    sin = sin_ref[...]
    scale = LOG2_E * (MLA_NOPE + MLA_ROPE) ** -0.5
    kr = (kr_a * cos + kr_b * sin).astype(BF16)
    q_nope = jnp.dot(cq, wqn_ref[...], preferred_element_type=F32) * scale
    q_ra = jnp.dot(cq, wqr_ref[...], preferred_element_type=F32)
    q_rb = jnp.dot(cq, wqrr_ref[...], preferred_element_type=F32)
    k_nope = jnp.dot(ckv, wkn_ref[...], preferred_element_type=F32)
    v_ref[...] = jnp.dot(ckv, wv_ref[...], preferred_element_type=F32).astype(BF16)
    for hh in range(MLA_HEADS):
        hs = slice(hh * LANES, (hh + 1) * LANES)
        lo = slice(hh * ATTN_WIDTH, hh * ATTN_WIDTH + LANES)
        hi = slice(hh * ATTN_WIDTH + LANES, (hh + 1) * ATTN_WIDTH)
        q_ref[:, lo] = q_nope[:, hs].astype(BF16)
        q_ref[:, hi] = ((q_ra[:, hs] * cos + q_rb[:, hs] * sin) * scale).astype(BF16)
        k_ref[:, lo] = k_nope[:, hs].astype(BF16)
        k_ref[:, hi] = kr


def _rot_half_cols(w, half):
    return jnp.concatenate([-w[..., half:], w[..., :half]], axis=-1)


def _mla_proj(x, sh, sc, g, w_in, q_norm_g, w_uq, kv_norm_g, w_ukv, cos, sin):
    B, S, D = x.shape
    H = MLA_HEADS
    t = TOKEN_TILE
    half = MLA_ROPE // 2
    w_kr = w_in[:, MLA_Q_LORA + MLA_KV_LORA:]
    zpad = jnp.zeros((D, LANES - MLA_ROPE), F32)
    w_in_ext = jnp.concatenate(
        [w_in[:, :MLA_Q_LORA + MLA_KV_LORA], w_kr, zpad, _rot_half_cols(w_kr, half), zpad],
        axis=1).astype(BF16)
    wq = w_uq.reshape(MLA_Q_LORA, H, MLA_NOPE + MLA_ROPE)
    wq_nope = wq[..., :MLA_NOPE].reshape(MLA_Q_LORA, H * MLA_NOPE).astype(BF16)
    wq_r = wq[..., MLA_NOPE:]
    pad = ((0, 0), (0, 0), (0, LANES - MLA_ROPE))
    wq_rope = jnp.pad(wq_r, pad).reshape(MLA_Q_LORA, H * LANES).astype(BF16)
    wq_rope_rot = jnp.pad(_rot_half_cols(wq_r, half), pad).reshape(MLA_Q_LORA, H * LANES).astype(BF16)
    wkv = w_ukv.reshape(MLA_KV_LORA, H, MLA_NOPE + MLA_V)
    wk_nope = wkv[..., :MLA_NOPE].reshape(MLA_KV_LORA, H * MLA_NOPE).astype(BF16)
    wv = wkv[..., MLA_NOPE:].reshape(MLA_KV_LORA, H * MLA_V).astype(BF16)

    tok = lambda w: pl.BlockSpec((None, t, w), lambda b, i: (b, i, 0))
    vec = lambda w: pl.BlockSpec((None, 1, w), lambda b, i: (b, 0, 0))
    full = lambda a: pl.BlockSpec(a.shape, lambda b, i: (0,) * a.ndim)
    g2 = g.reshape(1, D)
    qg2 = q_norm_g.reshape(1, MLA_Q_LORA)
    kvg2 = kv_norm_g.reshape(1, MLA_KV_LORA)
    wide = jax.ShapeDtypeStruct((B, S, H * ATTN_WIDTH), BF16)
    return pl.pallas_call(
        _mla_proj_kernel,
        out_shape=(wide, wide, jax.ShapeDtypeStruct((B, S, H * MLA_V), BF16)),
        grid=(B, S // t),
        in_specs=[tok(D), vec(D), vec(D), full(g2), full(w_in_ext), full(qg2), full(wq_nope),
                  full(wq_rope), full(wq_rope_rot), full(kvg2), full(wk_nope), full(wv),
                  tok(LANES), tok(LANES)],
        out_specs=(tok(H * ATTN_WIDTH), tok(H * ATTN_WIDTH), tok(H * MLA_V)),
        name="mla_proj",
    )(x, sh, sc, g2, w_in_ext, qg2, wq_nope, wq_rope, wq_rope_rot, kvg2, wk_nope, wv, cos, sin)


def _attn_kernel(*refs, moba, tq, bk, dq, dv, nh):
    n_in = 4 if moba else 3
    q_ref, k_ref, v_ref = refs[:3]
    o_ref = refs[n_in]
    scratch = refs[n_in + 1:]
    if moba:
        kx_ref, vx_ref = scratch[:2]
        scratch = scratch[2:]
    else:
        kx_ref, vx_ref = k_ref, scratch[0]
        scratch = scratch[1:]
    per_head = len(scratch) // nh
    w = ATTN_WIDTH
    i = pl.program_id(2)
    seq = vx_ref.shape[0]
    row_id = lax.broadcasted_iota(jnp.int32, (tq, 1), 0)
    col_id = lax.broadcasted_iota(jnp.int32, (1, bk), 1)

    @pl.when((pl.program_id(0) == 0) & (pl.program_id(1) == 0) & (i == 0))
    def _():
        ones = _ones_column(seq)
        if moba:
            key_row = lax.broadcasted_iota(jnp.int32, (seq, LANES), 0)
            lane = lax.broadcasted_iota(jnp.int32, (seq, LANES), 1)
            block_id = jnp.where(key_row // MOBA_BLOCK == lane, 1.0, 0.0).astype(BF16)
        for hh in range(nh):
            vx_ref[:, hh * w + dv:(hh + 1) * w] = ones
            if moba:
                kx_ref[:, hh * w + dq:(hh + 1) * w] = block_id

    @pl.when(i == 0)
    def _():
        for hh in range(nh):
            vx_ref[:, hh * w:hh * w + dv] = v_ref[:, hh * dv:(hh + 1) * dv]
            if moba:
                kx_ref[:, hh * w:hh * w + dq] = k_ref[:, hh * dq:(hh + 1) * dq]

    def head_scratch(hh):
        sc = scratch[hh * per_head:(hh + 1) * per_head]
        return sc if moba else (None,) + tuple(sc)

    def prepare(hh):
        qx_ref, s_a, s_b, p_a, p_b, al_a, al_b, m_sc, acc_sc = head_scratch(hh)
        if moba:
            km_ref = refs[3]
            q = q_ref[:, hh * dq:(hh + 1) * dq]
            nb = km_ref.shape[0]
            gate = _nt_dot(km_ref[:, hh * dq:(hh + 1) * dq].astype(BF16), q)
            blk = lax.broadcasted_iota(jnp.int32, (nb, tq), 0)
            blk_f = blk.astype(F32)
            own = (i * tq + lax.broadcasted_iota(jnp.int32, (1, tq), 1)) // MOBA_BLOCK
            gate = jnp.where(blk < own, gate, -jnp.inf)
            sel = jnp.where(blk == own, 1.0, 0.0)
            for _ in range(MOBA_TOPK):
                mx = jnp.max(gate, axis=0, keepdims=True)
                first = jnp.min(jnp.where(gate == mx, blk_f, float(nb)), axis=0, keepdims=True)
                hit = blk_f == first
                sel = jnp.where(hit & (mx > -jnp.inf), 1.0, sel)
                gate = jnp.where(hit, -jnp.inf, gate)
            bias = jnp.where(sel > 0.0, 0.0, NEG_INF)
            bias = jnp.concatenate([bias, jnp.zeros((LANES - nb, tq), F32)], axis=0)
            qx_ref[:, :LANES] = q
            qx_ref[:, LANES:] = bias.T.astype(BF16)
        m_sc[...] = jnp.full(m_sc.shape, SOFTMAX_M_INIT, F32)
        acc_sc[...] = jnp.zeros(acc_sc.shape, F32)
        p_b[...] = jnp.zeros(p_b.shape, BF16)
        al_b[...] = jnp.ones(al_b.shape, F32)

    def scores(hh, j, slot, rows=slice(None)):
        sc = head_scratch(hh)
        qx = sc[0][rows, :] if moba else q_ref[rows, hh * w:(hh + 1) * w]
        start = pl.multiple_of(j * bk, bk)
        sc[1 + slot][rows, :] = _nt_dot(qx, kx_ref[pl.ds(start, bk), hh * w:(hh + 1) * w])

    def soft(hh, slot, diag_offset=None, rows=slice(None)):
        sc = head_scratch(hh)
        s_src, p_dst, al_dst, m_sc = sc[1 + slot], sc[3 + slot], sc[5 + slot], sc[7]
        s = s_src[rows, :]
        if diag_offset is not None:
            s = jnp.where(col_id + diag_offset <= row_id[rows, :], s, -jnp.inf)
        m_prev = m_sc[rows, :]
        m_new = jnp.maximum(m_prev, jnp.max(s, axis=1, keepdims=True))
        p_dst[rows, :] = jnp.exp2(s - m_new).astype(BF16)
        al_dst[rows, :] = jnp.exp2(m_prev - m_new)
        m_sc[rows, :] = m_new

    def fold(hh, j, slot, rows=slice(None)):
        sc = head_scratch(hh)
        p_src, al_src, acc_sc = sc[3 + slot], sc[5 + slot], sc[8]
        start = pl.multiple_of(j * bk, bk)
        acc_sc[rows, :] = al_src[rows, :] * acc_sc[rows, :] + jnp.dot(
            p_src[rows, :], vx_ref[pl.ds(start, bk), hh * w:(hh + 1) * w], preferred_element_type=F32)

    hs = range(nh)
    for hh in hs:
        prepare(hh)
    for hh in hs:
        scores(hh, 0, 0)

    def body(t, carry):
        j = 2 * t
        for hh in hs:
            scores(hh, j + 1, 1)
        for hh in hs:
            soft(hh, 0)
        for hh in hs:
            fold(hh, jnp.maximum(j - 1, 0), 1)
        for hh in hs:
            scores(hh, j + 2, 0)
        for hh in hs:
            soft(hh, 1)
        for hh in hs:
            fold(hh, j, 0)
        return carry

    lax.fori_loop(0, i, body, 0)
    j = 2 * i
    late = slice(bk, tq)
    for hh in hs:
        scores(hh, j + 1, 1, late)
    for hh in hs:
        soft(hh, 0, diag_offset=0)
    for hh in hs:
        fold(hh, jnp.maximum(j - 1, 0), 1)
    for hh in hs:
        soft(hh, 1, diag_offset=bk, rows=late)
    for hh in hs:
        fold(hh, j, 0)
    for hh in hs:
        fold(hh, j + 1, 1, late)
    for hh in hs:
        acc = head_scratch(hh)[8][...]
        o_ref[:, hh * dv:(hh + 1) * dv] = (acc[:, :dv] / acc[:, dv:dv + 1]).astype(o_ref.dtype)


def _attention(q, k, v, k_mean, *, heads, dq, dv, moba):
    B, S, _ = q.shape
    tq, bk, w, nh = ATTN_Q_TILE, ATTN_KV_TILE, ATTN_WIDTH, ATTN_HEADS_PER_STEP
    assert tq == 2 * bk and tq % MOBA_BLOCK == 0 and S % tq == 0 and heads % nh == 0
    in_specs = [
        pl.BlockSpec((None, tq, nh * dq), lambda b, h, i: (b, i, h)),
        pl.BlockSpec((None, S, nh * (dq if moba else w)), lambda b, h, i: (b, 0, h)),
        pl.BlockSpec((None, S, nh * dv), lambda b, h, i: (b, 0, h)),
    ]
    args = [q, k, v]
    scratch = [pltpu.VMEM((S, nh * w), BF16)]
    if moba:
        nb = k_mean.shape[1]
        in_specs.append(pl.BlockSpec((None, nb, nh * dq), lambda b, h, i: (b, 0, h)))
        args.append(k_mean)
        scratch.insert(0, pltpu.VMEM((S, nh * w), BF16))
    for _ in range(nh):
        if moba:
            scratch.append(pltpu.VMEM((tq, w), BF16))
        scratch += [pltpu.VMEM((tq, bk), F32), pltpu.VMEM((tq, bk), F32),
                    pltpu.VMEM((tq, bk), BF16), pltpu.VMEM((tq, bk), BF16),
                    pltpu.VMEM((tq, 1), F32), pltpu.VMEM((tq, 1), F32),
                    pltpu.VMEM((tq, 1), F32), pltpu.VMEM((tq, w), F32)]
    return pl.pallas_call(
        functools.partial(_attn_kernel, moba=moba, tq=tq, bk=bk, dq=dq, dv=dv, nh=nh),
        out_shape=jax.ShapeDtypeStruct((B, S, heads * dv), BF16),
        grid=(B, heads // nh, S // tq),
        in_specs=in_specs,
        out_specs=pl.BlockSpec((None, tq, nh * dv), lambda b, h, i: (b, i, h)),
        scratch_shapes=scratch,
        compiler_params=pltpu.CompilerParams(
            dimension_semantics=("arbitrary", "arbitrary", "arbitrary"),
            vmem_limit_bytes=VMEM_LIMIT_BYTES),
        name="moba_attention" if moba else "mla_attention",
    )(*args)


def _attn_out_kernel(a_ref, wo_ref, x_ref, ga_ref, g_ref, sh_ref, sc_ref, rw_ref, rb_ref,
                     x1_ref, h_ref, loc_ref, pw_ref, seg_ref, tot_ref, run_ref):
    first_step = (pl.program_id(0) == 0) & (pl.program_id(1) == 0)

    @pl.when(first_step)
    def _():
        run_ref[...] = jnp.zeros(run_ref.shape, F32)

    a = jnp.dot(a_ref[...], wo_ref[...], preferred_element_type=F32)
    x1 = x_ref[...] + ga_ref[...] * a
    x1_ref[...] = x1
    y = x1 * _rms_scale(x1) * g_ref[...]
    h = (y * (1.0 + sc_ref[...]) + sh_ref[...]).astype(BF16)
    h_ref[...] = h
    logits = jnp.dot(h, rw_ref[...], preferred_element_type=F32) + rb_ref[...]
    t = logits.shape[0]
    lane_i = lax.broadcasted_iota(jnp.int32, logits.shape, 1)
    lane = lane_i.astype(F32)
    logits = jnp.where(lane_i < N_EXPERTS, logits, -jnp.inf)
    top = None
    den = jnp.zeros((t, 1), F32)
    sel = jnp.zeros(logits.shape, F32)
    hits, ws = [], []
    for _ in range(TOP_K):
        mx = jnp.max(logits, axis=1, keepdims=True)
        first = jnp.min(jnp.where(logits == mx, lane, float(LANES)), axis=1, keepdims=True)
        hit = lane == first
        if top is None:
            top = mx
        w = jnp.exp(mx - top)
        den = den + w
        sel = jnp.where(hit, 1.0, sel)
        logits = jnp.where(hit, -jnp.inf, logits)
        hits.append(hit)
        ws.append(w)
    rt = ROUTE_TILE
    r_id = lax.broadcasted_iota(jnp.int32, (rt, rt), 0)
    c_id = lax.broadcasted_iota(jnp.int32, (rt, rt), 1)
    before = jnp.where(c_id < r_id, 1.0, 0.0).astype(BF16)
    a_id = lax.broadcasted_iota(jnp.int32, (LANES, LANES), 0)
    b_id = lax.broadcasted_iota(jnp.int32, (LANES, LANES), 1)
    earlier = jnp.where(a_id < b_id, 1.0, 0.0).astype(BF16)
    row8 = lax.broadcasted_iota(jnp.int32, (8, LANES), 0)
    local = []
    for sub in range(t // rt):
        sel_s = sel[sub * rt:(sub + 1) * rt]
        prefix = jnp.dot(before, sel_s.astype(BF16), preferred_element_type=F32)
        cnt = jnp.sum(sel_s, axis=0, keepdims=True)
        units = jnp.floor((cnt + (SEG_ALIGN - 1.0)) * (1.0 / SEG_ALIGN))
        cnt_pad = units * SEG_ALIGN
        seg_off = jnp.dot(jnp.broadcast_to(units, (8, LANES)).astype(BF16), earlier,
                          preferred_element_type=F32)[0:1] * SEG_ALIGN
        base = run_ref[...]
        run_ref[...] = base + cnt_pad
        seg = jnp.where(row8 == 0, base, jnp.where(row8 == 1, cnt_pad, jnp.where(row8 == 2, seg_off, 0.0)))
        seg_ref[sub] = seg.astype(jnp.int32)
        local.append(prefix + seg_off)
    tot_ref[...] = run_ref[...]
    local = jnp.concatenate(local, axis=0)
    loc = jnp.zeros(logits.shape, F32)
    pw = jnp.zeros(logits.shape, F32)
    for kk in range(TOP_K):
        row = jnp.sum(jnp.where(hits[kk], local, 0.0), axis=1, keepdims=True)
        loc = jnp.where(lane_i == kk, row, loc)
        pw = jnp.where(lane_i == kk, ws[kk] / den, pw)
    loc_ref[...] = loc.astype(jnp.int32)
    pw_ref[...] = pw


def _attn_out(attn, w_o, x, g_a, norm_g, sh_f, sc_f, router_w, router_b):
    B, S, D = x.shape
    t = TOKEN_TILE
    per = S // t
    sub = t // ROUTE_TILE
    rw = jnp.pad(router_w, ((0, 0), (0, LANES - N_EXPERTS))).astype(BF16)
    rb = jnp.pad(router_b, (0, LANES - N_EXPERTS)).reshape(1, LANES)
    wo = w_o.astype(BF16)
    g2 = norm_g.reshape(1, D)
    tok = lambda w: pl.BlockSpec((None, t, w), lambda b, i: (b, i, 0))
    vec = lambda w: pl.BlockSpec((None, 1, w), lambda b, i: (b, 0, 0))
    full = lambda a: pl.BlockSpec(a.shape, lambda b, i: (0,) * a.ndim)
    return pl.pallas_call(
        _attn_out_kernel,
        out_shape=(jax.ShapeDtypeStruct((B, S, D), F32),
                   jax.ShapeDtypeStruct((B, S, D), BF16),
                   jax.ShapeDtypeStruct((B, S, LANES), jnp.int32),
                   jax.ShapeDtypeStruct((B, S, LANES), F32),
                   jax.ShapeDtypeStruct((B * per * sub, 8, LANES), jnp.int32),
                   jax.ShapeDtypeStruct((1, LANES), F32)),
        grid=(B, per),
        in_specs=[tok(attn.shape[-1]), full(wo), tok(D), vec(D), full(g2), vec(D), vec(D),
                  full(rw), full(rb)],
        out_specs=(tok(D), tok(D), tok(LANES), tok(LANES),
                   pl.BlockSpec((sub, 8, LANES), lambda b, i: (b * per + i, 0, 0)),
                   pl.BlockSpec((1, LANES), lambda b, i: (0, 0))),
        scratch_shapes=[pltpu.VMEM((1, LANES), F32)],
        compiler_params=pltpu.CompilerParams(dimension_semantics=("arbitrary", "arbitrary")),
        name="attn_out_router",
    )(attn, wo, x, g_a, g2, sh_f, sc_f, rw, rb)


def _for_each_chunk(cnt, fn):
    shift = SEG_CHUNK.bit_length() - 1
    whole = cnt >> shift

    def piece(c, carry):
        fn(pl.multiple_of(c * SEG_CHUNK, SEG_CHUNK), SEG_CHUNK)
        return carry

    lax.fori_loop(0, whole, piece, 0)
    rows = SEG_CHUNK // 2
    while rows >= SEG_ALIGN:
        shift = rows.bit_length()
        done = (cnt >> shift) << shift

        @pl.when((cnt & rows) != 0)
        def _(done=done, rows=rows):
            fn(pl.multiple_of(done, SEG_ALIGN), rows)

        rows //= 2


def _dispatch_kernel(dst_ref, cnt_ref, off_ref, pad_start_ref, pad_cnt_ref, h_ref, loc_ref,
                     xs_ref, seg_buf, zero_buf, sems):
    i = pl.program_id(0)
    last = pl.num_programs(0) - 1
    slot = lax.rem(i, 2)
    t = h_ref.shape[0]
    n_loc = seg_buf.shape[1]
    pad_sem = sems.at[2]

    @pl.when(i == 0)
    def _():
        zero_buf[...] = jnp.zeros(zero_buf.shape, F32)
        for wait in (False, True):
            def per_expert(e, carry, wait=wait):
                start = pad_start_ref[e]

                def chunk(done, rows):
                    cp = pltpu.make_async_copy(
                        zero_buf.at[pl.ds(0, rows), :],
                        xs_ref.at[pl.ds(pl.multiple_of(start + done, SEG_ALIGN), rows), :], pad_sem)
                    cp.wait() if wait else cp.start()

                _for_each_chunk(pad_cnt_ref[e], chunk)
                return carry

            lax.fori_loop(0, N_EXPERTS, per_expert, 0)

    def segment_copies(tile, buf, wait):
        def per_expert(e, carry):
            idx = tile * N_EXPERTS + e
            off = off_ref[idx]
            dst = dst_ref[idx]

            def chunk(done, rows):
                cp = pltpu.make_async_copy(
                    seg_buf.at[buf, pl.ds(pl.multiple_of(off + done, SEG_ALIGN), rows), :],
                    xs_ref.at[pl.ds(pl.multiple_of(dst + done, SEG_ALIGN), rows), :], sems.at[buf])
                cp.wait() if wait else cp.start()

            _for_each_chunk(cnt_ref[idx], chunk)
            return carry

        lax.fori_loop(0, N_EXPERTS, per_expert, 0)

    loc_t = loc_ref[...].astype(F32).T
    l_id = lax.broadcasted_iota(jnp.int32, (n_loc, t), 0).astype(F32)
    onehot = jnp.zeros((n_loc, t), F32)
    for kk in range(TOP_K):
        onehot = jnp.where(l_id == loc_t[kk:kk + 1, :], 1.0, onehot)
    seg_buf[slot] = jnp.dot(onehot.astype(BF16), h_ref[...], preferred_element_type=F32)
    segment_copies(i, slot, wait=False)

    @pl.when(i > 0)
    def _():
        segment_copies(i - 1, 1 - slot, wait=True)

    @pl.when(i == last)
    def _():
        segment_copies(i, slot, wait=True)


def _dispatch(h, loc, dst, seg_cnt, seg_off, pad_start, pad_cnt, n_rows):
    N, D = h.shape
    t = ROUTE_TILE
    return pl.pallas_call(
        _dispatch_kernel,
        out_shape=jax.ShapeDtypeStruct((n_rows, D), F32),
        grid_spec=pltpu.PrefetchScalarGridSpec(
            num_scalar_prefetch=5,
            grid=(N // t,),
            in_specs=[pl.BlockSpec((t, D), lambda i, *_: (i, 0)),
                      pl.BlockSpec((t, LANES), lambda i, *_: (i, 0))],
            out_specs=pl.BlockSpec(memory_space=pl.ANY),
            scratch_shapes=[pltpu.VMEM((2, SEG_BUF_ROWS, D), F32),
                            pltpu.VMEM((SEG_CHUNK, D), F32),
                            pltpu.SemaphoreType.DMA((3,))],
        ),
        compiler_params=pltpu.CompilerParams(dimension_semantics=("arbitrary",),
                                             vmem_limit_bytes=VMEM_LIMIT_BYTES),
        name="moe_dispatch",
    )(dst, seg_cnt, seg_off, pad_start, pad_cnt, h, loc)


def _experts_kernel(tile_e_ref, tile_blk_ref, nact_ref, xs_ref, wg_ref, bg_ref, wu_ref, bu_ref,
                    wd_ref, bd_ref, y_ref, wg_bf, wu_bf, wd_bf):
    j = pl.program_id(0)

    @pl.when(j < nact_ref[0])
    def _():
        prev = tile_e_ref[jnp.maximum(j - 1, 0)]

        @pl.when((j == 0) | (tile_e_ref[j] != prev))
        def _():
            wg_bf[...] = wg_ref[...].astype(BF16)
            wu_bf[...] = wu_ref[...].astype(BF16)
            wd_bf[...] = wd_ref[...].astype(BF16)

        x = xs_ref[...].astype(BF16)
        g = jnp.minimum(jnp.dot(x, wg_bf[...], preferred_element_type=F32) + bg_ref[...], SWIGLU_LIMIT)
        u = jnp.clip(jnp.dot(x, wu_bf[...], preferred_element_type=F32) + bu_ref[...],
                     -SWIGLU_LIMIT, SWIGLU_LIMIT)
        a = g * jax.nn.sigmoid(SWIGLU_ALPHA * g) * (u + 1.0)
        y_ref[...] = jnp.dot(a.astype(BF16), wd_bf[...], preferred_element_type=F32) + bd_ref[...]


def _experts(xs, tile_e, tile_blk, nact, layer, w_gate, b_gate, w_up, b_up, w_down, b_down):
    P, D = xs.shape
    L, E, _, F = w_gate.shape
    tm = EXPERT_TILE
    rows = pl.BlockSpec((tm, D), lambda j, te, tb, na: (tb[j], 0))
    wspec = lambda r, c: pl.BlockSpec((None, None, r, c), lambda j, te, tb, na: (layer, te[j], 0, 0))
    return pl.pallas_call(
        _experts_kernel,
        out_shape=jax.ShapeDtypeStruct((P, D), F32),
        grid_spec=pltpu.PrefetchScalarGridSpec(
            num_scalar_prefetch=3,
            grid=(P // tm,),
            in_specs=[rows, wspec(D, F), wspec(1, F), wspec(D, F), wspec(1, F), wspec(F, D), wspec(1, D)],
            out_specs=rows,
            scratch_shapes=[pltpu.VMEM((D, F), BF16), pltpu.VMEM((D, F), BF16), pltpu.VMEM((F, D), BF16)],
        ),
        compiler_params=pltpu.CompilerParams(dimension_semantics=("arbitrary",),
                                             vmem_limit_bytes=VMEM_LIMIT_BYTES),
        name="moe_experts",
    )(tile_e, tile_blk, nact, xs, w_gate, b_gate.reshape(L, E, 1, F), w_up, b_up.reshape(L, E, 1, F),
      w_down, b_down.reshape(L, E, 1, D))


def _combine_kernel(*refs, final_norm):
    if final_norm:
        (src_ref, cnt_ref, off_ref, y_ref, loc_ref, pw_ref, x_ref, gf_ref, fg_ref, fsh_ref, fsc_ref,
         o_ref, seg_buf, sems) = refs
    else:
        src_ref, cnt_ref, off_ref, y_ref, loc_ref, pw_ref, x_ref, gf_ref, o_ref, seg_buf, sems = refs
    i = pl.program_id(0)
    last = pl.num_programs(0) - 1
    slot = lax.rem(i, 2)
    t = x_ref.shape[0]
    n_loc = seg_buf.shape[1]

    def segment_copies(tile, buf, wait):
        def per_expert(e, carry):
            idx = tile * N_EXPERTS + e
            off = off_ref[idx]
            src = src_ref[idx]

            def chunk(done, rows):
                cp = pltpu.make_async_copy(
                    y_ref.at[pl.ds(pl.multiple_of(src + done, SEG_ALIGN), rows), :],
                    seg_buf.at[buf, pl.ds(pl.multiple_of(off + done, SEG_ALIGN), rows), :], sems.at[buf])
                cp.wait() if wait else cp.start()

            _for_each_chunk(cnt_ref[idx], chunk)
            return carry

        lax.fori_loop(0, N_EXPERTS, per_expert, 0)

    @pl.when(i == 0)
    def _():
        seg_buf[...] = jnp.zeros(seg_buf.shape, F32)
        segment_copies(0, 0, wait=False)

    @pl.when(i < last)
    def _():
        segment_copies(i + 1, 1 - slot, wait=False)

    segment_copies(i, slot, wait=True)
    loc = loc_ref[...].astype(F32)
    pw = pw_ref[...]
    l_id = lax.broadcasted_iota(jnp.int32, (t, n_loc), 1).astype(F32)
    weights = jnp.zeros((t, n_loc), F32)
    for kk in range(TOP_K):
        weights = jnp.where(l_id == loc[:, kk:kk + 1], pw[:, kk:kk + 1], weights)
    f = jnp.dot(weights.astype(BF16), seg_buf[slot].astype(BF16), preferred_element_type=F32)
    x2 = x_ref[...] + gf_ref[...] * f
    if final_norm:
        y = x2 * _rms_scale(x2) * fg_ref[...]
        x2 = y * (1.0 + fsc_ref[...]) + fsh_ref[...]
    o_ref[...] = x2


def _combine(y, loc, pw, src, seg_cnt, seg_off, x, g_f, final):
    B, S, D = x.shape
    t = ROUTE_TILE
    per = S // t
    tok = lambda w: pl.BlockSpec((None, t, w), lambda i, *_: (i // per, i % per, 0))
    vec = pl.BlockSpec((None, 1, D), lambda i, *_: (i // per, 0, 0))
    in_specs = [pl.BlockSpec(memory_space=pl.ANY), tok(LANES), tok(LANES), tok(D), vec]
    args = [y, loc, pw, x, g_f]
    if final is not None:
        fg, fsh, fsc = final
        in_specs += [pl.BlockSpec((1, D), lambda i, *_: (0, 0)), vec, vec]
        args += [fg.reshape(1, D), fsh, fsc]
    return pl.pallas_call(
        functools.partial(_combine_kernel, final_norm=final is not None),
        out_shape=jax.ShapeDtypeStruct((B, S, D), F32),
        grid_spec=pltpu.PrefetchScalarGridSpec(
            num_scalar_prefetch=3,
            grid=(B * per,),
            in_specs=in_specs,
            out_specs=tok(D),
            scratch_shapes=[pltpu.VMEM((2, SEG_BUF_ROWS, D), F32), pltpu.SemaphoreType.DMA((2,))],
        ),
        compiler_params=pltpu.CompilerParams(dimension_semantics=("arbitrary",),
                                             vmem_limit_bytes=VMEM_LIMIT_BYTES),
        name="moe_combine",
    )(src, seg_cnt, seg_off, *args)


def _moe(h, loc, pw, seg, totals, x, g_f, layer, w_gate, b_gate, w_up, b_up, w_down, b_down, final):
    B, S, D = x.shape
    N = B * S
    E = N_EXPERTS
    tm = EXPERT_TILE
    n_tok_tiles = N // ROUTE_TILE
    max_rows = N * TOP_K + n_tok_tiles * E * (SEG_ALIGN - 1)
    n_tiles = -(-max_rows // tm) + E
    tot = totals[0, :E].astype(jnp.int32)
    tiles_per = (tot + tm - 1) // tm
    tile_end = jnp.cumsum(tiles_per)
    row_start = (tile_end - tiles_per) * tm
    nact = tile_end[-1:]
    jj = jnp.minimum(jnp.arange(n_tiles, dtype=jnp.int32), nact[0] - 1)
    tile_e = jnp.sum((jj[:, None] >= tile_end[None, :]).astype(jnp.int32), axis=1)
    place = (row_start[None, :] + seg[:, 0, :E]).reshape(-1)
    seg_cnt = seg[:, 1, :E].reshape(-1)
    seg_off = seg[:, 2, :E].reshape(-1)
    loc2 = loc.reshape(N, LANES)
    xs = _dispatch(h.reshape(N, D), loc2, place, seg_cnt, seg_off, row_start + tot,
                   tiles_per * tm - tot, n_tiles * tm)
    y = _experts(xs, tile_e, jj, nact, layer, w_gate, b_gate, w_up, b_up, w_down, b_down)
    return _combine(y, loc, pw, place, seg_cnt, seg_off, x, g_f, final)


def _moba_proj_kernel(x_ref, sha_ref, sca_ref, ga_ref, shk_ref, sck_ref, gk_ref, wq_ref, wkv_ref,
                      cos_ref, sin_ref, q_ref, k_ref, v_ref, km_ref):
    x = x_ref[...]
    xn = x * _rms_scale(x)
    h = ((xn * ga_ref[...]) * (1.0 + sca_ref[...]) + sha_ref[...]).astype(BF16)
    hkv = ((xn * gk_ref[...]) * (1.0 + sck_ref[...]) + shk_ref[...]).astype(BF16)
    q = jnp.dot(h, wq_ref[...], preferred_element_type=F32)
    kv = jnp.dot(hkv, wkv_ref[...], preferred_element_type=F32)
    cos = cos_ref[...]
    sin = sin_ref[...]
    scale = LOG2_E * MOBA_HEAD_DIM ** -0.5
    hd = MOBA_HEAD_DIM
    t = x.shape[0]
    width = MOBA_HEADS * hd
    v_ref[...] = kv[:, width:].astype(BF16)
    for hh in range(MOBA_HEADS):
        hs = slice(hh * hd, (hh + 1) * hd)
        qh = q[:, hs]
        kh = kv[:, hs]
        q_ref[:, hs] = ((qh * cos + pltpu.roll(qh, hd // 2, 1) * sin) * scale).astype(BF16)
        kr = kh * cos + pltpu.roll(kh, hd // 2, 1) * sin
        k_ref[:, hs] = kr.astype(BF16)
        km_ref[:, hs] = jnp.mean(kr.reshape(t // MOBA_BLOCK, MOBA_BLOCK, hd), axis=1)


def _moba_proj(x, sh_a, sc_a, g_a, sh_k, sc_k, g_k, w_q, w_kv, cos, sin):
    B, S, D = x.shape
    t = TOKEN_TILE
    H = MOBA_HEADS
    width = H * MOBA_HEAD_DIM
    per = t // MOBA_BLOCK
    assert S // MOBA_BLOCK <= LANES
    tok = lambda w: pl.BlockSpec((None, t, w), lambda b, i: (b, i, 0))
    vec = lambda w: pl.BlockSpec((None, 1, w), lambda b, i: (b, 0, 0))
    full = lambda a: pl.BlockSpec(a.shape, lambda b, i: (0,) * a.ndim)
    ga2 = g_a.reshape(1, D)
    gk2 = g_k.reshape(1, D)
    wq = w_q.astype(BF16)
    wkv = w_kv.astype(BF16)
    rows = jax.ShapeDtypeStruct((B, S, width), BF16)
    q, k, v, km = pl.pallas_call(
        _moba_proj_kernel,
        out_shape=(rows, rows, rows, jax.ShapeDtypeStruct((B, S // t, per, width), F32)),
        grid=(B, S // t),
        in_specs=[tok(D), vec(D), vec(D), full(ga2), vec(D), vec(D), full(gk2), full(wq), full(wkv),
                  tok(LANES), tok(LANES)],
        out_specs=(tok(width), tok(width), tok(width),
                   pl.BlockSpec((None, None, per, width), lambda b, i: (b, i, 0, 0))),
        name="moba_proj",
    )(x, sh_a, sc_a, ga2, sh_k, sc_k, gk2, wq, wkv, cos, sin)
    return q, k, v, km.reshape(B, S // MOBA_BLOCK, width)


def _split_mod(mod, n):
    return [mod[:, None, j * D_MODEL:(j + 1) * D_MODEL] for j in range(n)]


def kernel(x, c, positions, ada_w, ada_b, norm_attn_g, norm_ffn_g, mla_w_in, mla_q_norm_g, mla_w_uq, mla_kv_norm_g, mla_w_ukv, mla_w_o, kv_ada_w, kv_ada_b, kv_norm_g, moba_w_kv, moba_w_q, moba_w_o, router_w, router_b, w_gate, b_gate, w_up, b_up, w_down, b_down, final_ada_w, final_ada_b, final_norm_g):
    mods = _ada_linear(c, ada_w, ada_b)
    kv_mod = _ada_linear(c, kv_ada_w[None], kv_ada_b[None])[0]
    f_mod = _ada_linear(c, final_ada_w[None], final_ada_b[None])[0]
    cos_a, sin_a, cos_b, sin_b = _rope_tables(positions)
    f_sh, f_sc = _split_mod(f_mod, 2)
    shared = None
    for layer in range(DEPTH):
        sh_a, sc_a, g_a, sh_f, sc_f, g_f = _split_mod(mods[layer], 6)
        if layer < N_A:
            q, k, v = _mla_proj(x, sh_a, sc_a, norm_attn_g[layer], mla_w_in[layer], mla_q_norm_g[layer],
                                mla_w_uq[layer], mla_kv_norm_g[layer], mla_w_ukv[layer], cos_a, sin_a)
            attn = _attention(q, k, v, None, heads=MLA_HEADS, dq=ATTN_WIDTH, dv=MLA_V, moba=False)
            w_o = mla_w_o[layer]
        else:
            j = layer - N_A
            kv_sh, kv_sc = _split_mod(kv_mod, 2)
            q, k, v, km = _moba_proj(x, sh_a, sc_a, norm_attn_g[layer], kv_sh, kv_sc, kv_norm_g,
                                     moba_w_q[j], moba_w_kv, cos_b, sin_b)
            if shared is None:
                shared = (k, v, km)
            attn = _attention(q, shared[0], shared[1], shared[2], heads=MOBA_HEADS, dq=MOBA_HEAD_DIM,
                              dv=MOBA_HEAD_DIM, moba=True)
            w_o = moba_w_o[j]
        x, h, loc, pw, seg, totals = _attn_out(attn, w_o, x, g_a, norm_ffn_g[layer], sh_f, sc_f,
                                               router_w[layer], router_b[layer])
        final = (final_norm_g, f_sh, f_sc) if layer == DEPTH - 1 else None
        x = _moe(h, loc, pw, seg, totals, x, g_f, layer, w_gate, b_gate, w_up, b_up, w_down, b_down,
                 final)
    return x
```

```python
import functools

import jax
import jax.numpy as jnp
from jax import lax
from jax.experimental import pallas as pl
from jax.experimental.pallas import tpu as pltpu

D_MODEL = 1024
DEPTH = 2
N_A = DEPTH // 2

MLA_HEADS = 8
MLA_Q_LORA = 256
MLA_KV_LORA = 128
MLA_NOPE = 128
MLA_ROPE = 64
MLA_V = 128

MOBA_HEADS = 8
MOBA_HEAD_DIM = D_MODEL // MOBA_HEADS
MOBA_BLOCK = 256
MOBA_TOPK = 3

N_EXPERTS = 32
TOP_K = 4
SWIGLU_LIMIT = 7.0
SWIGLU_ALPHA = 1.702

ROPE_THETA = 10000.0
NORM_EPS = 1e-6
NEG_INF = -1e30
LOG2_E = 1.4426950408889634
SOFTMAX_M_INIT = -1e29

LANES = 128
TOKEN_TILE = 1024
ATTN_Q_TILE = 1024
ATTN_KV_TILE = 512
ATTN_WIDTH = 2 * LANES
ATTN_HEADS_PER_STEP = 2
EXPERT_TILE = 512
SEG_ALIGN = 8
SEG_CHUNK = 64
ROUTE_TILE = 512
SEG_BUF_ROWS = ROUTE_TILE * TOP_K + N_EXPERTS * SEG_ALIGN
VMEM_LIMIT_BYTES = 56 * 1024 * 1024

F32 = jnp.float32
BF16 = jnp.bfloat16


def _rms_scale(x):
    return lax.rsqrt(jnp.mean(x * x, axis=-1, keepdims=True) + NORM_EPS)


def _nt_dot(a, b):
    return lax.dot_general(a, b, (((1,), (1,)), ((), ())), preferred_element_type=F32)


def _ones_column(rows):
    lane = lax.broadcasted_iota(jnp.int32, (rows, LANES), 1)
    return jnp.where(lane == 0, 1.0, 0.0).astype(BF16)


def _ada_kernel(c_ref, w_ref, b_ref, o_ref):
    c = c_ref[...]
    ca = c * jax.nn.sigmoid(c)
    o_ref[...] = jnp.dot(ca, w_ref[...], precision=lax.Precision.HIGHEST,
                         preferred_element_type=F32) + b_ref[...]


def _ada_linear(c, w, b):
    L, D, M = w.shape
    B = c.shape[0]
    bn = 2048
    return pl.pallas_call(
        _ada_kernel,
        out_shape=jax.ShapeDtypeStruct((L, B, M), F32),
        grid=(L, M // bn),
        in_specs=[
            pl.BlockSpec((B, D), lambda l, j: (0, 0)),
            pl.BlockSpec((None, D, bn), lambda l, j: (l, 0, j)),
            pl.BlockSpec((None, 1, bn), lambda l, j: (l, 0, j)),
        ],
        out_specs=pl.BlockSpec((None, B, bn), lambda l, j: (l, 0, j)),
        name="ada_linear",
    )(c, w, b.reshape(L, 1, M))


def _rope_table_kernel(pos_ref, inv_ref, sign_ref, ca_ref, sa_ref, cb_ref, sb_ref):
    ang = pos_ref[...] * inv_ref[...]
    lane = lax.broadcasted_iota(jnp.int32, ang.shape, 1)
    na, nb = MLA_ROPE // 2, MOBA_HEAD_DIM // 2

    def spread(v):
        a = jnp.where(lane < na, v, 0.0)
        a = a + pltpu.roll(a, na, 1)
        a = a + pltpu.roll(a, 2 * na, 1)
        b = jnp.where((lane >= na) & (lane < na + nb), v, 0.0)
        b = pltpu.roll(b, LANES - na, 1)
        b = b + pltpu.roll(b, nb, 1)
        return a, b

    ca_ref[...], cb_ref[...] = spread(jnp.cos(ang))
    sa, sb = spread(jnp.sin(ang))
    sa_ref[...] = sa
    sb_ref[...] = sb * sign_ref[...]


def _rope_tables(positions):
    B, S = positions.shape
    t = 1024
    assert MLA_ROPE // 2 + MOBA_HEAD_DIM // 2 <= LANES and MLA_ROPE * 2 == LANES == MOBA_HEAD_DIM

    def inv_freq(d):
        half = d // 2
        return ROPE_THETA ** (-jnp.arange(half, dtype=F32) * (2.0 / d))

    inv = jnp.concatenate([inv_freq(MLA_ROPE), inv_freq(MOBA_HEAD_DIM),
                           jnp.zeros((LANES - MLA_ROPE // 2 - MOBA_HEAD_DIM // 2,), F32)]).reshape(1, LANES)
    half = MOBA_HEAD_DIM // 2
    sign = jnp.concatenate([-jnp.ones((half,), F32), jnp.ones((half,), F32)]).reshape(1, LANES)
    pos = positions.astype(F32).reshape(B, S, 1)
    row = pl.BlockSpec((1, LANES), lambda b, i: (0, 0))
    out = pl.BlockSpec((None, t, LANES), lambda b, i: (b, i, 0))
    shp = jax.ShapeDtypeStruct((B, S, LANES), F32)
    return pl.pallas_call(
        _rope_table_kernel,
        out_shape=(shp, shp, shp, shp),
        grid=(B, S // t),
        in_specs=[pl.BlockSpec((None, t, 1), lambda b, i: (b, i, 0)), row, row],
        out_specs=(out, out, out, out),
        name="rope_tables",
    )(pos, inv, sign)


def _mla_proj_kernel(x_ref, sh_ref, sc_ref, g_ref, win_ref, qg_ref, wqn_ref, wqr_ref, wqrr_ref,
                     kvg_ref, wkn_ref, wv_ref, cos_ref, sin_ref, q_ref, k_ref, v_ref):
    x = x_ref[...]
    y = x * _rms_scale(x) * g_ref[...]
    h = (y * (1.0 + sc_ref[...]) + sh_ref[...]).astype(BF16)
    proj = jnp.dot(h, win_ref[...], preferred_element_type=F32)
    c_q = proj[:, :MLA_Q_LORA]
    c_kv = proj[:, MLA_Q_LORA:MLA_Q_LORA + MLA_KV_LORA]
    kr_a = proj[:, 384:512]
    kr_b = proj[:, 512:640]
    cq = (c_q * _rms_scale(c_q) * qg_ref[...]).astype(BF16)
    ckv = (c_kv * _rms_scale(c_kv) * kvg_ref[...]).astype(BF16)
    cos = cos_ref[...]
    sin = sin_ref[...]
    scale = LOG2_E * (MLA_NOPE + MLA_ROPE) ** -0.5
    kr = (kr_a * cos + kr_b * sin).astype(BF16)
    q_nope = jnp.dot(cq, wqn_ref[...], preferred_element_type=F32) * scale
    q_ra = jnp.dot(cq, wqr_ref[...], preferred_element_type=F32)
    q_rb = jnp.dot(cq, wqrr_ref[...], preferred_element_type=F32)
    k_nope = jnp.dot(ckv, wkn_ref[...], preferred_element_type=F32)
    v_ref[...] = jnp.dot(ckv, wv_ref[...], preferred_element_type=F32).astype(BF16)
    for hh in range(MLA_HEADS):
        hs = slice(hh * LANES, (hh + 1) * LANES)
        lo = slice(hh * ATTN_WIDTH, hh * ATTN_WIDTH + LANES)
        hi = slice(hh * ATTN_WIDTH + LANES, (hh + 1) * ATTN_WIDTH)
        q_ref[:, lo] = q_nope[:, hs].astype(BF16)
        q_ref[:, hi] = ((q_ra[:, hs] * cos + q_rb[:, hs] * sin) * scale).astype(BF16)
        k_ref[:, lo] = k_nope[:, hs].astype(BF16)
        k_ref[:, hi] = kr


def _rot_half_cols(w, half):
    return jnp.concatenate([-w[..., half:], w[..., :half]], axis=-1)


def _mla_proj(x, sh, sc, g, w_in, q_norm_g, w_uq, kv_norm_g, w_ukv, cos, sin):
    B, S, D = x.shape
    H = MLA_HEADS
    t = TOKEN_TILE
    half = MLA_ROPE // 2
    w_kr = w_in[:, MLA_Q_LORA + MLA_KV_LORA:]
    zpad = jnp.zeros((D, LANES - MLA_ROPE), F32)
    w_in_ext = jnp.concatenate(
        [w_in[:, :MLA_Q_LORA + MLA_KV_LORA], w_kr, zpad, _rot_half_cols(w_kr, half), zpad],
        axis=1).astype(BF16)
    wq = w_uq.reshape(MLA_Q_LORA, H, MLA_NOPE + MLA_ROPE)
    wq_nope = wq[..., :MLA_NOPE].reshape(MLA_Q_LORA, H * MLA_NOPE).astype(BF16)
    wq_r = wq[..., MLA_NOPE:]
    pad = ((0, 0), (0, 0), (0, LANES - MLA_ROPE))
    wq_rope = jnp.pad(wq_r, pad).reshape(MLA_Q_LORA, H * LANES).astype(BF16)
    wq_rope_rot = jnp.pad(_rot_half_cols(wq_r, half), pad).reshape(MLA_Q_LORA, H * LANES).astype(BF16)
    wkv = w_ukv.reshape(MLA_KV_LORA, H, MLA_NOPE + MLA_V)
    wk_nope = wkv[..., :MLA_NOPE].reshape(MLA_KV_LORA, H * MLA_NOPE).astype(BF16)
    wv = wkv[..., MLA_NOPE:].reshape(MLA_KV_LORA, H * MLA_V).astype(BF16)

    tok = lambda w: pl.BlockSpec((None, t, w), lambda b, i: (b, i, 0))
    vec = lambda w: pl.BlockSpec((None, 1, w), lambda b, i: (b, 0, 0))
    full = lambda a: pl.BlockSpec(a.shape, lambda b, i: (0,) * a.ndim)
    g2 = g.reshape(1, D)
    qg2 = q_norm_g.reshape(1, MLA_Q_LORA)
    kvg2 = kv_norm_g.reshape(1, MLA_KV_LORA)
    wide = jax.ShapeDtypeStruct((B, S, H * ATTN_WIDTH), BF16)
    return pl.pallas_call(
        _mla_proj_kernel,
        out_shape=(wide, wide, jax.ShapeDtypeStruct((B, S, H * MLA_V), BF16)),
        grid=(B, S // t),
        in_specs=[tok(D), vec(D), vec(D), full(g2), full(w_in_ext), full(qg2), full(wq_nope),
                  full(wq_rope), full(wq_rope_rot), full(kvg2), full(wk_nope), full(wv),
                  tok(LANES), tok(LANES)],
        out_specs=(tok(H * ATTN_WIDTH), tok(H * ATTN_WIDTH), tok(H * MLA_V)),
        name="mla_proj",
    )(x, sh, sc, g2, w_in_ext, qg2, wq_nope, wq_rope, wq_rope_rot, kvg2, wk_nope, wv, cos, sin)


def _attn_kernel(*refs, moba, tq, bk, dq, dv, nh):
    n_in = 4 if moba else 3
    q_ref, k_ref, v_ref = refs[:3]
    o_ref = refs[n_in]
    scratch = refs[n_in + 1:]
    if moba:
        kx_ref, vx_ref = scratch[:2]
        scratch = scratch[2:]
    else:
        kx_ref, vx_ref = k_ref, scratch[0]
        scratch = scratch[1:]
    per_head = len(scratch) // nh
    w = ATTN_WIDTH
    i = pl.program_id(2)
    seq = vx_ref.shape[0]
    row_id = lax.broadcasted_iota(jnp.int32, (tq, 1), 0)
    col_id = lax.broadcasted_iota(jnp.int32, (1, bk), 1)

    @pl.when((pl.program_id(0) == 0) & (pl.program_id(1) == 0) & (i == 0))
    def _():
        ones = _ones_column(seq)
        if moba:
            key_row = lax.broadcasted_iota(jnp.int32, (seq, LANES), 0)
            lane = lax.broadcasted_iota(jnp.int32, (seq, LANES), 1)
            block_id = jnp.where(key_row // MOBA_BLOCK == lane, 1.0, 0.0).astype(BF16)
        for hh in range(nh):
            vx_ref[:, hh * w + dv:(hh + 1) * w] = ones
            if moba:
                kx_ref[:, hh * w + dq:(hh + 1) * w] = block_id

    @pl.when(i == 0)
    def _():
        for hh in range(nh):
            vx_ref[:, hh * w:hh * w + dv] = v_ref[:, hh * dv:(hh + 1) * dv]
            if moba:
                kx_ref[:, hh * w:hh * w + dq] = k_ref[:, hh * dq:(hh + 1) * dq]

    def head_scratch(hh):
        sc = scratch[hh * per_head:(hh + 1) * per_head]
        return sc if moba else (None,) + tuple(sc)

    def prepare(hh):
        qx_ref, s_a, s_b, p_a, p_b, al_a, al_b, m_sc, acc_sc = head_scratch(hh)
        if moba:
            km_ref = refs[3]
            q = q_ref[:, hh * dq:(hh + 1) * dq]
            nb = km_ref.shape[0]
            gate = _nt_dot(km_ref[:, hh * dq:(hh + 1) * dq].astype(BF16), q)
            blk = lax.broadcasted_iota(jnp.int32, (nb, tq), 0)
            blk_f = blk.astype(F32)
            own = (i * tq + lax.broadcasted_iota(jnp.int32, (1, tq), 1)) // MOBA_BLOCK
            gate = jnp.where(blk < own, gate, -jnp.inf)
            sel = jnp.where(blk == own, 1.0, 0.0)
            for _ in range(MOBA_TOPK):
                mx = jnp.max(gate, axis=0, keepdims=True)
                first = jnp.min(jnp.where(gate == mx, blk_f, float(nb)), axis=0, keepdims=True)
                hit = blk_f == first
                sel = jnp.where(hit & (mx > -jnp.inf), 1.0, sel)
                gate = jnp.where(hit, -jnp.inf, gate)
            bias = jnp.where(sel > 0.0, 0.0, NEG_INF)
            bias = jnp.concatenate([bias, jnp.zeros((LANES - nb, tq), F32)], axis=0)
            qx_ref[:, :LANES] = q
            qx_ref[:, LANES:] = bias.T.astype(BF16)
        m_sc[...] = jnp.full(m_sc.shape, SOFTMAX_M_INIT, F32)
        acc_sc[...] = jnp.zeros(acc_sc.shape, F32)
        p_b[...] = jnp.zeros(p_b.shape, BF16)
        al_b[...] = jnp.ones(al_b.shape, F32)

    def scores(hh, j, slot, rows=slice(None)):
        sc = head_scratch(hh)
        qx = sc[0][rows, :] if moba else q_ref[rows, hh * w:(hh + 1) * w]
        start = pl.multiple_of(j * bk, bk)
        sc[1 + slot][rows, :] = _nt_dot(qx, kx_ref[pl.ds(start, bk), hh * w:(hh + 1) * w])

    def soft(hh, slot, diag_offset=None, rows=slice(None)):
        sc = head_scratch(hh)
        s_src, p_dst, al_dst, m_sc = sc[1 + slot], sc[3 + slot], sc[5 + slot], sc[7]
        s = s_src[rows, :]
        if diag_offset is not None:
            s = jnp.where(col_id + diag_offset <= row_id[rows, :], s, -jnp.inf)
        m_prev = m_sc[rows, :]
        m_new = jnp.maximum(m_prev, jnp.max(s, axis=1, keepdims=True))
        p_dst[rows, :] = jnp.exp2(s - m_new).astype(BF16)
        al_dst[rows, :] = jnp.exp2(m_prev - m_new)
        m_sc[rows, :] = m_new

    def fold(hh, j, slot, rows=slice(None)):
        sc = head_scratch(hh)
        p_src, al_src, acc_sc = sc[3 + slot], sc[5 + slot], sc[8]
        start = pl.multiple_of(j * bk, bk)
        acc_sc[rows, :] = al_src[rows, :] * acc_sc[rows, :] + jnp.dot(
            p_src[rows, :], vx_ref[pl.ds(start, bk), hh * w:(hh + 1) * w], preferred_element_type=F32)

    hs = range(nh)
    for hh in hs:
        prepare(hh)
    for hh in hs:
        scores(hh, 0, 0)

    def body(t, carry):
        j = 2 * t
        for hh in hs:
            scores(hh, j + 1, 1)
        for hh in hs:
            soft(hh, 0)
        for hh in hs:
            fold(hh, jnp.maximum(j - 1, 0), 1)
        for hh in hs:
            scores(hh, j + 2, 0)
        for hh in hs:
            soft(hh, 1)
        for hh in hs:
            fold(hh, j, 0)
        return carry

    lax.fori_loop(0, i, body, 0)
    j = 2 * i
    late = slice(bk, tq)
    for hh in hs:
        scores(hh, j + 1, 1, late)
    for hh in hs:
        soft(hh, 0, diag_offset=0)
    for hh in hs:
        fold(hh, jnp.maximum(j - 1, 0), 1)
    for hh in hs:
        soft(hh, 1, diag_offset=bk, rows=late)
    for hh in hs:
        fold(hh, j, 0)
    for hh in hs:
        fold(hh, j + 1, 1, late)
    for hh in hs:
        acc = head_scratch(hh)[8][...]
        o_ref[:, hh * dv:(hh + 1) * dv] = (acc[:, :dv] / acc[:, dv:dv + 1]).astype(o_ref.dtype)


def _attention(q, k, v, k_mean, *, heads, dq, dv, moba):
    B, S, _ = q.shape
    tq, bk, w, nh = ATTN_Q_TILE, ATTN_KV_TILE, ATTN_WIDTH, ATTN_HEADS_PER_STEP
    assert tq == 2 * bk and tq % MOBA_BLOCK == 0 and S % tq == 0 and heads % nh == 0
    in_specs = [
        pl.BlockSpec((None, tq, nh * dq), lambda b, h, i: (b, i, h)),
        pl.BlockSpec((None, S, nh * (dq if moba else w)), lambda b, h, i: (b, 0, h)),
        pl.BlockSpec((None, S, nh * dv), lambda b, h, i: (b, 0, h)),
    ]
    args = [q, k, v]
    scratch = [pltpu.VMEM((S, nh * w), BF16)]
    if moba:
        nb = k_mean.shape[1]
        in_specs.append(pl.BlockSpec((None, nb, nh * dq), lambda b, h, i: (b, 0, h)))
        args.append(k_mean)
        scratch.insert(0, pltpu.VMEM((S, nh * w), BF16))
    for _ in range(nh):
        if moba:
            scratch.append(pltpu.VMEM((tq, w), BF16))
        scratch += [pltpu.VMEM((tq, bk), F32), pltpu.VMEM((tq, bk), F32),
                    pltpu.VMEM((tq, bk), BF16), pltpu.VMEM((tq, bk), BF16),
                    pltpu.VMEM((tq, 1), F32), pltpu.VMEM((tq, 1), F32),
                    pltpu.VMEM((tq, 1), F32), pltpu.VMEM((tq, w), F32)]
    return pl.pallas_call(
        functools.partial(_attn_kernel, moba=moba, tq=tq, bk=bk, dq=dq, dv=dv, nh=nh),
        out_shape=jax.ShapeDtypeStruct((B, S, heads * dv), BF16),
        grid=(B, heads // nh, S // tq),
        in_specs=in_specs,
        out_specs=pl.BlockSpec((None, tq, nh * dv), lambda b, h, i: (b, i, h)),
        scratch_shapes=scratch,
        compiler_params=pltpu.CompilerParams(
            dimension_semantics=("arbitrary", "arbitrary", "arbitrary"),
            vmem_limit_bytes=VMEM_LIMIT_BYTES),
        name="moba_attention" if moba else "mla_attention",
    )(*args)


def _attn_out_kernel(a_ref, wo_ref, x_ref, ga_ref, g_ref, sh_ref, sc_ref, rw_ref, rb_ref,
                     x1_ref, h_ref, loc_ref, pw_ref, seg_ref, tot_ref, run_ref):
    first_step = (pl.program_id(0) == 0) & (pl.program_id(1) == 0)

    @pl.when(first_step)
    def _():
        run_ref[...] = jnp.zeros(run_ref.shape, F32)

    a = jnp.dot(a_ref[...], wo_ref[...], preferred_element_type=F32)
    x1 = x_ref[...] + ga_ref[...] * a
    x1_ref[...] = x1
    y = x1 * _rms_scale(x1) * g_ref[...]
    h = (y * (1.0 + sc_ref[...]) + sh_ref[...]).astype(BF16)
    h_ref[...] = h
    logits = jnp.dot(h, rw_ref[...], preferred_element_type=F32) + rb_ref[...]
    t = logits.shape[0]
    lane_i = lax.broadcasted_iota(jnp.int32, logits.shape, 1)
    lane = lane_i.astype(F32)
    logits = jnp.where(lane_i < N_EXPERTS, logits, -jnp.inf)
    top = None
    den = jnp.zeros((t, 1), F32)
    sel = jnp.zeros(logits.shape, F32)
    hits, ws = [], []
    for _ in range(TOP_K):
        mx = jnp.max(logits, axis=1, keepdims=True)
        first = jnp.min(jnp.where(logits == mx, lane, float(LANES)), axis=1, keepdims=True)
        hit = lane == first
        if top is None:
            top = mx
        w = jnp.exp(mx - top)
        den = den + w
        sel = jnp.where(hit, 1.0, sel)
        logits = jnp.where(hit, -jnp.inf, logits)
        hits.append(hit)
        ws.append(w)
    rt = ROUTE_TILE
    r_id = lax.broadcasted_iota(jnp.int32, (rt, rt), 0)
    c_id = lax.broadcasted_iota(jnp.int32, (rt, rt), 1)
    before = jnp.where(c_id < r_id, 1.0, 0.0).astype(BF16)
    a_id = lax.broadcasted_iota(jnp.int32, (LANES, LANES), 0)
    b_id = lax.broadcasted_iota(jnp.int32, (LANES, LANES), 1)
    earlier = jnp.where(a_id < b_id, 1.0, 0.0).astype(BF16)
    row8 = lax.broadcasted_iota(jnp.int32, (8, LANES), 0)
    local = []
    for sub in range(t // rt):
        sel_s = sel[sub * rt:(sub + 1) * rt]
        prefix = jnp.dot(before, sel_s.astype(BF16), preferred_element_type=F32)
        cnt = jnp.sum(sel_s, axis=0, keepdims=True)
        units = jnp.floor((cnt + (SEG_ALIGN - 1.0)) * (1.0 / SEG_ALIGN))
        cnt_pad = units * SEG_ALIGN
        seg_off = jnp.dot(jnp.broadcast_to(units, (8, LANES)).astype(BF16), earlier,
                          preferred_element_type=F32)[0:1] * SEG_ALIGN
        base = run_ref[...]
        run_ref[...] = base + cnt_pad
        seg = jnp.where(row8 == 0, base, jnp.where(row8 == 1, cnt_pad, jnp.where(row8 == 2, seg_off, 0.0)))
        seg_ref[sub] = seg.astype(jnp.int32)
        local.append(prefix + seg_off)
    tot_ref[...] = run_ref[...]
    local = jnp.concatenate(local, axis=0)
    loc = jnp.zeros(logits.shape, F32)
    pw = jnp.zeros(logits.shape, F32)
    for kk in range(TOP_K):
        row = jnp.sum(jnp.where(hits[kk], local, 0.0), axis=1, keepdims=True)
        loc = jnp.where(lane_i == kk, row, loc)
        pw = jnp.where(lane_i == kk, ws[kk] / den, pw)
    loc_ref[...] = loc.astype(jnp.int32)
    pw_ref[...] = pw


def _attn_out(attn, w_o, x, g_a, norm_g, sh_f, sc_f, router_w, router_b):
    B, S, D = x.shape
    t = TOKEN_TILE
    per = S // t
    sub = t // ROUTE_TILE
    rw = jnp.pad(router_w, ((0, 0), (0, LANES - N_EXPERTS))).astype(BF16)
    rb = jnp.pad(router_b, (0, LANES - N_EXPERTS)).reshape(1, LANES)
    wo = w_o.astype(BF16)
    g2 = norm_g.reshape(1, D)
    tok = lambda w: pl.BlockSpec((None, t, w), lambda b, i: (b, i, 0))
    vec = lambda w: pl.BlockSpec((None, 1, w), lambda b, i: (b, 0, 0))
    full = lambda a: pl.BlockSpec(a.shape, lambda b, i: (0,) * a.ndim)
    return pl.pallas_call(
        _attn_out_kernel,
        out_shape=(jax.ShapeDtypeStruct((B, S, D), F32),
                   jax.ShapeDtypeStruct((B, S, D), BF16),
                   jax.ShapeDtypeStruct((B, S, LANES), jnp.int32),
                   jax.ShapeDtypeStruct((B, S, LANES), F32),
                   jax.ShapeDtypeStruct((B * per * sub, 8, LANES), jnp.int32),
                   jax.ShapeDtypeStruct((1, LANES), F32)),
        grid=(B, per),
        in_specs=[tok(attn.shape[-1]), full(wo), tok(D), vec(D), full(g2), vec(D), vec(D),
                  full(rw), full(rb)],
        out_specs=(tok(D), tok(D), tok(LANES), tok(LANES),
                   pl.BlockSpec((sub, 8, LANES), lambda b, i: (b * per + i, 0, 0)),
                   pl.BlockSpec((1, LANES), lambda b, i: (0, 0))),
        scratch_shapes=[pltpu.VMEM((1, LANES), F32)],
        compiler_params=pltpu.CompilerParams(dimension_semantics=("arbitrary", "arbitrary")),
        name="attn_out_router",
    )(attn, wo, x, g_a, g2, sh_f, sc_f, rw, rb)


def _for_each_chunk(cnt, fn):
    shift = SEG_CHUNK.bit_length() - 1
    whole = cnt >> shift

    def piece(c, carry):
        fn(pl.multiple_of(c * SEG_CHUNK, SEG_CHUNK), SEG_CHUNK)
        return carry

    lax.fori_loop(0, whole, piece, 0)
    rows = SEG_CHUNK // 2
    while rows >= SEG_ALIGN:
        shift = rows.bit_length()
        done = (cnt >> shift) << shift

        @pl.when((cnt & rows) != 0)
        def _(done=done, rows=rows):
            fn(pl.multiple_of(done, SEG_ALIGN), rows)

        rows //= 2


def _dispatch_kernel(dst_ref, cnt_ref, off_ref, pad_start_ref, pad_cnt_ref, h_ref, loc_ref,
                     xs_ref, seg_buf, zero_buf, sems):
    i = pl.program_id(0)
    last = pl.num_programs(0) - 1
    slot = lax.rem(i, 2)
    t = h_ref.shape[0]
    n_loc = seg_buf.shape[1]
    pad_sem = sems.at[2]

    @pl.when(i == 0)
    def _():
        zero_buf[...] = jnp.zeros(zero_buf.shape, F32)
        for wait in (False, True):
            def per_expert(e, carry, wait=wait):
                start = pad_start_ref[e]

                def chunk(done, rows):
                    cp = pltpu.make_async_copy(
                        zero_buf.at[pl.ds(0, rows), :],
                        xs_ref.at[pl.ds(pl.multiple_of(start + done, SEG_ALIGN), rows), :], pad_sem)
                    cp.wait() if wait else cp.start()

                _for_each_chunk(pad_cnt_ref[e], chunk)
                return carry

            lax.fori_loop(0, N_EXPERTS, per_expert, 0)

    def segment_copies(tile, buf, wait):
        def per_expert(e, carry):
            idx = tile * N_EXPERTS + e
            off = off_ref[idx]
            dst = dst_ref[idx]

            def chunk(done, rows):
                cp = pltpu.make_async_copy(
                    seg_buf.at[buf, pl.ds(pl.multiple_of(off + done, SEG_ALIGN), rows), :],
                    xs_ref.at[pl.ds(pl.multiple_of(dst + done, SEG_ALIGN), rows), :], sems.at[buf])
                cp.wait() if wait else cp.start()

            _for_each_chunk(cnt_ref[idx], chunk)
            return carry

        lax.fori_loop(0, N_EXPERTS, per_expert, 0)

    loc_t = loc_ref[...].astype(F32).T
    l_id = lax.broadcasted_iota(jnp.int32, (n_loc, t), 0).astype(F32)
    onehot = jnp.zeros((n_loc, t), F32)
    for kk in range(TOP_K):
        onehot = jnp.where(l_id == loc_t[kk:kk + 1, :], 1.0, onehot)
    seg_buf[slot] = jnp.dot(onehot.astype(BF16), h_ref[...], preferred_element_type=F32)
    segment_copies(i, slot, wait=False)

    @pl.when(i > 0)
    def _():
        segment_copies(i - 1, 1 - slot, wait=True)

    @pl.when(i == last)
    def _():
        segment_copies(i, slot, wait=True)


def _dispatch(h, loc, dst, seg_cnt, seg_off, pad_start, pad_cnt, n_rows):
    N, D = h.shape
    t = ROUTE_TILE
    return pl.pallas_call(
        _dispatch_kernel,
        out_shape=jax.ShapeDtypeStruct((n_rows, D), F32),
        grid_spec=pltpu.PrefetchScalarGridSpec(
            num_scalar_prefetch=5,
            grid=(N // t,),
            in_specs=[pl.BlockSpec((t, D), lambda i, *_: (i, 0)),
                      pl.BlockSpec((t, LANES), lambda i, *_: (i, 0))],
            out_specs=pl.BlockSpec(memory_space=pl.ANY),
            scratch_shapes=[pltpu.VMEM((2, SEG_BUF_ROWS, D), F32),
                            pltpu.VMEM((SEG_CHUNK, D), F32),
                            pltpu.SemaphoreType.DMA((3,))],
        ),
        compiler_params=pltpu.CompilerParams(dimension_semantics=("arbitrary",),
                                             vmem_limit_bytes=VMEM_LIMIT_BYTES),
        name="moe_dispatch",
    )(dst, seg_cnt, seg_off, pad_start, pad_cnt, h, loc)


def _experts_kernel(tile_e_ref, tile_blk_ref, nact_ref, xs_ref, wg_ref, bg_ref, wu_ref, bu_ref,
                    wd_ref, bd_ref, y_ref, wg_bf, wu_bf, wd_bf):
    j = pl.program_id(0)

    @pl.when(j < nact_ref[0])
    def _():
        prev = tile_e_ref[jnp.maximum(j - 1, 0)]

        @pl.when((j == 0) | (tile_e_ref[j] != prev))
        def _():
            wg_bf[...] = wg_ref[...].astype(BF16)
            wu_bf[...] = wu_ref[...].astype(BF16)
            wd_bf[...] = wd_ref[...].astype(BF16)

        x = xs_ref[...].astype(BF16)
        g = jnp.minimum(jnp.dot(x, wg_bf[...], preferred_element_type=F32) + bg_ref[...], SWIGLU_LIMIT)
        u = jnp.clip(jnp.dot(x, wu_bf[...], preferred_element_type=F32) + bu_ref[...],
                     -SWIGLU_LIMIT, SWIGLU_LIMIT)
        a = g * jax.nn.sigmoid(SWIGLU_ALPHA * g) * (u + 1.0)
        y_ref[...] = jnp.dot(a.astype(BF16), wd_bf[...], preferred_element_type=F32) + bd_ref[...]


def _experts(xs, tile_e, tile_blk, nact, layer, w_gate, b_gate, w_up, b_up, w_down, b_down):
    P, D = xs.shape
    L, E, _, F = w_gate.shape
    tm = EXPERT_TILE
    rows = pl.BlockSpec((tm, D), lambda j, te, tb, na: (tb[j], 0))
    wspec = lambda r, c: pl.BlockSpec((None, None, r, c), lambda j, te, tb, na: (layer, te[j], 0, 0))
    return pl.pallas_call(
        _experts_kernel,
        out_shape=jax.ShapeDtypeStruct((P, D), F32),
        grid_spec=pltpu.PrefetchScalarGridSpec(
            num_scalar_prefetch=3,
            grid=(P // tm,),
            in_specs=[rows, wspec(D, F), wspec(1, F), wspec(D, F), wspec(1, F), wspec(F, D), wspec(1, D)],
            out_specs=rows,
            scratch_shapes=[pltpu.VMEM((D, F), BF16), pltpu.VMEM((D, F), BF16), pltpu.VMEM((F, D), BF16)],
        ),
        compiler_params=pltpu.CompilerParams(dimension_semantics=("arbitrary",),
                                             vmem_limit_bytes=VMEM_LIMIT_BYTES),
        name="moe_experts",
    )(tile_e, tile_blk, nact, xs, w_gate, b_gate.reshape(L, E, 1, F), w_up, b_up.reshape(L, E, 1, F),
      w_down, b_down.reshape(L, E, 1, D))


def _combine_kernel(*refs, final_norm):
    if final_norm:
        (src_ref, cnt_ref, off_ref, y_ref, loc_ref, pw_ref, x_ref, gf_ref, fg_ref, fsh_ref, fsc_ref,
         o_ref, seg_buf, sems) = refs
    else:
        src_ref, cnt_ref, off_ref, y_ref, loc_ref, pw_ref, x_ref, gf_ref, o_ref, seg_buf, sems = refs
    i = pl.program_id(0)
    last = pl.num_programs(0) - 1
    slot = lax.rem(i, 2)
    t = x_ref.shape[0]
    n_loc = seg_buf.shape[1]

    def segment_copies(tile, buf, wait):
        def per_expert(e, carry):
            idx = tile * N_EXPERTS + e
            off = off_ref[idx]
            src = src_ref[idx]

            def chunk(done, rows):
                cp = pltpu.make_async_copy(
                    y_ref.at[pl.ds(pl.multiple_of(src + done, SEG_ALIGN), rows), :],
                    seg_buf.at[buf, pl.ds(pl.multiple_of(off + done, SEG_ALIGN), rows), :], sems.at[buf])
                cp.wait() if wait else cp.start()

            _for_each_chunk(cnt_ref[idx], chunk)
            return carry

        lax.fori_loop(0, N_EXPERTS, per_expert, 0)

    @pl.when(i == 0)
    def _():
        seg_buf[...] = jnp.zeros(seg_buf.shape, F32)
        segment_copies(0, 0, wait=False)

    @pl.when(i < last)
    def _():
        segment_copies(i + 1, 1 - slot, wait=False)

    segment_copies(i, slot, wait=True)
    loc = loc_ref[...].astype(F32)
    pw = pw_ref[...]
    l_id = lax.broadcasted_iota(jnp.int32, (t, n_loc), 1).astype(F32)
    weights = jnp.zeros((t, n_loc), F32)
    for kk in range(TOP_K):
        weights = jnp.where(l_id == loc[:, kk:kk + 1], pw[:, kk:kk + 1], weights)
    f = jnp.dot(weights.astype(BF16), seg_buf[slot].astype(BF16), preferred_element_type=F32)
    x2 = x_ref[...] + gf_ref[...] * f
    if final_norm:
        y = x2 * _rms_scale(x2) * fg_ref[...]
        x2 = y * (1.0 + fsc_ref[...]) + fsh_ref[...]
    o_ref[...] = x2


def _combine(y, loc, pw, src, seg_cnt, seg_off, x, g_f, final):
    B, S, D = x.shape
    t = ROUTE_TILE
    per = S // t
    tok = lambda w: pl.BlockSpec((None, t, w), lambda i, *_: (i // per, i % per, 0))
    vec = pl.BlockSpec((None, 1, D), lambda i, *_: (i // per, 0, 0))
    in_specs = [pl.BlockSpec(memory_space=pl.ANY), tok(LANES), tok(LANES), tok(D), vec]
    args = [y, loc, pw, x, g_f]
    if final is not None:
        fg, fsh, fsc = final
        in_specs += [pl.BlockSpec((1, D), lambda i, *_: (0, 0)), vec, vec]
        args += [fg.reshape(1, D), fsh, fsc]
    return pl.pallas_call(
        functools.partial(_combine_kernel, final_norm=final is not None),
        out_shape=jax.ShapeDtypeStruct((B, S, D), F32),
        grid_spec=pltpu.PrefetchScalarGridSpec(
            num_scalar_prefetch=3,
            grid=(B * per,),
            in_specs=in_specs,
            out_specs=tok(D),
            scratch_shapes=[pltpu.VMEM((2, SEG_BUF_ROWS, D), F32), pltpu.SemaphoreType.DMA((2,))],
        ),
        compiler_params=pltpu.CompilerParams(dimension_semantics=("arbitrary",),
                                             vmem_limit_bytes=VMEM_LIMIT_BYTES),
        name="moe_combine",
    )(src, seg_cnt, seg_off, *args)


def _moe(h, loc, pw, seg, totals, x, g_f, layer, w_gate, b_gate, w_up, b_up, w_down, b_down, final):
    B, S, D = x.shape
    N = B * S
    E = N_EXPERTS
    tm = EXPERT_TILE
    n_tok_tiles = N // ROUTE_TILE
    max_rows = N * TOP_K + n_tok_tiles * E * (SEG_ALIGN - 1)
    n_tiles = -(-max_rows // tm) + E
    tot = totals[0, :E].astype(jnp.int32)
    tiles_per = (tot + tm - 1) // tm
    tile_end = jnp.cumsum(tiles_per)
    row_start = (tile_end - tiles_per) * tm
    nact = tile_end[-1:]
    jj = jnp.minimum(jnp.arange(n_tiles, dtype=jnp.int32), nact[0] - 1)
    tile_e = jnp.sum((jj[:, None] >= tile_end[None, :]).astype(jnp.int32), axis=1)
    place = (row_start[None, :] + seg[:, 0, :E]).reshape(-1)
    seg_cnt = seg[:, 1, :E].reshape(-1)
    seg_off = seg[:, 2, :E].reshape(-1)
    loc2 = loc.reshape(N, LANES)
    xs = _dispatch(h.reshape(N, D), loc2, place, seg_cnt, seg_off, row_start + tot,
                   tiles_per * tm - tot, n_tiles * tm)
    y = _experts(xs, tile_e, jj, nact, layer, w_gate, b_gate, w_up, b_up, w_down, b_down)
    return _combine(y, loc, pw, place, seg_cnt, seg_off, x, g_f, final)


def _moba_proj_kernel(x_ref, sha_ref, sca_ref, ga_ref, shk_ref, sck_ref, gk_ref, wq_ref, wkv_ref,
                      cos_ref, sin_ref, q_ref, k_ref, v_ref, km_ref):
    x = x_ref[...]
    xn = x * _rms_scale(x)
    h = ((xn * ga_ref[...]) * (1.0 + sca_ref[...]) + sha_ref[...]).astype(BF16)
    hkv = ((xn * gk_ref[...]) * (1.0 + sck_ref[...]) + shk_ref[...]).astype(BF16)
    q = jnp.dot(h, wq_ref[...], preferred_element_type=F32)
    kv = jnp.dot(hkv, wkv_ref[...], preferred_element_type=F32)
    cos = cos_ref[...]
    sin = sin_ref[...]
    scale = LOG2_E * MOBA_HEAD_DIM ** -0.5
    hd = MOBA_HEAD_DIM
    t = x.shape[0]
    width = MOBA_HEADS * hd
    v_ref[...] = kv[:, width:].astype(BF16)
    for hh in range(MOBA_HEADS):
        hs = slice(hh * hd, (hh + 1) * hd)
        qh = q[:, hs]
        kh = kv[:, hs]
        q_ref[:, hs] = ((qh * cos + pltpu.roll(qh, hd // 2, 1) * sin) * scale).astype(BF16)
        kr = kh * cos + pltpu.roll(kh, hd // 2, 1) * sin
        k_ref[:, hs] = kr.astype(BF16)
        km_ref[:, hs] = jnp.mean(kr.reshape(t // MOBA_BLOCK, MOBA_BLOCK, hd), axis=1)


def _moba_proj(x, sh_a, sc_a, g_a, sh_k, sc_k, g_k, w_q, w_kv, cos, sin):
    B, S, D = x.shape
    t = TOKEN_TILE
    H = MOBA_HEADS
    width = H * MOBA_HEAD_DIM
    per = t // MOBA_BLOCK
    assert S // MOBA_BLOCK <= LANES
    tok = lambda w: pl.BlockSpec((None, t, w), lambda b, i: (b, i, 0))
    vec = lambda w: pl.BlockSpec((None, 1, w), lambda b, i: (b, 0, 0))
    full = lambda a: pl.BlockSpec(a.shape, lambda b, i: (0,) * a.ndim)
    ga2 = g_a.reshape(1, D)
    gk2 = g_k.reshape(1, D)
    wq = w_q.astype(BF16)
    wkv = w_kv.astype(BF16)
    rows = jax.ShapeDtypeStruct((B, S, width), BF16)
    q, k, v, km = pl.pallas_call(
        _moba_proj_kernel,
        out_shape=(rows, rows, rows, jax.ShapeDtypeStruct((B, S // t, per, width), F32)),
        grid=(B, S // t),
        in_specs=[tok(D), vec(D), vec(D), full(ga2), vec(D), vec(D), full(gk2), full(wq), full(wkv),
                  tok(LANES), tok(LANES)],
        out_specs=(tok(width), tok(width), tok(width),
                   pl.BlockSpec((None, None, per, width), lambda b, i: (b, i, 0, 0))),
        name="moba_proj",
    )(x, sh_a, sc_a, ga2, sh_k, sc_k, gk2, wq, wkv, cos, sin)
    return q, k, v, km.reshape(B, S // MOBA_BLOCK, width)


def _split_mod(mod, n):
    return [mod[:, None, j * D_MODEL:(j + 1) * D_MODEL] for j in range(n)]


def kernel(x, c, positions, ada_w, ada_b, norm_attn_g, norm_ffn_g, mla_w_in, mla_q_norm_g, mla_w_uq, mla_kv_norm_g, mla_w_ukv, mla_w_o, kv_ada_w, kv_ada_b, kv_norm_g, moba_w_kv, moba_w_q, moba_w_o, router_w, router_b, w_gate, b_gate, w_up, b_up, w_down, b_down, final_ada_w, final_ada_b, final_norm_g):
    mods = _ada_linear(c, ada_w, ada_b)
    kv_mod = _ada_linear(c, kv_ada_w[None], kv_ada_b[None])[0]
    f_mod = _ada_linear(c, final_ada_w[None], final_ada_b[None])[0]
    cos_a, sin_a, cos_b, sin_b = _rope_tables(positions)
    f_sh, f_sc = _split_mod(f_mod, 2)
    shared = None
    for layer in range(DEPTH):
        sh_a, sc_a, g_a, sh_f, sc_f, g_f = _split_mod(mods[layer], 6)
        if layer < N_A:
            q, k, v = _mla_proj(x, sh_a, sc_a, norm_attn_g[layer], mla_w_in[layer], mla_q_norm_g[layer],
                                mla_w_uq[layer], mla_kv_norm_g[layer], mla_w_ukv[layer], cos_a, sin_a)
            attn = _attention(q, k, v, None, heads=MLA_HEADS, dq=ATTN_WIDTH, dv=MLA_V, moba=False)
            w_o = mla_w_o[layer]
        else:
            j = layer - N_A
            kv_sh, kv_sc = _split_mod(kv_mod, 2)
            q, k, v, km = _moba_proj(x, sh_a, sc_a, norm_attn_g[layer], kv_sh, kv_sc, kv_norm_g,
                                     moba_w_q[j], moba_w_kv, cos_b, sin_b)
            if shared is None:
                shared = (k, v, km)
            attn = _attention(q, shared[0], shared[1], shared[2], heads=MOBA_HEADS, dq=MOBA_HEAD_DIM,
                              dv=MOBA_HEAD_DIM, moba=True)
            w_o = moba_w_o[j]
        x, h, loc, pw, seg, totals = _attn_out(attn, w_o, x, g_a, norm_ffn_g[layer], sh_f, sc_f,
                                               router_w[layer], router_b[layer])
        final = (final_norm_g, f_sh, f_sc) if layer == DEPTH - 1 else None
        x = _moe(h, loc, pw, seg, totals, x, g_f, layer, w_gate, b_gate, w_up, b_up, w_down, b_down,
                 final)
    return x
```

```python
import functools

import jax
import jax.numpy as jnp
from jax import lax
from jax.experimental import pallas as pl
from jax.experimental.pallas import tpu as pltpu

D_MODEL = 1024
DEPTH = 2
N_A = DEPTH // 2

MLA_HEADS = 8
MLA_Q_LORA = 256
MLA_KV_LORA = 128
MLA_NOPE = 128
MLA_ROPE = 64
MLA_V = 128

MOBA_HEADS = 8
MOBA_HEAD_DIM = D_MODEL // MOBA_HEADS
MOBA_BLOCK = 256
MOBA_TOPK = 3

N_EXPERTS = 32
TOP_K = 4
SWIGLU_LIMIT = 7.0
SWIGLU_ALPHA = 1.702

ROPE_THETA = 10000.0
NORM_EPS = 1e-6
NEG_INF = -1e30
LOG2_E = 1.4426950408889634
SOFTMAX_M_INIT = -1e29

LANES = 128
TOKEN_TILE = 1024
ATTN_Q_TILE = 1024
ATTN_KV_TILE = 512
ATTN_WIDTH = 2 * LANES
ATTN_HEADS_PER_STEP = 2
EXPERT_TILE = 512
FF_SPLIT = 2
SEG_ALIGN = 8
SEG_CHUNK = 64
ROUTE_TILE = 512
SEG_BUF_ROWS = ROUTE_TILE * TOP_K + N_EXPERTS * SEG_ALIGN
VMEM_LIMIT_BYTES = 56 * 1024 * 1024

F32 = jnp.float32
BF16 = jnp.bfloat16


def _rms_scale(x):
    return lax.rsqrt(jnp.mean(x * x, axis=-1, keepdims=True) + NORM_EPS)


def _nt_dot(a, b):
    return lax.dot_general(a, b, (((1,), (1,)), ((), ())), preferred_element_type=F32)


def _ones_column(rows):
    lane = lax.broadcasted_iota(jnp.int32, (rows, LANES), 1)
    return jnp.where(lane == 0, 1.0, 0.0).astype(BF16)


def _ada_kernel(c_ref, w_ref, b_ref, o_ref):
    c = c_ref[...]
    ca = c * jax.nn.sigmoid(c)
    o_ref[...] = jnp.dot(ca, w_ref[...], precision=lax.Precision.HIGHEST,
                         preferred_element_type=F32) + b_ref[...]


def _ada_linear(c, w, b):
    L, D, M = w.shape
    B = c.shape[0]
    bn = 2048
    return pl.pallas_call(
        _ada_kernel,
        out_shape=jax.ShapeDtypeStruct((L, B, M), F32),
        grid=(L, M // bn),
        in_specs=[
            pl.BlockSpec((B, D), lambda l, j: (0, 0)),
            pl.BlockSpec((None, D, bn), lambda l, j: (l, 0, j)),
            pl.BlockSpec((None, 1, bn), lambda l, j: (l, 0, j)),
        ],
        out_specs=pl.BlockSpec((None, B, bn), lambda l, j: (l, 0, j)),
        name="ada_linear",
    )(c, w, b.reshape(L, 1, M))


def _rope_table_kernel(pos_ref, inv_ref, sign_ref, ca_ref, sa_ref, cb_ref, sb_ref):
    ang = pos_ref[...] * inv_ref[...]
    lane = lax.broadcasted_iota(jnp.int32, ang.shape, 1)
    na, nb = MLA_ROPE // 2, MOBA_HEAD_DIM // 2

    def spread(v):
        a = jnp.where(lane < na, v, 0.0)
        a = a + pltpu.roll(a, na, 1)
        a = a + pltpu.roll(a, 2 * na, 1)
        b = jnp.where((lane >= na) & (lane < na + nb), v, 0.0)
        b = pltpu.roll(b, LANES - na, 1)
        b = b + pltpu.roll(b, nb, 1)
        return a, b

    ca_ref[...], cb_ref[...] = spread(jnp.cos(ang))
    sa, sb = spread(jnp.sin(ang))
    sa_ref[...] = sa
    sb_ref[...] = sb * sign_ref[...]


def _rope_tables(positions):
    B, S = positions.shape
    t = 1024
    assert MLA_ROPE // 2 + MOBA_HEAD_DIM // 2 <= LANES and MLA_ROPE * 2 == LANES == MOBA_HEAD_DIM

    def inv_freq(d):
        half = d // 2
        return ROPE_THETA ** (-jnp.arange(half, dtype=F32) * (2.0 / d))

    inv = jnp.concatenate([inv_freq(MLA_ROPE), inv_freq(MOBA_HEAD_DIM),
                           jnp.zeros((LANES - MLA_ROPE // 2 - MOBA_HEAD_DIM // 2,), F32)]).reshape(1, LANES)
    half = MOBA_HEAD_DIM // 2
    sign = jnp.concatenate([-jnp.ones((half,), F32), jnp.ones((half,), F32)]).reshape(1, LANES)
    pos = positions.astype(F32).reshape(B, S, 1)
    row = pl.BlockSpec((1, LANES), lambda b, i: (0, 0))
    out = pl.BlockSpec((None, t, LANES), lambda b, i: (b, i, 0))
    shp = jax.ShapeDtypeStruct((B, S, LANES), F32)
    return pl.pallas_call(
        _rope_table_kernel,
        out_shape=(shp, shp, shp, shp),
        grid=(B, S // t),
        in_specs=[pl.BlockSpec((None, t, 1), lambda b, i: (b, i, 0)), row, row],
        out_specs=(out, out, out, out),
        name="rope_tables",
    )(pos, inv, sign)


def _mla_proj_kernel(x_ref, sh_ref, sc_ref, g_ref, win_ref, qg_ref, wqn_ref, wqr_ref, wqrr_ref,
                     kvg_ref, wkn_ref, wv_ref, cos_ref, sin_ref, q_ref, k_ref, v_ref):
    x = x_ref[...]
    y = x * _rms_scale(x) * g_ref[...]
    h = (y * (1.0 + sc_ref[...]) + sh_ref[...]).astype(BF16)
    proj = jnp.dot(h, win_ref[...], preferred_element_type=F32)
    c_q = proj[:, :MLA_Q_LORA]
    c_kv = proj[:, MLA_Q_LORA:MLA_Q_LORA + MLA_KV_LORA]
    kr_a = proj[:, 384:512]
    kr_b = proj[:, 512:640]
    cq = (c_q * _rms_scale(c_q) * qg_ref[...]).astype(BF16)
    ckv = (c_kv * _rms_scale(c_kv) * kvg_ref[...]).astype(BF16)
    cos = cos_ref[...]
    sin = sin_ref[...]
    scale = LOG2_E * (MLA_NOPE + MLA_ROPE) ** -0.5
    kr = (kr_a * cos + kr_b * sin).astype(BF16)
    q_nope = jnp.dot(cq, wqn_ref[...], preferred_element_type=F32) * scale
    q_ra = jnp.dot(cq, wqr_ref[...], preferred_element_type=F32)
    q_rb = jnp.dot(cq, wqrr_ref[...], preferred_element_type=F32)
    k_nope = jnp.dot(ckv, wkn_ref[...], preferred_element_type=F32)
    v_ref[...] = jnp.dot(ckv, wv_ref[...], preferred_element_type=F32).astype(BF16)
    for hh in range(MLA_HEADS):
        hs = slice(hh * LANES, (hh + 1) * LANES)
        lo = slice(hh * ATTN_WIDTH, hh * ATTN_WIDTH + LANES)
        hi = slice(hh * ATTN_WIDTH + LANES, (hh + 1) * ATTN_WIDTH)
        q_ref[:, lo] = q_nope[:, hs].astype(BF16)
        q_ref[:, hi] = ((q_ra[:, hs] * cos + q_rb[:, hs] * sin) * scale).astype(BF16)
        k_ref[:, lo] = k_nope[:, hs].astype(BF16)
        k_ref[:, hi] = kr


def _rot_half_cols(w, half):
    return jnp.concatenate([-w[..., half:], w[..., :half]], axis=-1)


def _mla_proj(x, sh, sc, g, w_in, q_norm_g, w_uq, kv_norm_g, w_ukv, cos, sin):
    B, S, D = x.shape
    H = MLA_HEADS
    t = TOKEN_TILE
    half = MLA_ROPE // 2
    w_kr = w_in[:, MLA_Q_LORA + MLA_KV_LORA:]
    zpad = jnp.zeros((D, LANES - MLA_ROPE), F32)
    w_in_ext = jnp.concatenate(
        [w_in[:, :MLA_Q_LORA + MLA_KV_LORA], w_kr, zpad, _rot_half_cols(w_kr, half), zpad],
        axis=1).astype(BF16)
    wq = w_uq.reshape(MLA_Q_LORA, H, MLA_NOPE + MLA_ROPE)
    wq_nope = wq[..., :MLA_NOPE].reshape(MLA_Q_LORA, H * MLA_NOPE).astype(BF16)
    wq_r = wq[..., MLA_NOPE:]
    pad = ((0, 0), (0, 0), (0, LANES - MLA_ROPE))
    wq_rope = jnp.pad(wq_r, pad).reshape(MLA_Q_LORA, H * LANES).astype(BF16)
    wq_rope_rot = jnp.pad(_rot_half_cols(wq_r, half), pad).reshape(MLA_Q_LORA, H * LANES).astype(BF16)
    wkv = w_ukv.reshape(MLA_KV_LORA, H, MLA_NOPE + MLA_V)
    wk_nope = wkv[..., :MLA_NOPE].reshape(MLA_KV_LORA, H * MLA_NOPE).astype(BF16)
    wv = wkv[..., MLA_NOPE:].reshape(MLA_KV_LORA, H * MLA_V).astype(BF16)

    tok = lambda w: pl.BlockSpec((None, t, w), lambda b, i: (b, i, 0))
    vec = lambda w: pl.BlockSpec((None, 1, w), lambda b, i: (b, 0, 0))
    full = lambda a: pl.BlockSpec(a.shape, lambda b, i: (0,) * a.ndim)
    g2 = g.reshape(1, D)
    qg2 = q_norm_g.reshape(1, MLA_Q_LORA)
    kvg2 = kv_norm_g.reshape(1, MLA_KV_LORA)
    wide = jax.ShapeDtypeStruct((B, S, H * ATTN_WIDTH), BF16)
    return pl.pallas_call(
        _mla_proj_kernel,
        out_shape=(wide, wide, jax.ShapeDtypeStruct((B, S, H * MLA_V), BF16)),
        grid=(B, S // t),
        in_specs=[tok(D), vec(D), vec(D), full(g2), full(w_in_ext), full(qg2), full(wq_nope),
                  full(wq_rope), full(wq_rope_rot), full(kvg2), full(wk_nope), full(wv),
                  tok(LANES), tok(LANES)],
        out_specs=(tok(H * ATTN_WIDTH), tok(H * ATTN_WIDTH), tok(H * MLA_V)),
        name="mla_proj",
    )(x, sh, sc, g2, w_in_ext, qg2, wq_nope, wq_rope, wq_rope_rot, kvg2, wk_nope, wv, cos, sin)


def _attn_kernel(*refs, moba, tq, bk, dq, dv, nh):
    n_in = 4 if moba else 3
    q_ref, k_ref, v_ref = refs[:3]
    o_ref = refs[n_in]
    scratch = refs[n_in + 1:]
    if moba:
        kx_ref, vx_ref = scratch[:2]
        scratch = scratch[2:]
    else:
        kx_ref, vx_ref = k_ref, scratch[0]
        scratch = scratch[1:]
    per_head = len(scratch) // nh
    w = ATTN_WIDTH
    i = pl.program_id(2)
    seq = vx_ref.shape[0]
    row_id = lax.broadcasted_iota(jnp.int32, (tq, 1), 0)
    col_id = lax.broadcasted_iota(jnp.int32, (1, bk), 1)

    @pl.when((pl.program_id(0) == 0) & (pl.program_id(1) == 0) & (i == 0))
    def _():
        ones = _ones_column(seq)
        if moba:
            key_row = lax.broadcasted_iota(jnp.int32, (seq, LANES), 0)
            lane = lax.broadcasted_iota(jnp.int32, (seq, LANES), 1)
            block_id = jnp.where(key_row // MOBA_BLOCK == lane, 1.0, 0.0).astype(BF16)
        for hh in range(nh):
            vx_ref[:, hh * w + dv:(hh + 1) * w] = ones
            if moba:
                kx_ref[:, hh * w + dq:(hh + 1) * w] = block_id

    @pl.when(i == 0)
    def _():
        for hh in range(nh):
            vx_ref[:, hh * w:hh * w + dv] = v_ref[:, hh * dv:(hh + 1) * dv]
            if moba:
                kx_ref[:, hh * w:hh * w + dq] = k_ref[:, hh * dq:(hh + 1) * dq]

    def head_scratch(hh):
        sc = scratch[hh * per_head:(hh + 1) * per_head]
        return sc if moba else (None,) + tuple(sc)

    def prepare(hh):
        qx_ref, s_a, s_b, p_a, p_b, al_a, al_b, m_sc, acc_sc = head_scratch(hh)
        if moba:
            km_ref = refs[3]
            q = q_ref[:, hh * dq:(hh + 1) * dq]
            nb = km_ref.shape[0]
            gate = _nt_dot(km_ref[:, hh * dq:(hh + 1) * dq].astype(BF16), q)
            blk = lax.broadcasted_iota(jnp.int32, (nb, tq), 0)
            blk_f = blk.astype(F32)
            own = (i * tq + lax.broadcasted_iota(jnp.int32, (1, tq), 1)) // MOBA_BLOCK
            gate = jnp.where(blk < own, gate, -jnp.inf)
            sel = jnp.where(blk == own, 1.0, 0.0)
            for _ in range(MOBA_TOPK):
                mx = jnp.max(gate, axis=0, keepdims=True)
                first = jnp.min(jnp.where(gate == mx, blk_f, float(nb)), axis=0, keepdims=True)
                hit = blk_f == first
                sel = jnp.where(hit & (mx > -jnp.inf), 1.0, sel)
                gate = jnp.where(hit, -jnp.inf, gate)
            bias = jnp.where(sel > 0.0, 0.0, NEG_INF)
            bias = jnp.concatenate([bias, jnp.zeros((LANES - nb, tq), F32)], axis=0)
            qx_ref[:, :LANES] = q
            qx_ref[:, LANES:] = bias.T.astype(BF16)
        m_sc[...] = jnp.full(m_sc.shape, SOFTMAX_M_INIT, F32)
        acc_sc[...] = jnp.zeros(acc_sc.shape, F32)
        p_b[...] = jnp.zeros(p_b.shape, BF16)
        al_b[...] = jnp.ones(al_b.shape, F32)

    def scores(hh, j, slot, rows=slice(None)):
        sc = head_scratch(hh)
        qx = sc[0][rows, :] if moba else q_ref[rows, hh * w:(hh + 1) * w]
        start = pl.multiple_of(j * bk, bk)
        sc[1 + slot][rows, :] = _nt_dot(qx, kx_ref[pl.ds(start, bk), hh * w:(hh + 1) * w])

    def soft(hh, slot, diag_offset=None, rows=slice(None)):
        sc = head_scratch(hh)
        s_src, p_dst, al_dst, m_sc = sc[1 + slot], sc[3 + slot], sc[5 + slot], sc[7]
        s = s_src[rows, :]
        if diag_offset is not None:
            s = jnp.where(col_id + diag_offset <= row_id[rows, :], s, -jnp.inf)
        m_prev = m_sc[rows, :]
        m_new = jnp.maximum(m_prev, jnp.max(s, axis=1, keepdims=True))
        p_dst[rows, :] = jnp.exp2(s - m_new).astype(BF16)
        al_dst[rows, :] = jnp.exp2(m_prev - m_new)
        m_sc[rows, :] = m_new

    def fold(hh, j, slot, rows=slice(None)):
        sc = head_scratch(hh)
        p_src, al_src, acc_sc = sc[3 + slot], sc[5 + slot], sc[8]
        start = pl.multiple_of(j * bk, bk)
        acc_sc[rows, :] = al_src[rows, :] * acc_sc[rows, :] + jnp.dot(
            p_src[rows, :], vx_ref[pl.ds(start, bk), hh * w:(hh + 1) * w], preferred_element_type=F32)

    hs = range(nh)
    for hh in hs:
        prepare(hh)
    for hh in hs:
        scores(hh, 0, 0)

    def body(t, carry):
        j = 2 * t
        for hh in hs:
            scores(hh, j + 1, 1)
        for hh in hs:
            soft(hh, 0)
        for hh in hs:
            fold(hh, jnp.maximum(j - 1, 0), 1)
        for hh in hs:
            scores(hh, j + 2, 0)
        for hh in hs:
            soft(hh, 1)
        for hh in hs:
            fold(hh, j, 0)
        return carry

    lax.fori_loop(0, i, body, 0)
    j = 2 * i
    late = slice(bk, tq)
    for hh in hs:
        scores(hh, j + 1, 1, late)
    for hh in hs:
        soft(hh, 0, diag_offset=0)
    for hh in hs:
        fold(hh, jnp.maximum(j - 1, 0), 1)
    for hh in hs:
        soft(hh, 1, diag_offset=bk, rows=late)
    for hh in hs:
        fold(hh, j, 0)
    for hh in hs:
        fold(hh, j + 1, 1, late)
    for hh in hs:
        acc = head_scratch(hh)[8][...]
        o_ref[:, hh * dv:(hh + 1) * dv] = (acc[:, :dv] / acc[:, dv:dv + 1]).astype(o_ref.dtype)


def _attention(q, k, v, k_mean, *, heads, dq, dv, moba):
    B, S, _ = q.shape
    tq, bk, w, nh = ATTN_Q_TILE, ATTN_KV_TILE, ATTN_WIDTH, ATTN_HEADS_PER_STEP
    assert tq == 2 * bk and tq % MOBA_BLOCK == 0 and S % tq == 0 and heads % nh == 0
    in_specs = [
        pl.BlockSpec((None, tq, nh * dq), lambda b, h, i: (b, i, h)),
        pl.BlockSpec((None, S, nh * (dq if moba else w)), lambda b, h, i: (b, 0, h)),
        pl.BlockSpec((None, S, nh * dv), lambda b, h, i: (b, 0, h)),
    ]
    args = [q, k, v]
    scratch = [pltpu.VMEM((S, nh * w), BF16)]
    if moba:
        nb = k_mean.shape[1]
        in_specs.append(pl.BlockSpec((None, nb, nh * dq), lambda b, h, i: (b, 0, h)))
        args.append(k_mean)
        scratch.insert(0, pltpu.VMEM((S, nh * w), BF16))
    for _ in range(nh):
        if moba:
            scratch.append(pltpu.VMEM((tq, w), BF16))
        scratch += [pltpu.VMEM((tq, bk), F32), pltpu.VMEM((tq, bk), F32),
                    pltpu.VMEM((tq, bk), BF16), pltpu.VMEM((tq, bk), BF16),
                    pltpu.VMEM((tq, 1), F32), pltpu.VMEM((tq, 1), F32),
                    pltpu.VMEM((tq, 1), F32), pltpu.VMEM((tq, w), F32)]
    return pl.pallas_call(
        functools.partial(_attn_kernel, moba=moba, tq=tq, bk=bk, dq=dq, dv=dv, nh=nh),
        out_shape=jax.ShapeDtypeStruct((B, S, heads * dv), BF16),
        grid=(B, heads // nh, S // tq),
        in_specs=in_specs,
        out_specs=pl.BlockSpec((None, tq, nh * dv), lambda b, h, i: (b, i, h)),
        scratch_shapes=scratch,
        compiler_params=pltpu.CompilerParams(
            dimension_semantics=("arbitrary", "arbitrary", "arbitrary"),
            vmem_limit_bytes=VMEM_LIMIT_BYTES),
        name="moba_attention" if moba else "mla_attention",
    )(*args)


def _attn_out_kernel(a_ref, wo_ref, x_ref, ga_ref, g_ref, sh_ref, sc_ref, rw_ref, rb_ref,
                     x1_ref, h_ref, loc_ref, pw_ref, seg_ref, tot_ref, run_ref):
    first_step = (pl.program_id(0) == 0) & (pl.program_id(1) == 0)

    @pl.when(first_step)
    def _():
        run_ref[...] = jnp.zeros(run_ref.shape, F32)

    a = jnp.dot(a_ref[...], wo_ref[...], preferred_element_type=F32)
    x1 = x_ref[...] + ga_ref[...] * a
    x1_ref[...] = x1
    y = x1 * _rms_scale(x1) * g_ref[...]
    h = (y * (1.0 + sc_ref[...]) + sh_ref[...]).astype(BF16)
    h_ref[...] = h
    logits = jnp.dot(h, rw_ref[...], preferred_element_type=F32) + rb_ref[...]
    t = logits.shape[0]
    lane_i = lax.broadcasted_iota(jnp.int32, logits.shape, 1)
    lane = lane_i.astype(F32)
    logits = jnp.where(lane_i < N_EXPERTS, logits, -jnp.inf)
    top = None
    den = jnp.zeros((t, 1), F32)
    sel = jnp.zeros(logits.shape, F32)
    hits, ws = [], []
    for _ in range(TOP_K):
        mx = jnp.max(logits, axis=1, keepdims=True)
        first = jnp.min(jnp.where(logits == mx, lane, float(LANES)), axis=1, keepdims=True)
        hit = lane == first
        if top is None:
            top = mx
        w = jnp.exp(mx - top)
        den = den + w
        sel = jnp.where(hit, 1.0, sel)
        logits = jnp.where(hit, -jnp.inf, logits)
        hits.append(hit)
        ws.append(w)
    rt = ROUTE_TILE
    r_id = lax.broadcasted_iota(jnp.int32, (rt, rt), 0)
    c_id = lax.broadcasted_iota(jnp.int32, (rt, rt), 1)
    before = jnp.where(c_id < r_id, 1.0, 0.0).astype(BF16)
    a_id = lax.broadcasted_iota(jnp.int32, (LANES, LANES), 0)
    b_id = lax.broadcasted_iota(jnp.int32, (LANES, LANES), 1)
    earlier = jnp.where(a_id < b_id, 1.0, 0.0).astype(BF16)
    row8 = lax.broadcasted_iota(jnp.int32, (8, LANES), 0)
    local = []
    for sub in range(t // rt):
        sel_s = sel[sub * rt:(sub + 1) * rt]
        prefix = jnp.dot(before, sel_s.astype(BF16), preferred_element_type=F32)
        cnt = jnp.sum(sel_s, axis=0, keepdims=True)
        units = jnp.floor((cnt + (SEG_ALIGN - 1.0)) * (1.0 / SEG_ALIGN))
        cnt_pad = units * SEG_ALIGN
        seg_off = jnp.dot(jnp.broadcast_to(units, (8, LANES)).astype(BF16), earlier,
                          preferred_element_type=F32)[0:1] * SEG_ALIGN
        base = run_ref[...]
        run_ref[...] = base + cnt_pad
        seg = jnp.where(row8 == 0, base, jnp.where(row8 == 1, cnt_pad, jnp.where(row8 == 2, seg_off, 0.0)))
        seg_ref[sub] = seg.astype(jnp.int32)
        local.append(prefix + seg_off)
    tot_ref[...] = run_ref[...]
    local = jnp.concatenate(local, axis=0)
    loc = jnp.zeros(logits.shape, F32)
    pw = jnp.zeros(logits.shape, F32)
    for kk in range(TOP_K):
        row = jnp.sum(jnp.where(hits[kk], local, 0.0), axis=1, keepdims=True)
        loc = jnp.where(lane_i == kk, row, loc)
        pw = jnp.where(lane_i == kk, ws[kk] / den, pw)
    loc_ref[...] = loc.astype(jnp.int32)
    pw_ref[...] = pw


def _attn_out(attn, w_o, x, g_a, norm_g, sh_f, sc_f, router_w, router_b):
    B, S, D = x.shape
    t = TOKEN_TILE
    per = S // t
    sub = t // ROUTE_TILE
    rw = jnp.pad(router_w, ((0, 0), (0, LANES - N_EXPERTS))).astype(BF16)
    rb = jnp.pad(router_b, (0, LANES - N_EXPERTS)).reshape(1, LANES)
    wo = w_o.astype(BF16)
    g2 = norm_g.reshape(1, D)
    tok = lambda w: pl.BlockSpec((None, t, w), lambda b, i: (b, i, 0))
    vec = lambda w: pl.BlockSpec((None, 1, w), lambda b, i: (b, 0, 0))
    full = lambda a: pl.BlockSpec(a.shape, lambda b, i: (0,) * a.ndim)
    return pl.pallas_call(
        _attn_out_kernel,
        out_shape=(jax.ShapeDtypeStruct((B, S, D), F32),
                   jax.ShapeDtypeStruct((B, S, D), BF16),
                   jax.ShapeDtypeStruct((B, S, LANES), jnp.int32),
                   jax.ShapeDtypeStruct((B, S, LANES), F32),
                   jax.ShapeDtypeStruct((B * per * sub, 8, LANES), jnp.int32),
                   jax.ShapeDtypeStruct((1, LANES), F32)),
        grid=(B, per),
        in_specs=[tok(attn.shape[-1]), full(wo), tok(D), vec(D), full(g2), vec(D), vec(D),
                  full(rw), full(rb)],
        out_specs=(tok(D), tok(D), tok(LANES), tok(LANES),
                   pl.BlockSpec((sub, 8, LANES), lambda b, i: (b * per + i, 0, 0)),
                   pl.BlockSpec((1, LANES), lambda b, i: (0, 0))),
        scratch_shapes=[pltpu.VMEM((1, LANES), F32)],
        compiler_params=pltpu.CompilerParams(dimension_semantics=("arbitrary", "arbitrary")),
        name="attn_out_router",
    )(attn, wo, x, g_a, g2, sh_f, sc_f, rw, rb)


def _for_each_chunk(cnt, fn):
    shift = SEG_CHUNK.bit_length() - 1
    whole = cnt >> shift

    def piece(c, carry):
        fn(pl.multiple_of(c * SEG_CHUNK, SEG_CHUNK), SEG_CHUNK)
        return carry

    lax.fori_loop(0, whole, piece, 0)
    rows = SEG_CHUNK // 2
    while rows >= SEG_ALIGN:
        shift = rows.bit_length()
        done = (cnt >> shift) << shift

        @pl.when((cnt & rows) != 0)
        def _(done=done, rows=rows):
            fn(pl.multiple_of(done, SEG_ALIGN), rows)

        rows //= 2


def _dispatch_kernel(dst_ref, cnt_ref, off_ref, pad_start_ref, pad_cnt_ref, h_ref, loc_ref,
                     xs_ref, seg_buf, zero_buf, sems):
    i = pl.program_id(0)
    last = pl.num_programs(0) - 1
    slot = lax.rem(i, 2)
    t = h_ref.shape[0]
    n_loc = seg_buf.shape[1]
    pad_sem = sems.at[2]

    @pl.when(i == 0)
    def _():
        zero_buf[...] = jnp.zeros(zero_buf.shape, F32)
        for wait in (False, True):
            def per_expert(e, carry, wait=wait):
                start = pad_start_ref[e]

                def chunk(done, rows):
                    cp = pltpu.make_async_copy(
                        zero_buf.at[pl.ds(0, rows), :],
                        xs_ref.at[pl.ds(pl.multiple_of(start + done, SEG_ALIGN), rows), :], pad_sem)
                    cp.wait() if wait else cp.start()

                _for_each_chunk(pad_cnt_ref[e], chunk)
                return carry

            lax.fori_loop(0, N_EXPERTS, per_expert, 0)

    def segment_copies(tile, buf, wait):
        def per_expert(e, carry):
            idx = tile * N_EXPERTS + e
            off = off_ref[idx]
            dst = dst_ref[idx]

            def chunk(done, rows):
                cp = pltpu.make_async_copy(
                    seg_buf.at[buf, pl.ds(pl.multiple_of(off + done, SEG_ALIGN), rows), :],
                    xs_ref.at[pl.ds(pl.multiple_of(dst + done, SEG_ALIGN), rows), :], sems.at[buf])
                cp.wait() if wait else cp.start()

            _for_each_chunk(cnt_ref[idx], chunk)
            return carry

        lax.fori_loop(0, N_EXPERTS, per_expert, 0)

    loc_t = loc_ref[...].astype(F32).T
    l_id = lax.broadcasted_iota(jnp.int32, (n_loc, t), 0).astype(F32)
    onehot = jnp.zeros((n_loc, t), F32)
    for kk in range(TOP_K):
        onehot = jnp.where(l_id == loc_t[kk:kk + 1, :], 1.0, onehot)
    seg_buf[slot] = jnp.dot(onehot.astype(BF16), h_ref[...], preferred_element_type=F32)
    segment_copies(i, slot, wait=False)

    @pl.when(i > 0)
    def _():
        segment_copies(i - 1, 1 - slot, wait=True)

    @pl.when(i == last)
    def _():
        segment_copies(i, slot, wait=True)


def _dispatch(h, loc, dst, seg_cnt, seg_off, pad_start, pad_cnt, n_rows):
    N, D = h.shape
    t = ROUTE_TILE
    return pl.pallas_call(
        _dispatch_kernel,
        out_shape=jax.ShapeDtypeStruct((n_rows, D), F32),
        grid_spec=pltpu.PrefetchScalarGridSpec(
            num_scalar_prefetch=5,
            grid=(N // t,),
            in_specs=[pl.BlockSpec((t, D), lambda i, *_: (i, 0)),
                      pl.BlockSpec((t, LANES), lambda i, *_: (i, 0))],
            out_specs=pl.BlockSpec(memory_space=pl.ANY),
            scratch_shapes=[pltpu.VMEM((2, SEG_BUF_ROWS, D), F32),
                            pltpu.VMEM((SEG_CHUNK, D), F32),
                            pltpu.SemaphoreType.DMA((3,))],
        ),
        compiler_params=pltpu.CompilerParams(dimension_semantics=("arbitrary",),
                                             vmem_limit_bytes=VMEM_LIMIT_BYTES),
        name="moe_dispatch",
    )(dst, seg_cnt, seg_off, pad_start, pad_cnt, h, loc)


def _experts_kernel(tile_e_ref, tile_blk_ref, nact_ref, xs_ref, wg_ref, bg_ref, wu_ref, bu_ref,
                    wd_ref, bd_ref, y_ref, wg_bf, wu_bf, wd_bf):
    j = pl.program_id(0)

    @pl.when(j < nact_ref[0])
    def _():
        prev = tile_e_ref[jnp.maximum(j - 1, 0)]

        @pl.when((j == 0) | (tile_e_ref[j] != prev))
        def _():
            wg_bf[...] = wg_ref[...].astype(BF16)
            wu_bf[...] = wu_ref[...].astype(BF16)
            wd_bf[...] = wd_ref[...].astype(BF16)

        x = xs_ref[...].astype(BF16)
        fw = wg_bf.shape[1] // FF_SPLIT
        y = None
        for f in range(FF_SPLIT):
            fs = slice(f * fw, (f + 1) * fw)
            g = jnp.minimum(jnp.dot(x, wg_bf[:, fs], preferred_element_type=F32) + bg_ref[:, fs],
                            SWIGLU_LIMIT)
            u = jnp.clip(jnp.dot(x, wu_bf[:, fs], preferred_element_type=F32) + bu_ref[:, fs],
                         -SWIGLU_LIMIT, SWIGLU_LIMIT)
            a = g * jax.nn.sigmoid(SWIGLU_ALPHA * g) * (u + 1.0)
            part = jnp.dot(a.astype(BF16), wd_bf[fs, :], preferred_element_type=F32)
            y = part if y is None else y + part
        y_ref[...] = y + bd_ref[...]


def _experts(xs, tile_e, tile_blk, nact, layer, w_gate, b_gate, w_up, b_up, w_down, b_down):
    P, D = xs.shape
    L, E, _, F = w_gate.shape
    tm = EXPERT_TILE
    rows = pl.BlockSpec((tm, D), lambda j, te, tb, na: (tb[j], 0))
    wspec = lambda r, c: pl.BlockSpec((None, None, r, c), lambda j, te, tb, na: (layer, te[j], 0, 0))
    return pl.pallas_call(
        _experts_kernel,
        out_shape=jax.ShapeDtypeStruct((P, D), F32),
        grid_spec=pltpu.PrefetchScalarGridSpec(
            num_scalar_prefetch=3,
            grid=(P // tm,),
            in_specs=[rows, wspec(D, F), wspec(1, F), wspec(D, F), wspec(1, F), wspec(F, D), wspec(1, D)],
            out_specs=rows,
            scratch_shapes=[pltpu.VMEM((D, F), BF16), pltpu.VMEM((D, F), BF16), pltpu.VMEM((F, D), BF16)],
        ),
        compiler_params=pltpu.CompilerParams(dimension_semantics=("arbitrary",),
                                             vmem_limit_bytes=VMEM_LIMIT_BYTES),
        name="moe_experts",
    )(tile_e, tile_blk, nact, xs, w_gate, b_gate.reshape(L, E, 1, F), w_up, b_up.reshape(L, E, 1, F),
      w_down, b_down.reshape(L, E, 1, D))


def _combine_kernel(*refs, final_norm):
    if final_norm:
        (src_ref, cnt_ref, off_ref, y_ref, loc_ref, pw_ref, x_ref, gf_ref, fg_ref, fsh_ref, fsc_ref,
         o_ref, seg_buf, sems) = refs
    else:
        src_ref, cnt_ref, off_ref, y_ref, loc_ref, pw_ref, x_ref, gf_ref, o_ref, seg_buf, sems = refs
    i = pl.program_id(0)
    last = pl.num_programs(0) - 1
    slot = lax.rem(i, 2)
    t = x_ref.shape[0]
    n_loc = seg_buf.shape[1]

    def segment_copies(tile, buf, wait):
        def per_expert(e, carry):
            idx = tile * N_EXPERTS + e
            off = off_ref[idx]
            src = src_ref[idx]

            def chunk(done, rows):
                cp = pltpu.make_async_copy(
                    y_ref.at[pl.ds(pl.multiple_of(src + done, SEG_ALIGN), rows), :],
                    seg_buf.at[buf, pl.ds(pl.multiple_of(off + done, SEG_ALIGN), rows), :], sems.at[buf])
                cp.wait() if wait else cp.start()

            _for_each_chunk(cnt_ref[idx], chunk)
            return carry

        lax.fori_loop(0, N_EXPERTS, per_expert, 0)

    @pl.when(i == 0)
    def _():
        seg_buf[...] = jnp.zeros(seg_buf.shape, F32)
        segment_copies(0, 0, wait=False)

    @pl.when(i < last)
    def _():
        segment_copies(i + 1, 1 - slot, wait=False)

    segment_copies(i, slot, wait=True)
    loc = loc_ref[...].astype(F32)
    pw = pw_ref[...]
    l_id = lax.broadcasted_iota(jnp.int32, (t, n_loc), 1).astype(F32)
    weights = jnp.zeros((t, n_loc), F32)
    for kk in range(TOP_K):
        weights = jnp.where(l_id == loc[:, kk:kk + 1], pw[:, kk:kk + 1], weights)
    f = jnp.dot(weights.astype(BF16), seg_buf[slot].astype(BF16), preferred_element_type=F32)
    x2 = x_ref[...] + gf_ref[...] * f
    if final_norm:
        y = x2 * _rms_scale(x2) * fg_ref[...]
        x2 = y * (1.0 + fsc_ref[...]) + fsh_ref[...]
    o_ref[...] = x2


def _combine(y, loc, pw, src, seg_cnt, seg_off, x, g_f, final):
    B, S, D = x.shape
    t = ROUTE_TILE
    per = S // t
    tok = lambda w: pl.BlockSpec((None, t, w), lambda i, *_: (i // per, i % per, 0))
    vec = pl.BlockSpec((None, 1, D), lambda i, *_: (i // per, 0, 0))
    in_specs = [pl.BlockSpec(memory_space=pl.ANY), tok(LANES), tok(LANES), tok(D), vec]
    args = [y, loc, pw, x, g_f]
    if final is not None:
        fg, fsh, fsc = final
        in_specs += [pl.BlockSpec((1, D), lambda i, *_: (0, 0)), vec, vec]
        args += [fg.reshape(1, D), fsh, fsc]
    return pl.pallas_call(
        functools.partial(_combine_kernel, final_norm=final is not None),
        out_shape=jax.ShapeDtypeStruct((B, S, D), F32),
        grid_spec=pltpu.PrefetchScalarGridSpec(
            num_scalar_prefetch=3,
            grid=(B * per,),
            in_specs=in_specs,
            out_specs=tok(D),
            scratch_shapes=[pltpu.VMEM((2, SEG_BUF_ROWS, D), F32), pltpu.SemaphoreType.DMA((2,))],
        ),
        compiler_params=pltpu.CompilerParams(dimension_semantics=("arbitrary",),
                                             vmem_limit_bytes=VMEM_LIMIT_BYTES),
        name="moe_combine",
    )(src, seg_cnt, seg_off, *args)


def _moe(h, loc, pw, seg, totals, x, g_f, layer, w_gate, b_gate, w_up, b_up, w_down, b_down, final):
    B, S, D = x.shape
    N = B * S
    E = N_EXPERTS
    tm = EXPERT_TILE
    n_tok_tiles = N // ROUTE_TILE
    max_rows = N * TOP_K + n_tok_tiles * E * (SEG_ALIGN - 1)
    n_tiles = -(-max_rows // tm) + E
    tot = totals[0, :E].astype(jnp.int32)
    tiles_per = (tot + tm - 1) // tm
    tile_end = jnp.cumsum(tiles_per)
    row_start = (tile_end - tiles_per) * tm
    nact = tile_end[-1:]
    jj = jnp.minimum(jnp.arange(n_tiles, dtype=jnp.int32), nact[0] - 1)
    tile_e = jnp.sum((jj[:, None] >= tile_end[None, :]).astype(jnp.int32), axis=1)
    place = (row_start[None, :] + seg[:, 0, :E]).reshape(-1)
    seg_cnt = seg[:, 1, :E].reshape(-1)
    seg_off = seg[:, 2, :E].reshape(-1)
    loc2 = loc.reshape(N, LANES)
    xs = _dispatch(h.reshape(N, D), loc2, place, seg_cnt, seg_off, row_start + tot,
                   tiles_per * tm - tot, n_tiles * tm)
    y = _experts(xs, tile_e, jj, nact, layer, w_gate, b_gate, w_up, b_up, w_down, b_down)
    return _combine(y, loc, pw, place, seg_cnt, seg_off, x, g_f, final)


def _moba_proj_kernel(x_ref, sha_ref, sca_ref, ga_ref, shk_ref, sck_ref, gk_ref, wq_ref, wkv_ref,
                      cos_ref, sin_ref, q_ref, k_ref, v_ref, km_ref):
    x = x_ref[...]
    xn = x * _rms_scale(x)
    h = ((xn * ga_ref[...]) * (1.0 + sca_ref[...]) + sha_ref[...]).astype(BF16)
    hkv = ((xn * gk_ref[...]) * (1.0 + sck_ref[...]) + shk_ref[...]).astype(BF16)
    q = jnp.dot(h, wq_ref[...], preferred_element_type=F32)
    kv = jnp.dot(hkv, wkv_ref[...], preferred_element_type=F32)
    cos = cos_ref[...]
    sin = sin_ref[...]
    scale = LOG2_E * MOBA_HEAD_DIM ** -0.5
    hd = MOBA_HEAD_DIM
    t = x.shape[0]
    width = MOBA_HEADS * hd
    v_ref[...] = kv[:, width:].astype(BF16)
    for hh in range(MOBA_HEADS):
        hs = slice(hh * hd, (hh + 1) * hd)
        qh = q[:, hs]
        kh = kv[:, hs]
        q_ref[:, hs] = ((qh * cos + pltpu.roll(qh, hd // 2, 1) * sin) * scale).astype(BF16)
        kr = kh * cos + pltpu.roll(kh, hd // 2, 1) * sin
        k_ref[:, hs] = kr.astype(BF16)
        km_ref[:, hs] = jnp.mean(kr.reshape(t // MOBA_BLOCK, MOBA_BLOCK, hd), axis=1)


def _moba_proj(x, sh_a, sc_a, g_a, sh_k, sc_k, g_k, w_q, w_kv, cos, sin):
    B, S, D = x.shape
    t = TOKEN_TILE
    H = MOBA_HEADS
    width = H * MOBA_HEAD_DIM
    per = t // MOBA_BLOCK
    assert S // MOBA_BLOCK <= LANES
    tok = lambda w: pl.BlockSpec((None, t, w), lambda b, i: (b, i, 0))
    vec = lambda w: pl.BlockSpec((None, 1, w), lambda b, i: (b, 0, 0))
    full = lambda a: pl.BlockSpec(a.shape, lambda b, i: (0,) * a.ndim)
    ga2 = g_a.reshape(1, D)
    gk2 = g_k.reshape(1, D)
    wq = w_q.astype(BF16)
    wkv = w_kv.astype(BF16)
    rows = jax.ShapeDtypeStruct((B, S, width), BF16)
    q, k, v, km = pl.pallas_call(
        _moba_proj_kernel,
        out_shape=(rows, rows, rows, jax.ShapeDtypeStruct((B, S // t, per, width), F32)),
        grid=(B, S // t),
        in_specs=[tok(D), vec(D), vec(D), full(ga2), vec(D), vec(D), full(gk2), full(wq), full(wkv),
                  tok(LANES), tok(LANES)],
        out_specs=(tok(width), tok(width), tok(width),
                   pl.BlockSpec((None, None, per, width), lambda b, i: (b, i, 0, 0))),
        name="moba_proj",
    )(x, sh_a, sc_a, ga2, sh_k, sc_k, gk2, wq, wkv, cos, sin)
    return q, k, v, km.reshape(B, S // MOBA_BLOCK, width)


def _split_mod(mod, n):
    return [mod[:, None, j * D_MODEL:(j + 1) * D_MODEL] for j in range(n)]


def kernel(x, c, positions, ada_w, ada_b, norm_attn_g, norm_ffn_g, mla_w_in, mla_q_norm_g, mla_w_uq, mla_kv_norm_g, mla_w_ukv, mla_w_o, kv_ada_w, kv_ada_b, kv_norm_g, moba_w_kv, moba_w_q, moba_w_o, router_w, router_b, w_gate, b_gate, w_up, b_up, w_down, b_down, final_ada_w, final_ada_b, final_norm_g):
    mods = _ada_linear(c, ada_w, ada_b)
    kv_mod = _ada_linear(c, kv_ada_w[None], kv_ada_b[None])[0]
    f_mod = _ada_linear(c, final_ada_w[None], final_ada_b[None])[0]
    cos_a, sin_a, cos_b, sin_b = _rope_tables(positions)
    f_sh, f_sc = _split_mod(f_mod, 2)
    shared = None
    for layer in range(DEPTH):
        sh_a, sc_a, g_a, sh_f, sc_f, g_f = _split_mod(mods[layer], 6)
        if layer < N_A:
            q, k, v = _mla_proj(x, sh_a, sc_a, norm_attn_g[layer], mla_w_in[layer], mla_q_norm_g[layer],
                                mla_w_uq[layer], mla_kv_norm_g[layer], mla_w_ukv[layer], cos_a, sin_a)
            attn = _attention(q, k, v, None, heads=MLA_HEADS, dq=ATTN_WIDTH, dv=MLA_V, moba=False)
            w_o = mla_w_o[layer]
        else:
            j = layer - N_A
            kv_sh, kv_sc = _split_mod(kv_mod, 2)
            q, k, v, km = _moba_proj(x, sh_a, sc_a, norm_attn_g[layer], kv_sh, kv_sc, kv_norm_g,
                                     moba_w_q[j], moba_w_kv, cos_b, sin_b)
            if shared is None:
                shared = (k, v, km)
            attn = _attention(q, shared[0], shared[1], shared[2], heads=MOBA_HEADS, dq=MOBA_HEAD_DIM,
                              dv=MOBA_HEAD_DIM, moba=True)
            w_o = moba_w_o[j]
        x, h, loc, pw, seg, totals = _attn_out(attn, w_o, x, g_a, norm_ffn_g[layer], sh_f, sc_f,
                                               router_w[layer], router_b[layer])
        final = (final_norm_g, f_sh, f_sc) if layer == DEPTH - 1 else None
        x = _moe(h, loc, pw, seg, totals, x, g_f, layer, w_gate, b_gate, w_up, b_up, w_down, b_down,
                 final)
    return x
```

```python
import functools

import jax
import jax.numpy as jnp
from jax import lax
from jax.experimental import pallas as pl
from jax.experimental.pallas import tpu as pltpu

D_MODEL = 1024
DEPTH = 2
N_A = DEPTH // 2

MLA_HEADS = 8
MLA_Q_LORA = 256
MLA_KV_LORA = 128
MLA_NOPE = 128
MLA_ROPE = 64
MLA_V = 128

MOBA_HEADS = 8
MOBA_HEAD_DIM = D_MODEL // MOBA_HEADS
MOBA_BLOCK = 256
MOBA_TOPK = 3

N_EXPERTS = 32
TOP_K = 4
SWIGLU_LIMIT = 7.0
SWIGLU_ALPHA = 1.702

ROPE_THETA = 10000.0
NORM_EPS = 1e-6
NEG_INF = -1e30
LOG2_E = 1.4426950408889634
SOFTMAX_M_INIT = -1e29

LANES = 128
TOKEN_TILE = 1024
ATTN_Q_TILE = 1024
ATTN_KV_TILE = 512
ATTN_WIDTH = 2 * LANES
ATTN_HEADS_PER_STEP = 2
EXPERT_TILE = 512
SEG_ALIGN = 8
SEG_CHUNK = 64
ROUTE_TILE = 512
SEG_BUF_ROWS = ROUTE_TILE * TOP_K + N_EXPERTS * SEG_ALIGN
VMEM_LIMIT_BYTES = 56 * 1024 * 1024

F32 = jnp.float32
BF16 = jnp.bfloat16


def _rms_scale(x):
    return lax.rsqrt(jnp.mean(x * x, axis=-1, keepdims=True) + NORM_EPS)


def _nt_dot(a, b):
    return lax.dot_general(a, b, (((1,), (1,)), ((), ())), preferred_element_type=F32)


def _ones_column(rows):
    lane = lax.broadcasted_iota(jnp.int32, (rows, LANES), 1)
    return jnp.where(lane == 0, 1.0, 0.0).astype(BF16)


def _ada_kernel(c_ref, w_ref, b_ref, o_ref):
    c = c_ref[...]
    ca = c * jax.nn.sigmoid(c)
    o_ref[...] = jnp.dot(ca, w_ref[...], precision=lax.Precision.HIGHEST,
                         preferred_element_type=F32) + b_ref[...]


def _ada_linear(c, w, b):
    L, D, M = w.shape
    B = c.shape[0]
    bn = 2048
    return pl.pallas_call(
        _ada_kernel,
        out_shape=jax.ShapeDtypeStruct((L, B, M), F32),
        grid=(L, M // bn),
        in_specs=[
            pl.BlockSpec((B, D), lambda l, j: (0, 0)),
            pl.BlockSpec((None, D, bn), lambda l, j: (l, 0, j)),
            pl.BlockSpec((None, 1, bn), lambda l, j: (l, 0, j)),
        ],
        out_specs=pl.BlockSpec((None, B, bn), lambda l, j: (l, 0, j)),
        name="ada_linear",
    )(c, w, b.reshape(L, 1, M))


def _rope_table_kernel(pos_ref, inv_ref, sign_ref, ca_ref, sa_ref, cb_ref, sb_ref):
    ang = pos_ref[...] * inv_ref[...]
    lane = lax.broadcasted_iota(jnp.int32, ang.shape, 1)
    na, nb = MLA_ROPE // 2, MOBA_HEAD_DIM // 2

    def spread(v):
        a = jnp.where(lane < na, v, 0.0)
        a = a + pltpu.roll(a, na, 1)
        a = a + pltpu.roll(a, 2 * na, 1)
        b = jnp.where((lane >= na) & (lane < na + nb), v, 0.0)
        b = pltpu.roll(b, LANES - na, 1)
        b = b + pltpu.roll(b, nb, 1)
        return a, b

    ca_ref[...], cb_ref[...] = spread(jnp.cos(ang))
    sa, sb = spread(jnp.sin(ang))
    sa_ref[...] = sa
    sb_ref[...] = sb * sign_ref[...]


def _rope_tables(positions):
    B, S = positions.shape
    t = 1024
    assert MLA_ROPE // 2 + MOBA_HEAD_DIM // 2 <= LANES and MLA_ROPE * 2 == LANES == MOBA_HEAD_DIM

    def inv_freq(d):
        half = d // 2
        return ROPE_THETA ** (-jnp.arange(half, dtype=F32) * (2.0 / d))

    inv = jnp.concatenate([inv_freq(MLA_ROPE), inv_freq(MOBA_HEAD_DIM),
                           jnp.zeros((LANES - MLA_ROPE // 2 - MOBA_HEAD_DIM // 2,), F32)]).reshape(1, LANES)
    half = MOBA_HEAD_DIM // 2
    sign = jnp.concatenate([-jnp.ones((half,), F32), jnp.ones((half,), F32)]).reshape(1, LANES)
    pos = positions.astype(F32).reshape(B, S, 1)
    row = pl.BlockSpec((1, LANES), lambda b, i: (0, 0))
    out = pl.BlockSpec((None, t, LANES), lambda b, i: (b, i, 0))
    shp = jax.ShapeDtypeStruct((B, S, LANES), F32)
    return pl.pallas_call(
        _rope_table_kernel,
        out_shape=(shp, shp, shp, shp),
        grid=(B, S // t),
        in_specs=[pl.BlockSpec((None, t, 1), lambda b, i: (b, i, 0)), row, row],
        out_specs=(out, out, out, out),
        name="rope_tables",
    )(pos, inv, sign)


def _mla_proj_kernel(x_ref, sh_ref, sc_ref, g_ref, win_ref, qg_ref, wqn_ref, wqr_ref, wqrr_ref,
                     kvg_ref, wkn_ref, wv_ref, cos_ref, sin_ref, q_ref, k_ref, v_ref):
    x = x_ref[...]
    y = x * _rms_scale(x) * g_ref[...]
    h = (y * (1.0 + sc_ref[...]) + sh_ref[...]).astype(BF16)
    proj = jnp.dot(h, win_ref[...], preferred_element_type=F32)
    c_q = proj[:, :MLA_Q_LORA]
    c_kv = proj[:, MLA_Q_LORA:MLA_Q_LORA + MLA_KV_LORA]
    kr_a = proj[:, 384:512]
    kr_b = proj[:, 512:640]
    cq = (c_q * _rms_scale(c_q) * qg_ref[...]).astype(BF16)
    ckv = (c_kv * _rms_scale(c_kv) * kvg_ref[...]).astype(BF16)
    cos = cos_ref[...]
    sin = sin_ref[...]
    scale = LOG2_E * (MLA_NOPE + MLA_ROPE) ** -0.5
    kr = (kr_a * cos + kr_b * sin).astype(BF16)
    q_nope = jnp.dot(cq, wqn_ref[...], preferred_element_type=F32) * scale
    q_ra = jnp.dot(cq, wqr_ref[...], preferred_element_type=F32)
    q_rb = jnp.dot(cq, wqrr_ref[...], preferred_element_type=F32)
    k_nope = jnp.dot(ckv, wkn_ref[...], preferred_element_type=F32)
    v_ref[...] = jnp.dot(ckv, wv_ref[...], preferred_element_type=F32).astype(BF16)
    for hh in range(MLA_HEADS):
        hs = slice(hh * LANES, (hh + 1) * LANES)
        lo = slice(hh * ATTN_WIDTH, hh * ATTN_WIDTH + LANES)
        hi = slice(hh * ATTN_WIDTH + LANES, (hh + 1) * ATTN_WIDTH)
        q_ref[:, lo] = q_nope[:, hs].astype(BF16)
        q_ref[:, hi] = ((q_ra[:, hs] * cos + q_rb[:, hs] * sin) * scale).astype(BF16)
        k_ref[:, lo] = k_nope[:, hs].astype(BF16)
        k_ref[:, hi] = kr


def _rot_half_cols(w, half):
    return jnp.concatenate([-w[..., half:], w[..., :half]], axis=-1)


def _mla_proj(x, sh, sc, g, w_in, q_norm_g, w_uq, kv_norm_g, w_ukv, cos, sin):
    B, S, D = x.shape
    H = MLA_HEADS
    t = TOKEN_TILE
    half = MLA_ROPE // 2
    w_kr = w_in[:, MLA_Q_LORA + MLA_KV_LORA:]
    zpad = jnp.zeros((D, LANES - MLA_ROPE), F32)
    w_in_ext = jnp.concatenate(
        [w_in[:, :MLA_Q_LORA + MLA_KV_LORA], w_kr, zpad, _rot_half_cols(w_kr, half), zpad],
        axis=1).astype(BF16)
    wq = w_uq.reshape(MLA_Q_LORA, H, MLA_NOPE + MLA_ROPE)
    wq_nope = wq[..., :MLA_NOPE].reshape(MLA_Q_LORA, H * MLA_NOPE).astype(BF16)
    wq_r = wq[..., MLA_NOPE:]
    pad = ((0, 0), (0, 0), (0, LANES - MLA_ROPE))
    wq_rope = jnp.pad(wq_r, pad).reshape(MLA_Q_LORA, H * LANES).astype(BF16)
    wq_rope_rot = jnp.pad(_rot_half_cols(wq_r, half), pad).reshape(MLA_Q_LORA, H * LANES).astype(BF16)
    wkv = w_ukv.reshape(MLA_KV_LORA, H, MLA_NOPE + MLA_V)
    wk_nope = wkv[..., :MLA_NOPE].reshape(MLA_KV_LORA, H * MLA_NOPE).astype(BF16)
    wv = wkv[..., MLA_NOPE:].reshape(MLA_KV_LORA, H * MLA_V).astype(BF16)

    tok = lambda w: pl.BlockSpec((None, t, w), lambda b, i: (b, i, 0))
    vec = lambda w: pl.BlockSpec((None, 1, w), lambda b, i: (b, 0, 0))
    full = lambda a: pl.BlockSpec(a.shape, lambda b, i: (0,) * a.ndim)
    g2 = g.reshape(1, D)
    qg2 = q_norm_g.reshape(1, MLA_Q_LORA)
    kvg2 = kv_norm_g.reshape(1, MLA_KV_LORA)
    wide = jax.ShapeDtypeStruct((B, S, H * ATTN_WIDTH), BF16)
    return pl.pallas_call(
        _mla_proj_kernel,
        out_shape=(wide, wide, jax.ShapeDtypeStruct((B, S, H * MLA_V), BF16)),
        grid=(B, S // t),
        in_specs=[tok(D), vec(D), vec(D), full(g2), full(w_in_ext), full(qg2), full(wq_nope),
                  full(wq_rope), full(wq_rope_rot), full(kvg2), full(wk_nope), full(wv),
                  tok(LANES), tok(LANES)],
        out_specs=(tok(H * ATTN_WIDTH), tok(H * ATTN_WIDTH), tok(H * MLA_V)),
        name="mla_proj",
    )(x, sh, sc, g2, w_in_ext, qg2, wq_nope, wq_rope, wq_rope_rot, kvg2, wk_nope, wv, cos, sin)


def _attn_kernel(*refs, moba, tq, bk, dq, dv, nh):
    n_in = 4 if moba else 3
    q_ref, k_ref, v_ref = refs[:3]
    o_ref = refs[n_in]
    scratch = refs[n_in + 1:]
    if moba:
        kx_ref, vx_ref = scratch[:2]
        scratch = scratch[2:]
    else:
        kx_ref, vx_ref = k_ref, scratch[0]
        scratch = scratch[1:]
    per_head = len(scratch) // nh
    w = ATTN_WIDTH
    i = pl.program_id(2)
    seq = vx_ref.shape[0]
    row_id = lax.broadcasted_iota(jnp.int32, (tq, 1), 0)
    col_id = lax.broadcasted_iota(jnp.int32, (1, bk), 1)

    @pl.when((pl.program_id(0) == 0) & (pl.program_id(1) == 0) & (i == 0))
    def _():
        ones = _ones_column(seq)
        if moba:
            key_row = lax.broadcasted_iota(jnp.int32, (seq, LANES), 0)
            lane = lax.broadcasted_iota(jnp.int32, (seq, LANES), 1)
            block_id = jnp.where(key_row // MOBA_BLOCK == lane, 1.0, 0.0).astype(BF16)
        for hh in range(nh):
            vx_ref[:, hh * w + dv:(hh + 1) * w] = ones
            if moba:
                kx_ref[:, hh * w + dq:(hh + 1) * w] = block_id

    @pl.when(i == 0)
    def _():
        for hh in range(nh):
            vx_ref[:, hh * w:hh * w + dv] = v_ref[:, hh * dv:(hh + 1) * dv]
            if moba:
                kx_ref[:, hh * w:hh * w + dq] = k_ref[:, hh * dq:(hh + 1) * dq]

    def head_scratch(hh):
        sc = scratch[hh * per_head:(hh + 1) * per_head]
        return sc if moba else (None,) + tuple(sc)

    def prepare(hh):
        qx_ref, s_a, s_b, p_a, p_b, al_a, al_b, m_sc, acc_sc = head_scratch(hh)
        if moba:
            km_ref = refs[3]
            q = q_ref[:, hh * dq:(hh + 1) * dq]
            nb = km_ref.shape[0]
            gate = _nt_dot(km_ref[:, hh * dq:(hh + 1) * dq].astype(BF16), q)
            blk = lax.broadcasted_iota(jnp.int32, (nb, tq), 0)
            blk_f = blk.astype(F32)
            own = (i * tq + lax.broadcasted_iota(jnp.int32, (1, tq), 1)) // MOBA_BLOCK
            gate = jnp.where(blk < own, gate, -jnp.inf)
            sel = jnp.where(blk == own, 1.0, 0.0)
            for _ in range(MOBA_TOPK):
                mx = jnp.max(gate, axis=0, keepdims=True)
                first = jnp.min(jnp.where(gate == mx, blk_f, float(nb)), axis=0, keepdims=True)
                hit = blk_f == first
                sel = jnp.where(hit & (mx > -jnp.inf), 1.0, sel)
                gate = jnp.where(hit, -jnp.inf, gate)
            bias = jnp.where(sel > 0.0, 0.0, NEG_INF)
            bias = jnp.concatenate([bias, jnp.zeros((LANES - nb, tq), F32)], axis=0)
            qx_ref[:, :LANES] = q
            qx_ref[:, LANES:] = bias.T.astype(BF16)
        m_sc[...] = jnp.full(m_sc.shape, SOFTMAX_M_INIT, F32)
        acc_sc[...] = jnp.zeros(acc_sc.shape, F32)
        p_b[...] = jnp.zeros(p_b.shape, BF16)
        al_b[...] = jnp.ones(al_b.shape, F32)

    def scores(hh, j, slot, rows=slice(None)):
        sc = head_scratch(hh)
        qx = sc[0][rows, :] if moba else q_ref[rows, hh * w:(hh + 1) * w]
        start = pl.multiple_of(j * bk, bk)
        sc[1 + slot][rows, :] = _nt_dot(qx, kx_ref[pl.ds(start, bk), hh * w:(hh + 1) * w])

    def soft(hh, slot, diag_offset=None, rows=slice(None)):
        sc = head_scratch(hh)
        s_src, p_dst, al_dst, m_sc = sc[1 + slot], sc[3 + slot], sc[5 + slot], sc[7]
        s = s_src[rows, :]
        if diag_offset is not None:
            s = jnp.where(col_id + diag_offset <= row_id[rows, :], s, -jnp.inf)
        m_prev = m_sc[rows, :]
        m_new = jnp.maximum(m_prev, jnp.max(s, axis=1, keepdims=True))
        p_dst[rows, :] = jnp.exp2(s - m_new).astype(BF16)
        al_dst[rows, :] = jnp.exp2(m_prev - m_new)
        m_sc[rows, :] = m_new

    def fold(hh, j, slot, rows=slice(None)):
        sc = head_scratch(hh)
        p_src, al_src, acc_sc = sc[3 + slot], sc[5 + slot], sc[8]
        start = pl.multiple_of(j * bk, bk)
        acc_sc[rows, :] = al_src[rows, :] * acc_sc[rows, :] + jnp.dot(
            p_src[rows, :], vx_ref[pl.ds(start, bk), hh * w:(hh + 1) * w], preferred_element_type=F32)

    hs = range(nh)
    for hh in hs:
        prepare(hh)
    for hh in hs:
        scores(hh, 0, 0)

    def body(t, carry):
        j = 2 * t
        for hh in hs:
            scores(hh, j + 1, 1)
        for hh in hs:
            soft(hh, 0)
        for hh in hs:
            fold(hh, jnp.maximum(j - 1, 0), 1)
        for hh in hs:
            scores(hh, j + 2, 0)
        for hh in hs:
            soft(hh, 1)
        for hh in hs:
            fold(hh, j, 0)
        return carry

    lax.fori_loop(0, i, body, 0)
    j = 2 * i
    late = slice(bk, tq)
    for hh in hs:
        scores(hh, j + 1, 1, late)
    for hh in hs:
        soft(hh, 0, diag_offset=0)
    for hh in hs:
        fold(hh, jnp.maximum(j - 1, 0), 1)
    for hh in hs:
        soft(hh, 1, diag_offset=bk, rows=late)
    for hh in hs:
        fold(hh, j, 0)
    for hh in hs:
        fold(hh, j + 1, 1, late)
    for hh in hs:
        acc = head_scratch(hh)[8][...]
        o_ref[:, hh * dv:(hh + 1) * dv] = (acc[:, :dv] / acc[:, dv:dv + 1]).astype(o_ref.dtype)


def _attention(q, k, v, k_mean, *, heads, dq, dv, moba):
    B, S, _ = q.shape
    tq, bk, w, nh = ATTN_Q_TILE, ATTN_KV_TILE, ATTN_WIDTH, ATTN_HEADS_PER_STEP
    assert tq == 2 * bk and tq % MOBA_BLOCK == 0 and S % tq == 0 and heads % nh == 0
    in_specs = [
        pl.BlockSpec((None, tq, nh * dq), lambda b, h, i: (b, i, h)),
        pl.BlockSpec((None, S, nh * (dq if moba else w)), lambda b, h, i: (b, 0, h)),
        pl.BlockSpec((None, S, nh * dv), lambda b, h, i: (b, 0, h)),
    ]
    args = [q, k, v]
    scratch = [pltpu.VMEM((S, nh * w), BF16)]
    if moba:
        nb = k_mean.shape[1]
        in_specs.append(pl.BlockSpec((None, nb, nh * dq), lambda b, h, i: (b, 0, h)))
        args.append(k_mean)
        scratch.insert(0, pltpu.VMEM((S, nh * w), BF16))
    for _ in range(nh):
        if moba:
            scratch.append(pltpu.VMEM((tq, w), BF16))
        scratch += [pltpu.VMEM((tq, bk), F32), pltpu.VMEM((tq, bk), F32),
                    pltpu.VMEM((tq, bk), BF16), pltpu.VMEM((tq, bk), BF16),
                    pltpu.VMEM((tq, 1), F32), pltpu.VMEM((tq, 1), F32),
                    pltpu.VMEM((tq, 1), F32), pltpu.VMEM((tq, w), F32)]
    return pl.pallas_call(
        functools.partial(_attn_kernel, moba=moba, tq=tq, bk=bk, dq=dq, dv=dv, nh=nh),
        out_shape=jax.ShapeDtypeStruct((B, S, heads * dv), BF16),
        grid=(B, heads // nh, S // tq),
        in_specs=in_specs,
        out_specs=pl.BlockSpec((None, tq, nh * dv), lambda b, h, i: (b, i, h)),
        scratch_shapes=scratch,
        compiler_params=pltpu.CompilerParams(
            dimension_semantics=("arbitrary", "arbitrary", "arbitrary"),
            vmem_limit_bytes=VMEM_LIMIT_BYTES),
        name="moba_attention" if moba else "mla_attention",
    )(*args)


def _attn_out_kernel(a_ref, wo_ref, x_ref, ga_ref, g_ref, sh_ref, sc_ref, rw_ref, rb_ref,
                     x1_ref, h_ref, loc_ref, pw_ref, seg_ref, tot_ref, run_ref):
    first_step = (pl.program_id(0) == 0) & (pl.program_id(1) == 0)

    @pl.when(first_step)
    def _():
        run_ref[...] = jnp.zeros(run_ref.shape, F32)

    a = jnp.dot(a_ref[...], wo_ref[...], preferred_element_type=F32)
    x1 = x_ref[...] + ga_ref[...] * a
    x1_ref[...] = x1
    y = x1 * _rms_scale(x1) * g_ref[...]
    h = (y * (1.0 + sc_ref[...]) + sh_ref[...]).astype(BF16)
    h_ref[...] = h
    logits = jnp.dot(h, rw_ref[...], preferred_element_type=F32) + rb_ref[...]
    t = logits.shape[0]
    lane_i = lax.broadcasted_iota(jnp.int32, logits.shape, 1)
    lane = lane_i.astype(F32)
    logits = jnp.where(lane_i < N_EXPERTS, logits, -jnp.inf)
    top = None
    den = jnp.zeros((t, 1), F32)
    sel = jnp.zeros(logits.shape, F32)
    hits, ws = [], []
    for _ in range(TOP_K):
        mx = jnp.max(logits, axis=1, keepdims=True)
        first = jnp.min(jnp.where(logits == mx, lane, float(LANES)), axis=1, keepdims=True)
        hit = lane == first
        if top is None:
            top = mx
        w = jnp.exp(mx - top)
        den = den + w
        sel = jnp.where(hit, 1.0, sel)
        logits = jnp.where(hit, -jnp.inf, logits)
        hits.append(hit)
        ws.append(w)
    rt = ROUTE_TILE
    r_id = lax.broadcasted_iota(jnp.int32, (rt, rt), 0)
    c_id = lax.broadcasted_iota(jnp.int32, (rt, rt), 1)
    before = jnp.where(c_id < r_id, 1.0, 0.0).astype(BF16)
    a_id = lax.broadcasted_iota(jnp.int32, (LANES, LANES), 0)
    b_id = lax.broadcasted_iota(jnp.int32, (LANES, LANES), 1)
    earlier = jnp.where(a_id < b_id, 1.0, 0.0).astype(BF16)
    row8 = lax.broadcasted_iota(jnp.int32, (8, LANES), 0)
    local = []
    for sub in range(t // rt):
        sel_s = sel[sub * rt:(sub + 1) * rt]
        prefix = jnp.dot(before, sel_s.astype(BF16), preferred_element_type=F32)
        cnt = jnp.sum(sel_s, axis=0, keepdims=True)
        units = jnp.floor((cnt + (SEG_ALIGN - 1.0)) * (1.0 / SEG_ALIGN))
        cnt_pad = units * SEG_ALIGN
        seg_off = jnp.dot(jnp.broadcast_to(units, (8, LANES)).astype(BF16), earlier,
                          preferred_element_type=F32)[0:1] * SEG_ALIGN
        base = run_ref[...]
        run_ref[...] = base + cnt_pad
        seg = jnp.where(row8 == 0, base, jnp.where(row8 == 1, cnt_pad, jnp.where(row8 == 2, seg_off, 0.0)))
        seg_ref[sub] = seg.astype(jnp.int32)
        local.append(prefix + seg_off)
    tot_ref[...] = run_ref[...]
    local = jnp.concatenate(local, axis=0)
    loc = jnp.zeros(logits.shape, F32)
    pw = jnp.zeros(logits.shape, F32)
    for kk in range(TOP_K):
        row = jnp.sum(jnp.where(hits[kk], local, 0.0), axis=1, keepdims=True)
        loc = jnp.where(lane_i == kk, row, loc)
        pw = jnp.where(lane_i == kk, ws[kk] / den, pw)
    loc_ref[...] = loc.astype(jnp.int32)
    pw_ref[...] = pw


def _attn_out(attn, w_o, x, g_a, norm_g, sh_f, sc_f, router_w, router_b):
    B, S, D = x.shape
    t = TOKEN_TILE
    per = S // t
    sub = t // ROUTE_TILE
    rw = jnp.pad(router_w, ((0, 0), (0, LANES - N_EXPERTS))).astype(BF16)
    rb = jnp.pad(router_b, (0, LANES - N_EXPERTS)).reshape(1, LANES)
    wo = w_o.astype(BF16)
    g2 = norm_g.reshape(1, D)
    tok = lambda w: pl.BlockSpec((None, t, w), lambda b, i: (b, i, 0))
    vec = lambda w: pl.BlockSpec((None, 1, w), lambda b, i: (b, 0, 0))
    full = lambda a: pl.BlockSpec(a.shape, lambda b, i: (0,) * a.ndim)
    return pl.pallas_call(
        _attn_out_kernel,
        out_shape=(jax.ShapeDtypeStruct((B, S, D), F32),
                   jax.ShapeDtypeStruct((B, S, D), BF16),
                   jax.ShapeDtypeStruct((B, S, LANES), jnp.int32),
                   jax.ShapeDtypeStruct((B, S, LANES), F32),
                   jax.ShapeDtypeStruct((B * per * sub, 8, LANES), jnp.int32),
                   jax.ShapeDtypeStruct((1, LANES), F32)),
        grid=(B, per),
        in_specs=[tok(attn.shape[-1]), full(wo), tok(D), vec(D), full(g2), vec(D), vec(D),
                  full(rw), full(rb)],
        out_specs=(tok(D), tok(D), tok(LANES), tok(LANES),
                   pl.BlockSpec((sub, 8, LANES), lambda b, i: (b * per + i, 0, 0)),
                   pl.BlockSpec((1, LANES), lambda b, i: (0, 0))),
        scratch_shapes=[pltpu.VMEM((1, LANES), F32)],
        compiler_params=pltpu.CompilerParams(dimension_semantics=("arbitrary", "arbitrary")),
        name="attn_out_router",
    )(attn, wo, x, g_a, g2, sh_f, sc_f, rw, rb)


def _for_each_chunk(cnt, fn):
    shift = SEG_CHUNK.bit_length() - 1
    whole = cnt >> shift

    def piece(c, carry):
        fn(pl.multiple_of(c * SEG_CHUNK, SEG_CHUNK), SEG_CHUNK)
        return carry

    lax.fori_loop(0, whole, piece, 0)
    rows = SEG_CHUNK // 2
    while rows >= SEG_ALIGN:
        shift = rows.bit_length()
        done = (cnt >> shift) << shift

        @pl.when((cnt & rows) != 0)
        def _(done=done, rows=rows):
            fn(pl.multiple_of(done, SEG_ALIGN), rows)

        rows //= 2


def _dispatch_kernel(dst_ref, cnt_ref, off_ref, pad_start_ref, pad_cnt_ref, h_ref, loc_ref,
                     xs_ref, seg_buf, zero_buf, sems):
    i = pl.program_id(0)
    last = pl.num_programs(0) - 1
    slot = lax.rem(i, 2)
    t = h_ref.shape[0]
    n_loc = seg_buf.shape[1]
    pad_sem = sems.at[2]

    @pl.when(i == 0)
    def _():
        zero_buf[...] = jnp.zeros(zero_buf.shape, F32)
        for wait in (False, True):
            def per_expert(e, carry, wait=wait):
                start = pad_start_ref[e]

                def chunk(done, rows):
                    cp = pltpu.make_async_copy(
                        zero_buf.at[pl.ds(0, rows), :],
                        xs_ref.at[pl.ds(pl.multiple_of(start + done, SEG_ALIGN), rows), :], pad_sem)
                    cp.wait() if wait else cp.start()

                _for_each_chunk(pad_cnt_ref[e], chunk)
                return carry

            lax.fori_loop(0, N_EXPERTS, per_expert, 0)

    def segment_copies(tile, buf, wait):
        def per_expert(e, carry):
            idx = tile * N_EXPERTS + e
            off = off_ref[idx]
            dst = dst_ref[idx]

            def chunk(done, rows):
                cp = pltpu.make_async_copy(
                    seg_buf.at[buf, pl.ds(pl.multiple_of(off + done, SEG_ALIGN), rows), :],
                    xs_ref.at[pl.ds(pl.multiple_of(dst + done, SEG_ALIGN), rows), :], sems.at[buf])
                cp.wait() if wait else cp.start()

            _for_each_chunk(cnt_ref[idx], chunk)
            return carry

        lax.fori_loop(0, N_EXPERTS, per_expert, 0)

    loc_t = loc_ref[...].astype(F32).T
    l_id = lax.broadcasted_iota(jnp.int32, (n_loc, t), 0).astype(F32)
    onehot = jnp.zeros((n_loc, t), F32)
    for kk in range(TOP_K):
        onehot = jnp.where(l_id == loc_t[kk:kk + 1, :], 1.0, onehot)
    seg_buf[slot] = jnp.dot(onehot.astype(BF16), h_ref[...], preferred_element_type=F32)
    segment_copies(i, slot, wait=False)

    @pl.when(i > 0)
    def _():
        segment_copies(i - 1, 1 - slot, wait=True)

    @pl.when(i == last)
    def _():
        segment_copies(i, slot, wait=True)


def _dispatch(h, loc, dst, seg_cnt, seg_off, pad_start, pad_cnt, n_rows):
    N, D = h.shape
    t = ROUTE_TILE
    return pl.pallas_call(
        _dispatch_kernel,
        out_shape=jax.ShapeDtypeStruct((n_rows, D), F32),
        grid_spec=pltpu.PrefetchScalarGridSpec(
            num_scalar_prefetch=5,
            grid=(N // t,),
            in_specs=[pl.BlockSpec((t, D), lambda i, *_: (i, 0)),
                      pl.BlockSpec((t, LANES), lambda i, *_: (i, 0))],
            out_specs=pl.BlockSpec(memory_space=pl.ANY),
            scratch_shapes=[pltpu.VMEM((2, SEG_BUF_ROWS, D), F32),
                            pltpu.VMEM((SEG_CHUNK, D), F32),
                            pltpu.SemaphoreType.DMA((3,))],
        ),
        compiler_params=pltpu.CompilerParams(dimension_semantics=("arbitrary",),
                                             vmem_limit_bytes=VMEM_LIMIT_BYTES),
        name="moe_dispatch",
    )(dst, seg_cnt, seg_off, pad_start, pad_cnt, h, loc)


def _experts_kernel(tile_e_ref, tile_blk_ref, nact_ref, next_e_ref, xs_ref, wg_hbm, bg_ref, wu_hbm,
                    bu_ref, wd_hbm, bd_ref, y_ref, wg_bf, wu_bf, wd_bf, wg_st, wu_st, wd_st, sems,
                    *, layer):
    j = pl.program_id(0)

    def fetch(e, wait):
        for k, (src, dst) in enumerate(((wg_hbm, wg_st), (wu_hbm, wu_st), (wd_hbm, wd_st))):
            cp = pltpu.make_async_copy(src.at[layer, e], dst, sems.at[k])
            cp.wait() if wait else cp.start()

    @pl.when(j < nact_ref[0])
    def _():
        e = tile_e_ref[j]
        prev = tile_e_ref[jnp.maximum(j - 1, 0)]

        @pl.when(j == 0)
        def _():
            fetch(e, wait=False)

        @pl.when((j == 0) | (e != prev))
        def _():
            fetch(e, wait=True)
            wg_bf[...] = wg_st[...].astype(BF16)
            wu_bf[...] = wu_st[...].astype(BF16)
            wd_bf[...] = wd_st[...].astype(BF16)
            nxt = next_e_ref[j]

            @pl.when(nxt >= 0)
            def _():
                fetch(nxt, wait=False)

        x = xs_ref[...].astype(BF16)
        g = jnp.minimum(jnp.dot(x, wg_bf[...], preferred_element_type=F32) + bg_ref[...], SWIGLU_LIMIT)
        u = jnp.clip(jnp.dot(x, wu_bf[...], preferred_element_type=F32) + bu_ref[...],
                     -SWIGLU_LIMIT, SWIGLU_LIMIT)
        a = g * jax.nn.sigmoid(SWIGLU_ALPHA * g) * (u + 1.0)
        y_ref[...] = jnp.dot(a.astype(BF16), wd_bf[...], preferred_element_type=F32) + bd_ref[...]


def _experts(xs, tile_e, tile_blk, nact, next_e, layer, w_gate, b_gate, w_up, b_up, w_down, b_down):
    P, D = xs.shape
    L, E, _, F = w_gate.shape
    tm = EXPERT_TILE
    rows = pl.BlockSpec((tm, D), lambda j, te, tb, na, ne: (tb[j], 0))
    bspec = lambda c: pl.BlockSpec((None, None, 1, c), lambda j, te, tb, na, ne: (layer, te[j], 0, 0))
    hbm = pl.BlockSpec(memory_space=pl.ANY)
    return pl.pallas_call(
        functools.partial(_experts_kernel, layer=layer),
        out_shape=jax.ShapeDtypeStruct((P, D), F32),
        grid_spec=pltpu.PrefetchScalarGridSpec(
            num_scalar_prefetch=4,
            grid=(P // tm,),
            in_specs=[rows, hbm, bspec(F), hbm, bspec(F), hbm, bspec(D)],
            out_specs=rows,
            scratch_shapes=[pltpu.VMEM((D, F), BF16), pltpu.VMEM((D, F), BF16), pltpu.VMEM((F, D), BF16),
                            pltpu.VMEM((D, F), F32), pltpu.VMEM((D, F), F32), pltpu.VMEM((F, D), F32),
                            pltpu.SemaphoreType.DMA((3,))],
        ),
        compiler_params=pltpu.CompilerParams(dimension_semantics=("arbitrary",),
                                             vmem_limit_bytes=VMEM_LIMIT_BYTES),
        name="moe_experts",
    )(tile_e, tile_blk, nact, next_e, xs, w_gate, b_gate.reshape(L, E, 1, F), w_up,
      b_up.reshape(L, E, 1, F), w_down, b_down.reshape(L, E, 1, D))


def _combine_kernel(*refs, final_norm):
    if final_norm:
        (src_ref, cnt_ref, off_ref, y_ref, loc_ref, pw_ref, x_ref, gf_ref, fg_ref, fsh_ref, fsc_ref,
         o_ref, seg_buf, sems) = refs
    else:
        src_ref, cnt_ref, off_ref, y_ref, loc_ref, pw_ref, x_ref, gf_ref, o_ref, seg_buf, sems = refs
    i = pl.program_id(0)
    last = pl.num_programs(0) - 1
    slot = lax.rem(i, 2)
    t = x_ref.shape[0]
    n_loc = seg_buf.shape[1]

    def segment_copies(tile, buf, wait):
        def per_expert(e, carry):
            idx = tile * N_EXPERTS + e
            off = off_ref[idx]
            src = src_ref[idx]

            def chunk(done, rows):
                cp = pltpu.make_async_copy(
                    y_ref.at[pl.ds(pl.multiple_of(src + done, SEG_ALIGN), rows), :],
                    seg_buf.at[buf, pl.ds(pl.multiple_of(off + done, SEG_ALIGN), rows), :], sems.at[buf])
                cp.wait() if wait else cp.start()

            _for_each_chunk(cnt_ref[idx], chunk)
            return carry

        lax.fori_loop(0, N_EXPERTS, per_expert, 0)

    @pl.when(i == 0)
    def _():
        seg_buf[...] = jnp.zeros(seg_buf.shape, F32)
        segment_copies(0, 0, wait=False)

    @pl.when(i < last)
    def _():
        segment_copies(i + 1, 1 - slot, wait=False)

    segment_copies(i, slot, wait=True)
    loc = loc_ref[...].astype(F32)
    pw = pw_ref[...]
    l_id = lax.broadcasted_iota(jnp.int32, (t, n_loc), 1).astype(F32)
    weights = jnp.zeros((t, n_loc), F32)
    for kk in range(TOP_K):
        weights = jnp.where(l_id == loc[:, kk:kk + 1], pw[:, kk:kk + 1], weights)
    f = jnp.dot(weights.astype(BF16), seg_buf[slot].astype(BF16), preferred_element_type=F32)
    x2 = x_ref[...] + gf_ref[...] * f
    if final_norm:
        y = x2 * _rms_scale(x2) * fg_ref[...]
        x2 = y * (1.0 + fsc_ref[...]) + fsh_ref[...]
    o_ref[...] = x2


def _combine(y, loc, pw, src, seg_cnt, seg_off, x, g_f, final):
    B, S, D = x.shape
    t = ROUTE_TILE
    per = S // t
    tok = lambda w: pl.BlockSpec((None, t, w), lambda i, *_: (i // per, i % per, 0))
    vec = pl.BlockSpec((None, 1, D), lambda i, *_: (i // per, 0, 0))
    in_specs = [pl.BlockSpec(memory_space=pl.ANY), tok(LANES), tok(LANES), tok(D), vec]
    args = [y, loc, pw, x, g_f]
    if final is not None:
        fg, fsh, fsc = final
        in_specs += [pl.BlockSpec((1, D), lambda i, *_: (0, 0)), vec, vec]
        args += [fg.reshape(1, D), fsh, fsc]
    return pl.pallas_call(
        functools.partial(_combine_kernel, final_norm=final is not None),
        out_shape=jax.ShapeDtypeStruct((B, S, D), F32),
        grid_spec=pltpu.PrefetchScalarGridSpec(
            num_scalar_prefetch=3,
            grid=(B * per,),
            in_specs=in_specs,
            out_specs=tok(D),
            scratch_shapes=[pltpu.VMEM((2, SEG_BUF_ROWS, D), F32), pltpu.SemaphoreType.DMA((2,))],
        ),
        compiler_params=pltpu.CompilerParams(dimension_semantics=("arbitrary",),
                                             vmem_limit_bytes=VMEM_LIMIT_BYTES),
        name="moe_combine",
    )(src, seg_cnt, seg_off, *args)


def _moe(h, loc, pw, seg, totals, x, g_f, layer, w_gate, b_gate, w_up, b_up, w_down, b_down, final):
    B, S, D = x.shape
    N = B * S
    E = N_EXPERTS
    tm = EXPERT_TILE
    n_tok_tiles = N // ROUTE_TILE
    max_rows = N * TOP_K + n_tok_tiles * E * (SEG_ALIGN - 1)
    n_tiles = -(-max_rows // tm) + E
    tot = totals[0, :E].astype(jnp.int32)
    tiles_per = (tot + tm - 1) // tm
    tile_end = jnp.cumsum(tiles_per)
    row_start = (tile_end - tiles_per) * tm
    nact = tile_end[-1:]
    jj = jnp.minimum(jnp.arange(n_tiles, dtype=jnp.int32), nact[0] - 1)
    tile_e = jnp.sum((jj[:, None] >= tile_end[None, :]).astype(jnp.int32), axis=1)
    place = (row_start[None, :] + seg[:, 0, :E]).reshape(-1)
    seg_cnt = seg[:, 1, :E].reshape(-1)
    seg_off = seg[:, 2, :E].reshape(-1)
    loc2 = loc.reshape(N, LANES)
    xs = _dispatch(h.reshape(N, D), loc2, place, seg_cnt, seg_off, row_start + tot,
                   tiles_per * tm - tot, n_tiles * tm)
    first_at_or_after = jnp.flip(lax.cummin(jnp.flip(jnp.where(tiles_per > 0, jnp.arange(E), E))))
    nxt = jnp.concatenate([first_at_or_after[1:], jnp.full((1,), E, first_at_or_after.dtype)])
    next_e = jnp.where(nxt >= E, -1, nxt).astype(jnp.int32)[tile_e]
    y = _experts(xs, tile_e, jj, nact, next_e, layer, w_gate, b_gate, w_up, b_up, w_down, b_down)
    return _combine(y, loc, pw, place, seg_cnt, seg_off, x, g_f, final)


def _moba_proj_kernel(x_ref, sha_ref, sca_ref, ga_ref, shk_ref, sck_ref, gk_ref, wq_ref, wkv_ref,
                      cos_ref, sin_ref, q_ref, k_ref, v_ref, km_ref):
    x = x_ref[...]
    xn = x * _rms_scale(x)
    h = ((xn * ga_ref[...]) * (1.0 + sca_ref[...]) + sha_ref[...]).astype(BF16)
    hkv = ((xn * gk_ref[...]) * (1.0 + sck_ref[...]) + shk_ref[...]).astype(BF16)
    q = jnp.dot(h, wq_ref[...], preferred_element_type=F32)
    kv = jnp.dot(hkv, wkv_ref[...], preferred_element_type=F32)
    cos = cos_ref[...]
    sin = sin_ref[...]
    scale = LOG2_E * MOBA_HEAD_DIM ** -0.5
    hd = MOBA_HEAD_DIM
    t = x.shape[0]
    width = MOBA_HEADS * hd
    v_ref[...] = kv[:, width:].astype(BF16)
    for hh in range(MOBA_HEADS):
        hs = slice(hh * hd, (hh + 1) * hd)
        qh = q[:, hs]
        kh = kv[:, hs]
        q_ref[:, hs] = ((qh * cos + pltpu.roll(qh, hd // 2, 1) * sin) * scale).astype(BF16)
        kr = kh * cos + pltpu.roll(kh, hd // 2, 1) * sin
        k_ref[:, hs] = kr.astype(BF16)
        km_ref[:, hs] = jnp.mean(kr.reshape(t // MOBA_BLOCK, MOBA_BLOCK, hd), axis=1)


def _moba_proj(x, sh_a, sc_a, g_a, sh_k, sc_k, g_k, w_q, w_kv, cos, sin):
    B, S, D = x.shape
    t = TOKEN_TILE
    H = MOBA_HEADS
    width = H * MOBA_HEAD_DIM
    per = t // MOBA_BLOCK
    assert S // MOBA_BLOCK <= LANES
    tok = lambda w: pl.BlockSpec((None, t, w), lambda b, i: (b, i, 0))
    vec = lambda w: pl.BlockSpec((None, 1, w), lambda b, i: (b, 0, 0))
    full = lambda a: pl.BlockSpec(a.shape, lambda b, i: (0,) * a.ndim)
    ga2 = g_a.reshape(1, D)
    gk2 = g_k.reshape(1, D)
    wq = w_q.astype(BF16)
    wkv = w_kv.astype(BF16)
    rows = jax.ShapeDtypeStruct((B, S, width), BF16)
    q, k, v, km = pl.pallas_call(
        _moba_proj_kernel,
        out_shape=(rows, rows, rows, jax.ShapeDtypeStruct((B, S // t, per, width), F32)),
        grid=(B, S // t),
        in_specs=[tok(D), vec(D), vec(D), full(ga2), vec(D), vec(D), full(gk2), full(wq), full(wkv),
                  tok(LANES), tok(LANES)],
        out_specs=(tok(width), tok(width), tok(width),
                   pl.BlockSpec((None, None, per, width), lambda b, i: (b, i, 0, 0))),
        name="moba_proj",
    )(x, sh_a, sc_a, ga2, sh_k, sc_k, gk2, wq, wkv, cos, sin)
    return q, k, v, km.reshape(B, S // MOBA_BLOCK, width)


def _split_mod(mod, n):
    return [mod[:, None, j * D_MODEL:(j + 1) * D_MODEL] for j in range(n)]


def kernel(x, c, positions, ada_w, ada_b, norm_attn_g, norm_ffn_g, mla_w_in, mla_q_norm_g, mla_w_uq, mla_kv_norm_g, mla_w_ukv, mla_w_o, kv_ada_w, kv_ada_b, kv_norm_g, moba_w_kv, moba_w_q, moba_w_o, router_w, router_b, w_gate, b_gate, w_up, b_up, w_down, b_down, final_ada_w, final_ada_b, final_norm_g):
    mods = _ada_linear(c, ada_w, ada_b)
    kv_mod = _ada_linear(c, kv_ada_w[None], kv_ada_b[None])[0]
    f_mod = _ada_linear(c, final_ada_w[None], final_ada_b[None])[0]
    cos_a, sin_a, cos_b, sin_b = _rope_tables(positions)
    f_sh, f_sc = _split_mod(f_mod, 2)
    shared = None
    for layer in range(DEPTH):
        sh_a, sc_a, g_a, sh_f, sc_f, g_f = _split_mod(mods[layer], 6)
        if layer < N_A:
            q, k, v = _mla_proj(x, sh_a, sc_a, norm_attn_g[layer], mla_w_in[layer], mla_q_norm_g[layer],
                                mla_w_uq[layer], mla_kv_norm_g[layer], mla_w_ukv[layer], cos_a, sin_a)
            attn = _attention(q, k, v, None, heads=MLA_HEADS, dq=ATTN_WIDTH, dv=MLA_V, moba=False)
            w_o = mla_w_o[layer]
        else:
            j = layer - N_A
            kv_sh, kv_sc = _split_mod(kv_mod, 2)
            q, k, v, km = _moba_proj(x, sh_a, sc_a, norm_attn_g[layer], kv_sh, kv_sc, kv_norm_g,
                                     moba_w_q[j], moba_w_kv, cos_b, sin_b)
            if shared is None:
                shared = (k, v, km)
            attn = _attention(q, shared[0], shared[1], shared[2], heads=MOBA_HEADS, dq=MOBA_HEAD_DIM,
                              dv=MOBA_HEAD_DIM, moba=True)
            w_o = moba_w_o[j]
        x, h, loc, pw, seg, totals = _attn_out(attn, w_o, x, g_a, norm_ffn_g[layer], sh_f, sc_f,
                                               router_w[layer], router_b[layer])
        final = (final_norm_g, f_sh, f_sc) if layer == DEPTH - 1 else None
        x = _moe(h, loc, pw, seg, totals, x, g_f, layer, w_gate, b_gate, w_up, b_up, w_down, b_down,
                 final)
    return x
```

```python
import functools

import jax
import jax.numpy as jnp
from jax import lax
from jax.experimental import pallas as pl
from jax.experimental.pallas import tpu as pltpu

D_MODEL = 1024
DEPTH = 2
N_A = DEPTH // 2

MLA_HEADS = 8
MLA_Q_LORA = 256
MLA_KV_LORA = 128
MLA_NOPE = 128
MLA_ROPE = 64
MLA_V = 128

MOBA_HEADS = 8
MOBA_HEAD_DIM = D_MODEL // MOBA_HEADS
MOBA_BLOCK = 256
MOBA_TOPK = 3

N_EXPERTS = 32
TOP_K = 4
SWIGLU_LIMIT = 7.0
SWIGLU_ALPHA = 1.702

ROPE_THETA = 10000.0
NORM_EPS = 1e-6
NEG_INF = -1e30
LOG2_E = 1.4426950408889634
SOFTMAX_M_INIT = -1e29

LANES = 128
TOKEN_TILE = 1024
ATTN_Q_TILE = 1024
ATTN_KV_TILE = 512
ATTN_WIDTH = 2 * LANES
ATTN_HEADS_PER_STEP = 2
EXPERT_TILE = 1024
SEG_ALIGN = 8
SEG_CHUNK = 64
ROUTE_TILE = 512
SEG_BUF_ROWS = ROUTE_TILE * TOP_K + N_EXPERTS * SEG_ALIGN
VMEM_LIMIT_BYTES = 56 * 1024 * 1024

F32 = jnp.float32
BF16 = jnp.bfloat16


def _rms_scale(x):
    return lax.rsqrt(jnp.mean(x * x, axis=-1, keepdims=True) + NORM_EPS)


def _nt_dot(a, b):
    return lax.dot_general(a, b, (((1,), (1,)), ((), ())), preferred_element_type=F32)


def _ones_column(rows):
    lane = lax.broadcasted_iota(jnp.int32, (rows, LANES), 1)
    return jnp.where(lane == 0, 1.0, 0.0).astype(BF16)


def _ada_kernel(c_ref, w_ref, b_ref, o_ref):
    c = c_ref[...]
    ca = c * jax.nn.sigmoid(c)
    o_ref[...] = jnp.dot(ca, w_ref[...], precision=lax.Precision.HIGHEST,
                         preferred_element_type=F32) + b_ref[...]


def _ada_linear(c, w, b):
    L, D, M = w.shape
    B = c.shape[0]
    bn = 2048
    return pl.pallas_call(
        _ada_kernel,
        out_shape=jax.ShapeDtypeStruct((L, B, M), F32),
        grid=(L, M // bn),
        in_specs=[
            pl.BlockSpec((B, D), lambda l, j: (0, 0)),
            pl.BlockSpec((None, D, bn), lambda l, j: (l, 0, j)),
            pl.BlockSpec((None, 1, bn), lambda l, j: (l, 0, j)),
        ],
        out_specs=pl.BlockSpec((None, B, bn), lambda l, j: (l, 0, j)),
        name="ada_linear",
    )(c, w, b.reshape(L, 1, M))


def _rope_table_kernel(pos_ref, inv_ref, sign_ref, ca_ref, sa_ref, cb_ref, sb_ref):
    ang = pos_ref[...] * inv_ref[...]
    lane = lax.broadcasted_iota(jnp.int32, ang.shape, 1)
    na, nb = MLA_ROPE // 2, MOBA_HEAD_DIM // 2

    def spread(v):
        a = jnp.where(lane < na, v, 0.0)
        a = a + pltpu.roll(a, na, 1)
        a = a + pltpu.roll(a, 2 * na, 1)
        b = jnp.where((lane >= na) & (lane < na + nb), v, 0.0)
        b = pltpu.roll(b, LANES - na, 1)
        b = b + pltpu.roll(b, nb, 1)
        return a, b

    ca_ref[...], cb_ref[...] = spread(jnp.cos(ang))
    sa, sb = spread(jnp.sin(ang))
    sa_ref[...] = sa
    sb_ref[...] = sb * sign_ref[...]


def _rope_tables(positions):
    B, S = positions.shape
    t = 1024
    assert MLA_ROPE // 2 + MOBA_HEAD_DIM // 2 <= LANES and MLA_ROPE * 2 == LANES == MOBA_HEAD_DIM

    def inv_freq(d):
        half = d // 2
        return ROPE_THETA ** (-jnp.arange(half, dtype=F32) * (2.0 / d))

    inv = jnp.concatenate([inv_freq(MLA_ROPE), inv_freq(MOBA_HEAD_DIM),
                           jnp.zeros((LANES - MLA_ROPE // 2 - MOBA_HEAD_DIM // 2,), F32)]).reshape(1, LANES)
    half = MOBA_HEAD_DIM // 2
    sign = jnp.concatenate([-jnp.ones((half,), F32), jnp.ones((half,), F32)]).reshape(1, LANES)
    pos = positions.astype(F32).reshape(B, S, 1)
    row = pl.BlockSpec((1, LANES), lambda b, i: (0, 0))
    out = pl.BlockSpec((None, t, LANES), lambda b, i: (b, i, 0))
    shp = jax.ShapeDtypeStruct((B, S, LANES), F32)
    return pl.pallas_call(
        _rope_table_kernel,
        out_shape=(shp, shp, shp, shp),
        grid=(B, S // t),
        in_specs=[pl.BlockSpec((None, t, 1), lambda b, i: (b, i, 0)), row, row],
        out_specs=(out, out, out, out),
        name="rope_tables",
    )(pos, inv, sign)


def _mla_proj_kernel(x_ref, sh_ref, sc_ref, g_ref, win_ref, qg_ref, wqn_ref, wqr_ref, wqrr_ref,
                     kvg_ref, wkn_ref, wv_ref, cos_ref, sin_ref, q_ref, k_ref, v_ref):
    x = x_ref[...]
    y = x * _rms_scale(x) * g_ref[...]
    h = (y * (1.0 + sc_ref[...]) + sh_ref[...]).astype(BF16)
    proj = jnp.dot(h, win_ref[...], preferred_element_type=F32)
    c_q = proj[:, :MLA_Q_LORA]
    c_kv = proj[:, MLA_Q_LORA:MLA_Q_LORA + MLA_KV_LORA]
    kr_a = proj[:, 384:512]
    kr_b = proj[:, 512:640]
    cq = (c_q * _rms_scale(c_q) * qg_ref[...]).astype(BF16)
    ckv = (c_kv * _rms_scale(c_kv) * kvg_ref[...]).astype(BF16)
    cos = cos_ref[...]
    sin = sin_ref[...]
    scale = LOG2_E * (MLA_NOPE + MLA_ROPE) ** -0.5
    kr = (kr_a * cos + kr_b * sin).astype(BF16)
    q_nope = jnp.dot(cq, wqn_ref[...], preferred_element_type=F32) * scale
    q_ra = jnp.dot(cq, wqr_ref[...], preferred_element_type=F32)
    q_rb = jnp.dot(cq, wqrr_ref[...], preferred_element_type=F32)
    k_nope = jnp.dot(ckv, wkn_ref[...], preferred_element_type=F32)
    v_ref[...] = jnp.dot(ckv, wv_ref[...], preferred_element_type=F32).astype(BF16)
    for hh in range(MLA_HEADS):
        hs = slice(hh * LANES, (hh + 1) * LANES)
        lo = slice(hh * ATTN_WIDTH, hh * ATTN_WIDTH + LANES)
        hi = slice(hh * ATTN_WIDTH + LANES, (hh + 1) * ATTN_WIDTH)
        q_ref[:, lo] = q_nope[:, hs].astype(BF16)
        q_ref[:, hi] = ((q_ra[:, hs] * cos + q_rb[:, hs] * sin) * scale).astype(BF16)
        k_ref[:, lo] = k_nope[:, hs].astype(BF16)
        k_ref[:, hi] = kr


def _rot_half_cols(w, half):
    return jnp.concatenate([-w[..., half:], w[..., :half]], axis=-1)


def _mla_proj(x, sh, sc, g, w_in, q_norm_g, w_uq, kv_norm_g, w_ukv, cos, sin):
    B, S, D = x.shape
    H = MLA_HEADS
    t = TOKEN_TILE
    half = MLA_ROPE // 2
    w_kr = w_in[:, MLA_Q_LORA + MLA_KV_LORA:]
    zpad = jnp.zeros((D, LANES - MLA_ROPE), F32)
    w_in_ext = jnp.concatenate(
        [w_in[:, :MLA_Q_LORA + MLA_KV_LORA], w_kr, zpad, _rot_half_cols(w_kr, half), zpad],
        axis=1).astype(BF16)
    wq = w_uq.reshape(MLA_Q_LORA, H, MLA_NOPE + MLA_ROPE)
    wq_nope = wq[..., :MLA_NOPE].reshape(MLA_Q_LORA, H * MLA_NOPE).astype(BF16)
    wq_r = wq[..., MLA_NOPE:]
    pad = ((0, 0), (0, 0), (0, LANES - MLA_ROPE))
    wq_rope = jnp.pad(wq_r, pad).reshape(MLA_Q_LORA, H * LANES).astype(BF16)
    wq_rope_rot = jnp.pad(_rot_half_cols(wq_r, half), pad).reshape(MLA_Q_LORA, H * LANES).astype(BF16)
    wkv = w_ukv.reshape(MLA_KV_LORA, H, MLA_NOPE + MLA_V)
    wk_nope = wkv[..., :MLA_NOPE].reshape(MLA_KV_LORA, H * MLA_NOPE).astype(BF16)
    wv = wkv[..., MLA_NOPE:].reshape(MLA_KV_LORA, H * MLA_V).astype(BF16)

    tok = lambda w: pl.BlockSpec((None, t, w), lambda b, i: (b, i, 0))
    vec = lambda w: pl.BlockSpec((None, 1, w), lambda b, i: (b, 0, 0))
    full = lambda a: pl.BlockSpec(a.shape, lambda b, i: (0,) * a.ndim)
    g2 = g.reshape(1, D)
    qg2 = q_norm_g.reshape(1, MLA_Q_LORA)
    kvg2 = kv_norm_g.reshape(1, MLA_KV_LORA)
    wide = jax.ShapeDtypeStruct((B, S, H * ATTN_WIDTH), BF16)
    return pl.pallas_call(
        _mla_proj_kernel,
        out_shape=(wide, wide, jax.ShapeDtypeStruct((B, S, H * MLA_V), BF16)),
        grid=(B, S // t),
        in_specs=[tok(D), vec(D), vec(D), full(g2), full(w_in_ext), full(qg2), full(wq_nope),
                  full(wq_rope), full(wq_rope_rot), full(kvg2), full(wk_nope), full(wv),
                  tok(LANES), tok(LANES)],
        out_specs=(tok(H * ATTN_WIDTH), tok(H * ATTN_WIDTH), tok(H * MLA_V)),
        name="mla_proj",
    )(x, sh, sc, g2, w_in_ext, qg2, wq_nope, wq_rope, wq_rope_rot, kvg2, wk_nope, wv, cos, sin)


def _attn_kernel(*refs, moba, tq, bk, dq, dv, nh):
    n_in = 4 if moba else 3
    q_ref, k_ref, v_ref = refs[:3]
    o_ref = refs[n_in]
    scratch = refs[n_in + 1:]
    if moba:
        kx_ref, vx_ref = scratch[:2]
        scratch = scratch[2:]
    else:
        kx_ref, vx_ref = k_ref, scratch[0]
        scratch = scratch[1:]
    per_head = len(scratch) // nh
    w = ATTN_WIDTH
    i = pl.program_id(2)
    seq = vx_ref.shape[0]
    row_id = lax.broadcasted_iota(jnp.int32, (tq, 1), 0)
    col_id = lax.broadcasted_iota(jnp.int32, (1, bk), 1)

    @pl.when((pl.program_id(0) == 0) & (pl.program_id(1) == 0) & (i == 0))
    def _():
        ones = _ones_column(seq)
        if moba:
            key_row = lax.broadcasted_iota(jnp.int32, (seq, LANES), 0)
            lane = lax.broadcasted_iota(jnp.int32, (seq, LANES), 1)
            block_id = jnp.where(key_row // MOBA_BLOCK == lane, 1.0, 0.0).astype(BF16)
        for hh in range(nh):
            vx_ref[:, hh * w + dv:(hh + 1) * w] = ones
            if moba:
                kx_ref[:, hh * w + dq:(hh + 1) * w] = block_id

    @pl.when(i == 0)
    def _():
        for hh in range(nh):
            vx_ref[:, hh * w:hh * w + dv] = v_ref[:, hh * dv:(hh + 1) * dv]
            if moba:
                kx_ref[:, hh * w:hh * w + dq] = k_ref[:, hh * dq:(hh + 1) * dq]

    def head_scratch(hh):
        sc = scratch[hh * per_head:(hh + 1) * per_head]
        return sc if moba else (None,) + tuple(sc)

    def prepare(hh):
        qx_ref, s_a, s_b, p_a, p_b, al_a, al_b, m_sc, acc_sc = head_scratch(hh)
        if moba:
            km_ref = refs[3]
            q = q_ref[:, hh * dq:(hh + 1) * dq]
            nb = km_ref.shape[0]
            gate = _nt_dot(km_ref[:, hh * dq:(hh + 1) * dq].astype(BF16), q)
            blk = lax.broadcasted_iota(jnp.int32, (nb, tq), 0)
            blk_f = blk.astype(F32)
            own = (i * tq + lax.broadcasted_iota(jnp.int32, (1, tq), 1)) // MOBA_BLOCK
            gate = jnp.where(blk < own, gate, -jnp.inf)
            sel = jnp.where(blk == own, 1.0, 0.0)
            for _ in range(MOBA_TOPK):
                mx = jnp.max(gate, axis=0, keepdims=True)
                first = jnp.min(jnp.where(gate == mx, blk_f, float(nb)), axis=0, keepdims=True)
                hit = blk_f == first
                sel = jnp.where(hit & (mx > -jnp.inf), 1.0, sel)
                gate = jnp.where(hit, -jnp.inf, gate)
            bias = jnp.where(sel > 0.0, 0.0, NEG_INF)
            bias = jnp.concatenate([bias, jnp.zeros((LANES - nb, tq), F32)], axis=0)
            qx_ref[:, :LANES] = q
            qx_ref[:, LANES:] = bias.T.astype(BF16)
        m_sc[...] = jnp.full(m_sc.shape, SOFTMAX_M_INIT, F32)
        acc_sc[...] = jnp.zeros(acc_sc.shape, F32)
        p_b[...] = jnp.zeros(p_b.shape, BF16)
        al_b[...] = jnp.ones(al_b.shape, F32)

    def scores(hh, j, slot, rows=slice(None)):
        sc = head_scratch(hh)
        qx = sc[0][rows, :] if moba else q_ref[rows, hh * w:(hh + 1) * w]
        start = pl.multiple_of(j * bk, bk)
        sc[1 + slot][rows, :] = _nt_dot(qx, kx_ref[pl.ds(start, bk), hh * w:(hh + 1) * w])

    def soft(hh, slot, diag_offset=None, rows=slice(None)):
        sc = head_scratch(hh)
        s_src, p_dst, al_dst, m_sc = sc[1 + slot], sc[3 + slot], sc[5 + slot], sc[7]
        s = s_src[rows, :]
        if diag_offset is not None:
            s = jnp.where(col_id + diag_offset <= row_id[rows, :], s, -jnp.inf)
        m_prev = m_sc[rows, :]
        m_new = jnp.maximum(m_prev, jnp.max(s, axis=1, keepdims=True))
        p_dst[rows, :] = jnp.exp2(s - m_new).astype(BF16)
        al_dst[rows, :] = jnp.exp2(m_prev - m_new)
        m_sc[rows, :] = m_new

    def fold(hh, j, slot, rows=slice(None)):
        sc = head_scratch(hh)
        p_src, al_src, acc_sc = sc[3 + slot], sc[5 + slot], sc[8]
        start = pl.multiple_of(j * bk, bk)
        acc_sc[rows, :] = al_src[rows, :] * acc_sc[rows, :] + jnp.dot(
            p_src[rows, :], vx_ref[pl.ds(start, bk), hh * w:(hh + 1) * w], preferred_element_type=F32)

    hs = range(nh)
    for hh in hs:
        prepare(hh)
    for hh in hs:
        scores(hh, 0, 0)

    def body(t, carry):
        j = 2 * t
        for hh in hs:
            scores(hh, j + 1, 1)
        for hh in hs:
            soft(hh, 0)
        for hh in hs:
            fold(hh, jnp.maximum(j - 1, 0), 1)
        for hh in hs:
            scores(hh, j + 2, 0)
        for hh in hs:
            soft(hh, 1)
        for hh in hs:
            fold(hh, j, 0)
        return carry

    lax.fori_loop(0, i, body, 0)
    j = 2 * i
    late = slice(bk, tq)
    for hh in hs:
        scores(hh, j + 1, 1, late)
    for hh in hs:
        soft(hh, 0, diag_offset=0)
    for hh in hs:
        fold(hh, jnp.maximum(j - 1, 0), 1)
    for hh in hs:
        soft(hh, 1, diag_offset=bk, rows=late)
    for hh in hs:
        fold(hh, j, 0)
    for hh in hs:
        fold(hh, j + 1, 1, late)
    for hh in hs:
        acc = head_scratch(hh)[8][...]
        o_ref[:, hh * dv:(hh + 1) * dv] = (acc[:, :dv] / acc[:, dv:dv + 1]).astype(o_ref.dtype)


def _attention(q, k, v, k_mean, *, heads, dq, dv, moba):
    B, S, _ = q.shape
    tq, bk, w, nh = ATTN_Q_TILE, ATTN_KV_TILE, ATTN_WIDTH, ATTN_HEADS_PER_STEP
    assert tq == 2 * bk and tq % MOBA_BLOCK == 0 and S % tq == 0 and heads % nh == 0
    in_specs = [
        pl.BlockSpec((None, tq, nh * dq), lambda b, h, i: (b, i, h)),
        pl.BlockSpec((None, S, nh * (dq if moba else w)), lambda b, h, i: (b, 0, h)),
        pl.BlockSpec((None, S, nh * dv), lambda b, h, i: (b, 0, h)),
    ]
    args = [q, k, v]
    scratch = [pltpu.VMEM((S, nh * w), BF16)]
    if moba:
        nb = k_mean.shape[1]
        in_specs.append(pl.BlockSpec((None, nb, nh * dq), lambda b, h, i: (b, 0, h)))
        args.append(k_mean)
        scratch.insert(0, pltpu.VMEM((S, nh * w), BF16))
    for _ in range(nh):
        if moba:
            scratch.append(pltpu.VMEM((tq, w), BF16))
        scratch += [pltpu.VMEM((tq, bk), F32), pltpu.VMEM((tq, bk), F32),
                    pltpu.VMEM((tq, bk), BF16), pltpu.VMEM((tq, bk), BF16),
                    pltpu.VMEM((tq, 1), F32), pltpu.VMEM((tq, 1), F32),
                    pltpu.VMEM((tq, 1), F32), pltpu.VMEM((tq, w), F32)]
    return pl.pallas_call(
        functools.partial(_attn_kernel, moba=moba, tq=tq, bk=bk, dq=dq, dv=dv, nh=nh),
        out_shape=jax.ShapeDtypeStruct((B, S, heads * dv), BF16),
        grid=(B, heads // nh, S // tq),
        in_specs=in_specs,
        out_specs=pl.BlockSpec((None, tq, nh * dv), lambda b, h, i: (b, i, h)),
        scratch_shapes=scratch,
        compiler_params=pltpu.CompilerParams(
            dimension_semantics=("arbitrary", "arbitrary", "arbitrary"),
            vmem_limit_bytes=VMEM_LIMIT_BYTES),
        name="moba_attention" if moba else "mla_attention",
    )(*args)


def _attn_out_kernel(a_ref, wo_ref, x_ref, ga_ref, g_ref, sh_ref, sc_ref, rw_ref, rb_ref,
                     x1_ref, h_ref, loc_ref, pw_ref, seg_ref, tot_ref, run_ref):
    first_step = (pl.program_id(0) == 0) & (pl.program_id(1) == 0)

    @pl.when(first_step)
    def _():
        run_ref[...] = jnp.zeros(run_ref.shape, F32)

    a = jnp.dot(a_ref[...], wo_ref[...], preferred_element_type=F32)
    x1 = x_ref[...] + ga_ref[...] * a
    x1_ref[...] = x1
    y = x1 * _rms_scale(x1) * g_ref[...]
    h = (y * (1.0 + sc_ref[...]) + sh_ref[...]).astype(BF16)
    h_ref[...] = h
    logits = jnp.dot(h, rw_ref[...], preferred_element_type=F32) + rb_ref[...]
    t = logits.shape[0]
    lane_i = lax.broadcasted_iota(jnp.int32, logits.shape, 1)
    lane = lane_i.astype(F32)
    logits = jnp.where(lane_i < N_EXPERTS, logits, -jnp.inf)
    top = None
    den = jnp.zeros((t, 1), F32)
    sel = jnp.zeros(logits.shape, F32)
    hits, ws = [], []
    for _ in range(TOP_K):
        mx = jnp.max(logits, axis=1, keepdims=True)
        first = jnp.min(jnp.where(logits == mx, lane, float(LANES)), axis=1, keepdims=True)
        hit = lane == first
        if top is None:
            top = mx
        w = jnp.exp(mx - top)
        den = den + w
        sel = jnp.where(hit, 1.0, sel)
        logits = jnp.where(hit, -jnp.inf, logits)
        hits.append(hit)
        ws.append(w)
    rt = ROUTE_TILE
    r_id = lax.broadcasted_iota(jnp.int32, (rt, rt), 0)
    c_id = lax.broadcasted_iota(jnp.int32, (rt, rt), 1)
    before = jnp.where(c_id < r_id, 1.0, 0.0).astype(BF16)
    a_id = lax.broadcasted_iota(jnp.int32, (LANES, LANES), 0)
    b_id = lax.broadcasted_iota(jnp.int32, (LANES, LANES), 1)
    earlier = jnp.where(a_id < b_id, 1.0, 0.0).astype(BF16)
    row8 = lax.broadcasted_iota(jnp.int32, (8, LANES), 0)
    local = []
    for sub in range(t // rt):
        sel_s = sel[sub * rt:(sub + 1) * rt]
        prefix = jnp.dot(before, sel_s.astype(BF16), preferred_element_type=F32)
        cnt = jnp.sum(sel_s, axis=0, keepdims=True)
        units = jnp.floor((cnt + (SEG_ALIGN - 1.0)) * (1.0 / SEG_ALIGN))
        cnt_pad = units * SEG_ALIGN
        seg_off = jnp.dot(jnp.broadcast_to(units, (8, LANES)).astype(BF16), earlier,
                          preferred_element_type=F32)[0:1] * SEG_ALIGN
        base = run_ref[...]
        run_ref[...] = base + cnt_pad
        seg = jnp.where(row8 == 0, base, jnp.where(row8 == 1, cnt_pad, jnp.where(row8 == 2, seg_off, 0.0)))
        seg_ref[sub] = seg.astype(jnp.int32)
        local.append(prefix + seg_off)
    tot_ref[...] = run_ref[...]
    local = jnp.concatenate(local, axis=0)
    loc = jnp.zeros(logits.shape, F32)
    pw = jnp.zeros(logits.shape, F32)
    for kk in range(TOP_K):
        row = jnp.sum(jnp.where(hits[kk], local, 0.0), axis=1, keepdims=True)
        loc = jnp.where(lane_i == kk, row, loc)
        pw = jnp.where(lane_i == kk, ws[kk] / den, pw)
    loc_ref[...] = loc.astype(jnp.int32)
    pw_ref[...] = pw


def _attn_out(attn, w_o, x, g_a, norm_g, sh_f, sc_f, router_w, router_b):
    B, S, D = x.shape
    t = TOKEN_TILE
    per = S // t
    sub = t // ROUTE_TILE
    rw = jnp.pad(router_w, ((0, 0), (0, LANES - N_EXPERTS))).astype(BF16)
    rb = jnp.pad(router_b, (0, LANES - N_EXPERTS)).reshape(1, LANES)
    wo = w_o.astype(BF16)
    g2 = norm_g.reshape(1, D)
    tok = lambda w: pl.BlockSpec((None, t, w), lambda b, i: (b, i, 0))
    vec = lambda w: pl.BlockSpec((None, 1, w), lambda b, i: (b, 0, 0))
    full = lambda a: pl.BlockSpec(a.shape, lambda b, i: (0,) * a.ndim)
    return pl.pallas_call(
        _attn_out_kernel,
        out_shape=(jax.ShapeDtypeStruct((B, S, D), F32),
                   jax.ShapeDtypeStruct((B, S, D), BF16),
                   jax.ShapeDtypeStruct((B, S, LANES), jnp.int32),
                   jax.ShapeDtypeStruct((B, S, LANES), F32),
                   jax.ShapeDtypeStruct((B * per * sub, 8, LANES), jnp.int32),
                   jax.ShapeDtypeStruct((1, LANES), F32)),
        grid=(B, per),
        in_specs=[tok(attn.shape[-1]), full(wo), tok(D), vec(D), full(g2), vec(D), vec(D),
                  full(rw), full(rb)],
        out_specs=(tok(D), tok(D), tok(LANES), tok(LANES),
                   pl.BlockSpec((sub, 8, LANES), lambda b, i: (b * per + i, 0, 0)),
                   pl.BlockSpec((1, LANES), lambda b, i: (0, 0))),
        scratch_shapes=[pltpu.VMEM((1, LANES), F32)],
        compiler_params=pltpu.CompilerParams(dimension_semantics=("arbitrary", "arbitrary")),
        name="attn_out_router",
    )(attn, wo, x, g_a, g2, sh_f, sc_f, rw, rb)


def _for_each_chunk(cnt, fn):
    shift = SEG_CHUNK.bit_length() - 1
    whole = cnt >> shift

    def piece(c, carry):
        fn(pl.multiple_of(c * SEG_CHUNK, SEG_CHUNK), SEG_CHUNK)
        return carry

    lax.fori_loop(0, whole, piece, 0)
    rows = SEG_CHUNK // 2
    while rows >= SEG_ALIGN:
        shift = rows.bit_length()
        done = (cnt >> shift) << shift

        @pl.when((cnt & rows) != 0)
        def _(done=done, rows=rows):
            fn(pl.multiple_of(done, SEG_ALIGN), rows)

        rows //= 2


def _dispatch_kernel(dst_ref, cnt_ref, off_ref, pad_start_ref, pad_cnt_ref, h_ref, loc_ref,
                     xs_ref, seg_buf, zero_buf, sems):
    i = pl.program_id(0)
    last = pl.num_programs(0) - 1
    slot = lax.rem(i, 2)
    t = h_ref.shape[0]
    n_loc = seg_buf.shape[1]
    pad_sem = sems.at[2]

    @pl.when(i == 0)
    def _():
        zero_buf[...] = jnp.zeros(zero_buf.shape, F32)
        for wait in (False, True):
            def per_expert(e, carry, wait=wait):
                start = pad_start_ref[e]

                def chunk(done, rows):
                    cp = pltpu.make_async_copy(
                        zero_buf.at[pl.ds(0, rows), :],
                        xs_ref.at[pl.ds(pl.multiple_of(start + done, SEG_ALIGN), rows), :], pad_sem)
                    cp.wait() if wait else cp.start()

                _for_each_chunk(pad_cnt_ref[e], chunk)
                return carry

            lax.fori_loop(0, N_EXPERTS, per_expert, 0)

    def segment_copies(tile, buf, wait):
        def per_expert(e, carry):
            idx = tile * N_EXPERTS + e
            off = off_ref[idx]
            dst = dst_ref[idx]

            def chunk(done, rows):
                cp = pltpu.make_async_copy(
                    seg_buf.at[buf, pl.ds(pl.multiple_of(off + done, SEG_ALIGN), rows), :],
                    xs_ref.at[pl.ds(pl.multiple_of(dst + done, SEG_ALIGN), rows), :], sems.at[buf])
                cp.wait() if wait else cp.start()

            _for_each_chunk(cnt_ref[idx], chunk)
            return carry

        lax.fori_loop(0, N_EXPERTS, per_expert, 0)

    loc_t = loc_ref[...].astype(F32).T
    l_id = lax.broadcasted_iota(jnp.int32, (n_loc, t), 0).astype(F32)
    onehot = jnp.zeros((n_loc, t), F32)
    for kk in range(TOP_K):
        onehot = jnp.where(l_id == loc_t[kk:kk + 1, :], 1.0, onehot)
    seg_buf[slot] = jnp.dot(onehot.astype(BF16), h_ref[...], preferred_element_type=F32)
    segment_copies(i, slot, wait=False)

    @pl.when(i > 0)
    def _():
        segment_copies(i - 1, 1 - slot, wait=True)

    @pl.when(i == last)
    def _():
        segment_copies(i, slot, wait=True)


def _dispatch(h, loc, dst, seg_cnt, seg_off, pad_start, pad_cnt, n_rows):
    N, D = h.shape
    t = ROUTE_TILE
    return pl.pallas_call(
        _dispatch_kernel,
        out_shape=jax.ShapeDtypeStruct((n_rows, D), F32),
        grid_spec=pltpu.PrefetchScalarGridSpec(
            num_scalar_prefetch=5,
            grid=(N // t,),
            in_specs=[pl.BlockSpec((t, D), lambda i, *_: (i, 0)),
                      pl.BlockSpec((t, LANES), lambda i, *_: (i, 0))],
            out_specs=pl.BlockSpec(memory_space=pl.ANY),
            scratch_shapes=[pltpu.VMEM((2, SEG_BUF_ROWS, D), F32),
                            pltpu.VMEM((SEG_CHUNK, D), F32),
                            pltpu.SemaphoreType.DMA((3,))],
        ),
        compiler_params=pltpu.CompilerParams(dimension_semantics=("arbitrary",),
                                             vmem_limit_bytes=VMEM_LIMIT_BYTES),
        name="moe_dispatch",
    )(dst, seg_cnt, seg_off, pad_start, pad_cnt, h, loc)


def _experts_kernel(tile_e_ref, tile_blk_ref, nact_ref, next_e_ref, xs_ref, wg_hbm, bg_ref, wu_hbm,
                    bu_ref, wd_hbm, bd_ref, y_ref, wg_bf, wu_bf, wd_bf, wg_st, wu_st, wd_st, sems,
                    *, layer):
    j = pl.program_id(0)

    def fetch(e, wait):
        for k, (src, dst) in enumerate(((wg_hbm, wg_st), (wu_hbm, wu_st), (wd_hbm, wd_st))):
            cp = pltpu.make_async_copy(src.at[layer, e], dst, sems.at[k])
            cp.wait() if wait else cp.start()

    @pl.when(j < nact_ref[0])
    def _():
        e = tile_e_ref[j]
        prev = tile_e_ref[jnp.maximum(j - 1, 0)]

        @pl.when(j == 0)
        def _():
            fetch(e, wait=False)

        @pl.when((j == 0) | (e != prev))
        def _():
            fetch(e, wait=True)
            wg_bf[...] = wg_st[...].astype(BF16)
            wu_bf[...] = wu_st[...].astype(BF16)
            wd_bf[...] = wd_st[...].astype(BF16)
            nxt = next_e_ref[j]

            @pl.when(nxt >= 0)
            def _():
                fetch(nxt, wait=False)

        x = xs_ref[...].astype(BF16)
        g = jnp.minimum(jnp.dot(x, wg_bf[...], preferred_element_type=F32) + bg_ref[...], SWIGLU_LIMIT)
        u = jnp.clip(jnp.dot(x, wu_bf[...], preferred_element_type=F32) + bu_ref[...],
                     -SWIGLU_LIMIT, SWIGLU_LIMIT)
        a = g * jax.nn.sigmoid(SWIGLU_ALPHA * g) * (u + 1.0)
        y_ref[...] = jnp.dot(a.astype(BF16), wd_bf[...], preferred_element_type=F32) + bd_ref[...]


def _experts(xs, tile_e, tile_blk, nact, next_e, layer, w_gate, b_gate, w_up, b_up, w_down, b_down):
    P, D = xs.shape
    L, E, _, F = w_gate.shape
    tm = EXPERT_TILE
    rows = pl.BlockSpec((tm, D), lambda j, te, tb, na, ne: (tb[j], 0))
    bspec = lambda c: pl.BlockSpec((None, None, 1, c), lambda j, te, tb, na, ne: (layer, te[j], 0, 0))
    hbm = pl.BlockSpec(memory_space=pl.ANY)
    return pl.pallas_call(
        functools.partial(_experts_kernel, layer=layer),
        out_shape=jax.ShapeDtypeStruct((P, D), F32),
        grid_spec=pltpu.PrefetchScalarGridSpec(
            num_scalar_prefetch=4,
            grid=(P // tm,),
            in_specs=[rows, hbm, bspec(F), hbm, bspec(F), hbm, bspec(D)],
            out_specs=rows,
            scratch_shapes=[pltpu.VMEM((D, F), BF16), pltpu.VMEM((D, F), BF16), pltpu.VMEM((F, D), BF16),
                            pltpu.VMEM((D, F), F32), pltpu.VMEM((D, F), F32), pltpu.VMEM((F, D), F32),
                            pltpu.SemaphoreType.DMA((3,))],
        ),
        compiler_params=pltpu.CompilerParams(dimension_semantics=("arbitrary",),
                                             vmem_limit_bytes=VMEM_LIMIT_BYTES),
        name="moe_experts",
    )(tile_e, tile_blk, nact, next_e, xs, w_gate, b_gate.reshape(L, E, 1, F), w_up,
      b_up.reshape(L, E, 1, F), w_down, b_down.reshape(L, E, 1, D))


def _combine_kernel(*refs, final_norm):
    if final_norm:
        (src_ref, cnt_ref, off_ref, y_ref, loc_ref, pw_ref, x_ref, gf_ref, fg_ref, fsh_ref, fsc_ref,
         o_ref, seg_buf, sems) = refs
    else:
        src_ref, cnt_ref, off_ref, y_ref, loc_ref, pw_ref, x_ref, gf_ref, o_ref, seg_buf, sems = refs
    i = pl.program_id(0)
    last = pl.num_programs(0) - 1
    slot = lax.rem(i, 2)
    t = x_ref.shape[0]
    n_loc = seg_buf.shape[1]

    def segment_copies(tile, buf, wait):
        def per_expert(e, carry):
            idx = tile * N_EXPERTS + e
            off = off_ref[idx]
            src = src_ref[idx]

            def chunk(done, rows):
                cp = pltpu.make_async_copy(
                    y_ref.at[pl.ds(pl.multiple_of(src + done, SEG_ALIGN), rows), :],
                    seg_buf.at[buf, pl.ds(pl.multiple_of(off + done, SEG_ALIGN), rows), :], sems.at[buf])
                cp.wait() if wait else cp.start()

            _for_each_chunk(cnt_ref[idx], chunk)
            return carry

        lax.fori_loop(0, N_EXPERTS, per_expert, 0)

    @pl.when(i == 0)
    def _():
        seg_buf[...] = jnp.zeros(seg_buf.shape, F32)
        segment_copies(0, 0, wait=False)

    @pl.when(i < last)
    def _():
        segment_copies(i + 1, 1 - slot, wait=False)

    segment_copies(i, slot, wait=True)
    loc = loc_ref[...].astype(F32)
    pw = pw_ref[...]
    l_id = lax.broadcasted_iota(jnp.int32, (t, n_loc), 1).astype(F32)
    weights = jnp.zeros((t, n_loc), F32)
    for kk in range(TOP_K):
        weights = jnp.where(l_id == loc[:, kk:kk + 1], pw[:, kk:kk + 1], weights)
    f = jnp.dot(weights.astype(BF16), seg_buf[slot].astype(BF16), preferred_element_type=F32)
    x2 = x_ref[...] + gf_ref[...] * f
    if final_norm:
        y = x2 * _rms_scale(x2) * fg_ref[...]
        x2 = y * (1.0 + fsc_ref[...]) + fsh_ref[...]
    o_ref[...] = x2


def _combine(y, loc, pw, src, seg_cnt, seg_off, x, g_f, final):
    B, S, D = x.shape
    t = ROUTE_TILE
    per = S // t
    tok = lambda w: pl.BlockSpec((None, t, w), lambda i, *_: (i // per, i % per, 0))
    vec = pl.BlockSpec((None, 1, D), lambda i, *_: (i // per, 0, 0))
    in_specs = [pl.BlockSpec(memory_space=pl.ANY), tok(LANES), tok(LANES), tok(D), vec]
    args = [y, loc, pw, x, g_f]
    if final is not None:
        fg, fsh, fsc = final
        in_specs += [pl.BlockSpec((1, D), lambda i, *_: (0, 0)), vec, vec]
        args += [fg.reshape(1, D), fsh, fsc]
    return pl.pallas_call(
        functools.partial(_combine_kernel, final_norm=final is not None),
        out_shape=jax.ShapeDtypeStruct((B, S, D), F32),
        grid_spec=pltpu.PrefetchScalarGridSpec(
            num_scalar_prefetch=3,
            grid=(B * per,),
            in_specs=in_specs,
            out_specs=tok(D),
            scratch_shapes=[pltpu.VMEM((2, SEG_BUF_ROWS, D), F32), pltpu.SemaphoreType.DMA((2,))],
        ),
        compiler_params=pltpu.CompilerParams(dimension_semantics=("arbitrary",),
                                             vmem_limit_bytes=VMEM_LIMIT_BYTES),
        name="moe_combine",
    )(src, seg_cnt, seg_off, *args)


def _moe(h, loc, pw, seg, totals, x, g_f, layer, w_gate, b_gate, w_up, b_up, w_down, b_down, final):
    B, S, D = x.shape
    N = B * S
    E = N_EXPERTS
    tm = EXPERT_TILE
    n_tok_tiles = N // ROUTE_TILE
    max_rows = N * TOP_K + n_tok_tiles * E * (SEG_ALIGN - 1)
    n_tiles = -(-max_rows // tm) + E
    tot = totals[0, :E].astype(jnp.int32)
    tiles_per = (tot + tm - 1) // tm
    tile_end = jnp.cumsum(tiles_per)
    row_start = (tile_end - tiles_per) * tm
    nact = tile_end[-1:]
    jj = jnp.minimum(jnp.arange(n_tiles, dtype=jnp.int32), nact[0] - 1)
    tile_e = jnp.sum((jj[:, None] >= tile_end[None, :]).astype(jnp.int32), axis=1)
    place = (row_start[None, :] + seg[:, 0, :E]).reshape(-1)
    seg_cnt = seg[:, 1, :E].reshape(-1)
    seg_off = seg[:, 2, :E].reshape(-1)
    loc2 = loc.reshape(N, LANES)
    xs = _dispatch(h.reshape(N, D), loc2, place, seg_cnt, seg_off, row_start + tot,
                   tiles_per * tm - tot, n_tiles * tm)
    first_at_or_after = jnp.flip(lax.cummin(jnp.flip(jnp.where(tiles_per > 0, jnp.arange(E), E))))
    nxt = jnp.concatenate([first_at_or_after[1:], jnp.full((1,), E, first_at_or_after.dtype)])
    next_e = jnp.where(nxt >= E, -1, nxt).astype(jnp.int32)[tile_e]
    y = _experts(xs, tile_e, jj, nact, next_e, layer, w_gate, b_gate, w_up, b_up, w_down, b_down)
    return _combine(y, loc, pw, place, seg_cnt, seg_off, x, g_f, final)


def _moba_proj_kernel(x_ref, sha_ref, sca_ref, ga_ref, shk_ref, sck_ref, gk_ref, wq_ref, wkv_ref,
                      cos_ref, sin_ref, q_ref, k_ref, v_ref, km_ref):
    x = x_ref[...]
    xn = x * _rms_scale(x)
    h = ((xn * ga_ref[...]) * (1.0 + sca_ref[...]) + sha_ref[...]).astype(BF16)
    hkv = ((xn * gk_ref[...]) * (1.0 + sck_ref[...]) + shk_ref[...]).astype(BF16)
    q = jnp.dot(h, wq_ref[...], preferred_element_type=F32)
    kv = jnp.dot(hkv, wkv_ref[...], preferred_element_type=F32)
    cos = cos_ref[...]
    sin = sin_ref[...]
    scale = LOG2_E * MOBA_HEAD_DIM ** -0.5
    hd = MOBA_HEAD_DIM
    t = x.shape[0]
    width = MOBA_HEADS * hd
    v_ref[...] = kv[:, width:].astype(BF16)
    for hh in range(MOBA_HEADS):
        hs = slice(hh * hd, (hh + 1) * hd)
        qh = q[:, hs]
        kh = kv[:, hs]
        q_ref[:, hs] = ((qh * cos + pltpu.roll(qh, hd // 2, 1) * sin) * scale).astype(BF16)
        kr = kh * cos + pltpu.roll(kh, hd // 2, 1) * sin
        k_ref[:, hs] = kr.astype(BF16)
        km_ref[:, hs] = jnp.mean(kr.reshape(t // MOBA_BLOCK, MOBA_BLOCK, hd), axis=1)


def _moba_proj(x, sh_a, sc_a, g_a, sh_k, sc_k, g_k, w_q, w_kv, cos, sin):
    B, S, D = x.shape
    t = TOKEN_TILE
    H = MOBA_HEADS
    width = H * MOBA_HEAD_DIM
    per = t // MOBA_BLOCK
    assert S // MOBA_BLOCK <= LANES
    tok = lambda w: pl.BlockSpec((None, t, w), lambda b, i: (b, i, 0))
    vec = lambda w: pl.BlockSpec((None, 1, w), lambda b, i: (b, 0, 0))
    full = lambda a: pl.BlockSpec(a.shape, lambda b, i: (0,) * a.ndim)
    ga2 = g_a.reshape(1, D)
    gk2 = g_k.reshape(1, D)
    wq = w_q.astype(BF16)
    wkv = w_kv.astype(BF16)
    rows = jax.ShapeDtypeStruct((B, S, width), BF16)
    q, k, v, km = pl.pallas_call(
        _moba_proj_kernel,
        out_shape=(rows, rows, rows, jax.ShapeDtypeStruct((B, S // t, per, width), F32)),
        grid=(B, S // t),
        in_specs=[tok(D), vec(D), vec(D), full(ga2), vec(D), vec(D), full(gk2), full(wq), full(wkv),
                  tok(LANES), tok(LANES)],
        out_specs=(tok(width), tok(width), tok(width),
                   pl.BlockSpec((None, None, per, width), lambda b, i: (b, i, 0, 0))),
        name="moba_proj",
    )(x, sh_a, sc_a, ga2, sh_k, sc_k, gk2, wq, wkv, cos, sin)
    return q, k, v, km.reshape(B, S // MOBA_BLOCK, width)


def _split_mod(mod, n):
    return [mod[:, None, j * D_MODEL:(j + 1) * D_MODEL] for j in range(n)]


def kernel(x, c, positions, ada_w, ada_b, norm_attn_g, norm_ffn_g, mla_w_in, mla_q_norm_g, mla_w_uq, mla_kv_norm_g, mla_w_ukv, mla_w_o, kv_ada_w, kv_ada_b, kv_norm_g, moba_w_kv, moba_w_q, moba_w_o, router_w, router_b, w_gate, b_gate, w_up, b_up, w_down, b_down, final_ada_w, final_ada_b, final_norm_g):
    mods = _ada_linear(c, ada_w, ada_b)
    kv_mod = _ada_linear(c, kv_ada_w[None], kv_ada_b[None])[0]
    f_mod = _ada_linear(c, final_ada_w[None], final_ada_b[None])[0]
    cos_a, sin_a, cos_b, sin_b = _rope_tables(positions)
    f_sh, f_sc = _split_mod(f_mod, 2)
    shared = None
    for layer in range(DEPTH):
        sh_a, sc_a, g_a, sh_f, sc_f, g_f = _split_mod(mods[layer], 6)
        if layer < N_A:
            q, k, v = _mla_proj(x, sh_a, sc_a, norm_attn_g[layer], mla_w_in[layer], mla_q_norm_g[layer],
                                mla_w_uq[layer], mla_kv_norm_g[layer], mla_w_ukv[layer], cos_a, sin_a)
            attn = _attention(q, k, v, None, heads=MLA_HEADS, dq=ATTN_WIDTH, dv=MLA_V, moba=False)
            w_o = mla_w_o[layer]
        else:
            j = layer - N_A
            kv_sh, kv_sc = _split_mod(kv_mod, 2)
            q, k, v, km = _moba_proj(x, sh_a, sc_a, norm_attn_g[layer], kv_sh, kv_sc, kv_norm_g,
                                     moba_w_q[j], moba_w_kv, cos_b, sin_b)
            if shared is None:
                shared = (k, v, km)
            attn = _attention(q, shared[0], shared[1], shared[2], heads=MOBA_HEADS, dq=MOBA_HEAD_DIM,
                              dv=MOBA_HEAD_DIM, moba=True)
            w_o = moba_w_o[j]
        x, h, loc, pw, seg, totals = _attn_out(attn, w_o, x, g_a, norm_ffn_g[layer], sh_f, sc_f,
                                               router_w[layer], router_b[layer])
        final = (final_norm_g, f_sh, f_sc) if layer == DEPTH - 1 else None
        x = _moe(h, loc, pw, seg, totals, x, g_f, layer, w_gate, b_gate, w_up, b_up, w_down, b_down,
                 final)
    return x
```
